```python
import jax
import jax.numpy as jnp
from jax import lax
import numpy as np

D_MODEL = 1024
BATCH = 8
SEQ = 8192
DEPTH = 1
DEC_BATCH = 16
DEC_SEQ = 64
PAST_LEN = 1024

CHUNK = 64
EPS = 1e-6
ROPE_THETA = 10000.0
N_HEADS_A = 8
N_KV_A = 2
GROUP_A = N_HEADS_A // N_KV_A
HD_A = 64
D_ATT_A = N_HEADS_A * HD_A
D_KV_A = N_KV_A * HD_A
N_IDX_HEADS = 8
D_IDX = 32
D_QIDX = N_IDX_HEADS * D_IDX
IDX_SCALE = (N_IDX_HEADS * D_IDX) ** -0.5
TOPK_MAX = 256
Q_BLOCK = 128
POOL_WINDOWS = (2, 4, 8, 16)
N_POOL_GROUPS = 4
POOL_GROUP = 128
D_POOL = N_POOL_GROUPS * POOL_GROUP
POOL_HIST = 15
N_MEM = 256
N_HEADS_M = 4
HD_M = 128
D_MEM_ATT = N_HEADS_M * HD_M
N_BRANCH = 3
D_FF = -(-(8 * D_MODEL) // (3 * 256)) * 256
IN_WIDTHS = (D_ATT_A, D_KV_A, D_KV_A, D_QIDX, D_IDX, N_IDX_HEADS, D_POOL, D_MEM_ATT, N_BRANCH * D_MODEL)
D_IN = D_ATT_A + 2 * D_KV_A + D_QIDX + D_IDX + N_IDX_HEADS + D_POOL + D_MEM_ATT + N_BRANCH * D_MODEL

kernel_name = 'hybrid_dsa_pool_memory_streaming_step'


def rmsnorm(x, g):
    x32 = x.astype(jnp.float32)
    y = x32 * lax.rsqrt(jnp.mean(x32 * x32, axis=-1, keepdims=True) + EPS)
    return (y * g.astype(jnp.float32)).astype(x.dtype)


def rope(x, pos):
    half = x.shape[-1] // 2
    inv = ROPE_THETA ** (-jnp.arange(half, dtype=jnp.float32) / half)
    ang = pos.astype(jnp.float32)[:, None] * inv[None, :]
    cos = jnp.cos(ang)[None, :, None, :]
    sin = jnp.sin(ang)[None, :, None, :]
    x32 = x.astype(jnp.float32)
    x1, x2 = x32[..., :half], x32[..., half:]
    return jnp.concatenate([x1 * cos - x2 * sin, x2 * cos + x1 * sin], axis=-1).astype(x.dtype)


def project_inputs(x, pos, g_mix, w_in, g_qa, g_ka, g_kidx, g_qm):
    B, T, _ = x.shape
    h = rmsnorm(x, g_mix)
    p = h @ w_in
    cuts = [int(c) for c in np.cumsum(IN_WIDTHS)[:-1]]
    qa, ka, va, qi, ki, wi, ub, qm, gates = jnp.split(p, cuts, axis=-1)
    qa = rope(rmsnorm(qa.reshape(B, T, N_HEADS_A, HD_A), g_qa), pos)
    ka = rope(rmsnorm(ka.reshape(B, T, N_KV_A, HD_A), g_ka), pos)
    va = va.reshape(B, T, N_KV_A, HD_A)
    qi = rope(qi.reshape(B, T, N_IDX_HEADS, D_IDX), pos)
    ki = rope(rmsnorm(ki, g_kidx)[:, :, None, :], pos)[:, :, 0, :]
    wi = wi * IDX_SCALE
    qm = rmsnorm(qm.reshape(B, T, N_HEADS_M, HD_M), g_qm)
    return qa, ka, va, qi, ki, wi, ub, qm, gates


def dsa_attend(qa, qi, wi, qpos, ka, va, ki, topk):
    B, Tq = qa.shape[:2]
    n_keys = ka.shape[1]
    s = jax.nn.relu(jnp.einsum('bthd,bsd->bths', qi, ki, preferred_element_type=jnp.float32))
    score = jnp.einsum('bths,bth->bts', s, wi.astype(jnp.float32))
    q_chunk = qpos // CHUNK
    adm = (jnp.arange(n_keys) // CHUNK)[None, :] <= q_chunk[:, None]
    score = jnp.where(adm[None], score, -jnp.inf)
    _, sel = lax.top_k(score, topk)
    gather_rows = jax.vmap(lambda rows, i: rows[i])
    k_sel = gather_rows(ka, sel)
    v_sel = gather_rows(va, sel)
    valid = (sel // CHUNK) <= q_chunk[None, :, None]
    q = qa.reshape(B, Tq, N_KV_A, GROUP_A, HD_A)
    logits = jnp.einsum('bthgd,btjhd->bthgj', q, k_sel, preferred_element_type=jnp.float32) * (HD_A ** -0.5)
    logits = jnp.where(valid[:, :, None, None, :], logits, -jnp.inf)
    probs = jax.nn.softmax(logits, axis=-1).astype(va.dtype)
    out = jnp.einsum('bthgj,btjhd->bthgd', probs, v_sel)
    return out.reshape(B, Tq, D_ATT_A)


def dsa_prompt(qa, qi, wi, pos, ka, va, ki, topk):
    B, T = qa.shape[:2]
    nblk = T // Q_BLOCK

    def to_blocks(a):
        return jnp.moveaxis(a.reshape((B, nblk, Q_BLOCK) + a.shape[2:]), 1, 0)

    def one_block(args):
        qa_b, qi_b, wi_b, pos_b = args
        return dsa_attend(qa_b, qi_b, wi_b, pos_b, ka, va, ki, topk)

    out = lax.map(one_block, (to_blocks(qa), to_blocks(qi), to_blocks(wi), pos.reshape(nblk, Q_BLOCK)))
    return jnp.moveaxis(out, 0, 1).reshape(B, T, D_ATT_A)


def pool_mix(u_ext, pos0, w_pool, s_pool):
    B, n, _ = u_ext.shape
    T = n - POOL_HIST
    u = u_ext[:, POOL_HIST:]
    cs = jnp.cumsum(u_ext.astype(jnp.float32), axis=1)
    cs = jnp.concatenate([jnp.zeros_like(cs[:, :1]), cs], axis=1)
    pos = pos0 + jnp.arange(T, dtype=jnp.int32)
    means = []
    for g, w in enumerate(POOL_WINDOWS):
        c = slice(g * POOL_GROUP, (g + 1) * POOL_GROUP)
        win = cs[:, POOL_HIST + 1:, c] - cs[:, POOL_HIST + 1 - w:POOL_HIST + 1 - w + T, c]
        cnt = jnp.minimum(w, pos + 1).astype(jnp.float32)[None, :, None]
        means.append(win / cnt)
    mean = jnp.concatenate(means, axis=-1)
    p = (mean - u.astype(jnp.float32)).astype(u.dtype).reshape(B, T, N_POOL_GROUPS, POOL_GROUP)
    y = jnp.einsum('btgc,gce->btge', p, w_pool).reshape(B, T, D_POOL)
    return y * s_pool


def memory_kv(mem, g_mem, w_mem_kv, g_km):
    B, M, _ = mem.shape
    kv = rmsnorm(mem, g_mem) @ w_mem_kv
    k, v = jnp.split(kv, 2, axis=-1)
    k = rmsnorm(k.reshape(B, M, N_HEADS_M, HD_M), g_km)
    return k, v.reshape(B, M, N_HEADS_M, HD_M)


def memory_attend(qm, mk, mv):
    B, T = qm.shape[:2]
    logits = jnp.einsum('bthd,bmhd->bhtm', qm, mk, preferred_element_type=jnp.float32) * (HD_M ** -0.5)
    probs = jax.nn.softmax(logits, axis=-1).astype(mv.dtype)
    return jnp.einsum('bhtm,bmhd->bthd', probs, mv).reshape(B, T, D_MEM_ATT)


def merge_and_ffn(x, a, b, m, gates, w_oa, w_ob, w_om, w_out, g_ffn, w_gate, w_up, w_down):
    ga, gb, gm = jnp.split(jax.nn.sigmoid(gates.astype(jnp.float32)).astype(x.dtype), N_BRANCH, axis=-1)
    mixed = ga * (a @ w_oa) + gb * (b @ w_ob) + gm * (m @ w_om)
    x = x + mixed @ w_out
    h = rmsnorm(x, g_ffn)
    return x + (jax.nn.silu(h @ w_gate) * (h @ w_up)) @ w_down


def setup_inputs(seed: int = 0) -> dict:
    key = jax.random.key(seed)
    ks = jax.random.split(key, 32)
    f32 = jnp.float32
    L = DEPTH

    def nrm(k, shape, scale):
        return jax.random.normal(k, shape, f32) * scale

    def gain(k, shape):
        return 1.0 + 0.1 * jax.random.normal(k, shape, f32)

    return {
        'x_prompt': nrm(ks[0], (BATCH, SEQ, D_MODEL), 1.0),
        'x_sample': nrm(ks[1], (DEC_BATCH, DEC_SEQ, D_MODEL), 1.0),
        'mem_prompt': nrm(ks[2], (BATCH, N_MEM, D_MODEL), 1.0),
        'cache_a_k': nrm(ks[3], (L, DEC_BATCH, PAST_LEN, N_KV_A, HD_A), 1.0),
        'cache_a_v': nrm(ks[4], (L, DEC_BATCH, PAST_LEN, N_KV_A, HD_A), 1.0),
        'cache_idx_k': nrm(ks[5], (L, DEC_BATCH, PAST_LEN, D_IDX), 1.0),
        'cache_pool': nrm(ks[6], (L, DEC_BATCH, POOL_HIST, D_POOL), 1.0),
        'cache_mem_k': nrm(ks[7], (L, DEC_BATCH, N_MEM, N_HEADS_M, HD_M), 1.0),
        'cache_mem_v': nrm(ks[8], (L, DEC_BATCH, N_MEM, N_HEADS_M, HD_M), 1.0),
        'g_mix': gain(ks[9], (L, D_MODEL)),
        'w_in': nrm(ks[10], (L, D_MODEL, D_IN), D_MODEL ** -0.5),
        'g_qa': gain(ks[11], (L, HD_A)),
        'g_ka': gain(ks[12], (L, HD_A)),
        'g_kidx': gain(ks[13], (L, D_IDX)),
        'g_qm': gain(ks[14], (L, HD_M)),
        'g_mem': gain(ks[15], (L, D_MODEL)),
        'w_mem_kv': nrm(ks[16], (L, D_MODEL, 2 * D_MEM_ATT), D_MODEL ** -0.5),
        'g_km': gain(ks[17], (L, HD_M)),
        'w_pool': nrm(ks[18], (L, N_POOL_GROUPS, POOL_GROUP, POOL_GROUP), POOL_GROUP ** -0.5),
        's_pool': gain(ks[19], (L, D_POOL)),
        'w_oa': nrm(ks[20], (L, D_ATT_A, D_MODEL), D_ATT_A ** -0.5),
        'w_ob': nrm(ks[21], (L, D_POOL, D_MODEL), D_POOL ** -0.5),
        'w_om': nrm(ks[22], (L, D_MEM_ATT, D_MODEL), D_MEM_ATT ** -0.5),
        'w_out': nrm(ks[23], (L, D_MODEL, D_MODEL), D_MODEL ** -0.5),
        'g_ffn': gain(ks[24], (L, D_MODEL)),
        'w_gate': nrm(ks[25], (L, D_MODEL, D_FF), D_MODEL ** -0.5),
        'w_up': nrm(ks[26], (L, D_MODEL, D_FF), D_MODEL ** -0.5),
        'w_down': nrm(ks[27], (L, D_FF, D_MODEL), D_FF ** -0.5),
    }


def reference(x_prompt, x_sample, mem_prompt, cache_a_k, cache_a_v, cache_idx_k, cache_pool, cache_mem_k,
              cache_mem_v, g_mix, w_in, g_qa, g_ka, g_kidx, g_qm, g_mem, w_mem_kv, g_km, w_pool, s_pool,
              w_oa, w_ob, w_om, w_out, g_ffn, w_gate, w_up, w_down):
    T = x_prompt.shape[1]
    TS = x_sample.shape[1]
    P = cache_a_k.shape[2]
    pos_p = jnp.arange(T, dtype=jnp.int32)
    pos_s = P + jnp.arange(TS, dtype=jnp.int32)
    topk_p = min(TOPK_MAX, T // 4)
    topk_s = min(TOPK_MAX, (P + TS) // 4)
    xp, xs = x_prompt, x_sample
    ak_p, av_p, ik_p, pool_p, mk_p, mv_p = [], [], [], [], [], []
    ak_s, av_s, ik_s, pool_s = [], [], [], []
    for l in range(DEPTH):
        qa, ka, va, qi, ki, wi, ub, qm, gates = project_inputs(xp, pos_p, g_mix[l], w_in[l], g_qa[l], g_ka[l], g_kidx[l], g_qm[l])
        a = dsa_prompt(qa, qi, wi, pos_p, ka, va, ki, topk_p)
        ub_ext = jnp.pad(ub, ((0, 0), (POOL_HIST, 0), (0, 0)))
        b = pool_mix(ub_ext, 0, w_pool[l], s_pool[l])
        mk, mv = memory_kv(mem_prompt, g_mem[l], w_mem_kv[l], g_km[l])
        m = memory_attend(qm, mk, mv)
        xp = merge_and_ffn(xp, a, b, m, gates, w_oa[l], w_ob[l], w_om[l], w_out[l], g_ffn[l], w_gate[l], w_up[l], w_down[l])
        ak_p.append(ka)
        av_p.append(va)
        ik_p.append(ki)
        pool_p.append(ub[:, -POOL_HIST:])
        mk_p.append(mk)
        mv_p.append(mv)
        qa, ka, va, qi, ki, wi, ub, qm, gates = project_inputs(xs, pos_s, g_mix[l], w_in[l], g_qa[l], g_ka[l], g_kidx[l], g_qm[l])
        k_all = jnp.concatenate([cache_a_k[l], ka], axis=1)
        v_all = jnp.concatenate([cache_a_v[l], va], axis=1)
        ki_all = jnp.concatenate([cache_idx_k[l], ki], axis=1)
        a = dsa_attend(qa, qi, wi, pos_s, k_all, v_all, ki_all, topk_s)
        ub_ext = jnp.concatenate([cache_pool[l], ub], axis=1)
        b = pool_mix(ub_ext, P, w_pool[l], s_pool[l])
        m = memory_attend(qm, cache_mem_k[l], cache_mem_v[l])
        xs = merge_and_ffn(xs, a, b, m, gates, w_oa[l], w_ob[l], w_om[l], w_out[l], g_ffn[l], w_gate[l], w_up[l], w_down[l])
        ak_s.append(ka)
        av_s.append(va)
        ik_s.append(ki)
        pool_s.append(ub_ext[:, -POOL_HIST:])
    y_prompt = xp
    y_sample = xs
    new_a_k_prompt = jnp.stack(ak_p)
    new_a_v_prompt = jnp.stack(av_p)
    new_idx_k_prompt = jnp.stack(ik_p)
    new_pool_prompt = jnp.stack(pool_p)
    new_mem_k_prompt = jnp.stack(mk_p)
    new_mem_v_prompt = jnp.stack(mv_p)
    new_a_k_sample = jnp.stack(ak_s)
    new_a_v_sample = jnp.stack(av_s)
    new_idx_k_sample = jnp.stack(ik_s)
    new_pool_sample = jnp.stack(pool_s)
    return (y_prompt, y_sample, new_a_k_prompt, new_a_v_prompt, new_idx_k_prompt, new_pool_prompt, new_mem_k_prompt, new_mem_v_prompt, new_a_k_sample, new_a_v_sample, new_idx_k_sample, new_pool_sample)
```

```python
import functools

import jax
import jax.numpy as jnp
import numpy as np
from jax import lax
from jax.experimental import pallas as pl
from jax.experimental.pallas import tpu as pltpu

F32 = jnp.float32
BF16 = jnp.bfloat16

CHUNK = 64
EPS = 1e-6
ROPE_THETA = 10000.0
N_HEADS_A = 8
N_KV_A = 2
GROUP_A = N_HEADS_A // N_KV_A
HD_A = 64
D_ATT_A = N_HEADS_A * HD_A
D_KV_A = N_KV_A * HD_A
N_IDX_HEADS = 8
D_IDX = 32
D_QIDX = N_IDX_HEADS * D_IDX
IDX_SCALE = (N_IDX_HEADS * D_IDX) ** -0.5
TOPK_MAX = 256
POOL_WINDOWS = (2, 4, 8, 16)
POOL_GROUP = 128
D_POOL = len(POOL_WINDOWS) * POOL_GROUP
POOL_HIST = 15
N_HEADS_M = 4
HD_M = 128
D_MEM_ATT = N_HEADS_M * HD_M
N_BRANCH = 3

LANES = 128
SUBLANES = 8
V7X_VMEM_BYTES = 64 * 1024 * 1024
VMEM_LIMIT = 56 * 1024 * 1024

HIST_ROWS = 2 * SUBLANES
MASK_VALUE = -1e30
FLT_MAX = float(np.finfo(np.float32).max)

C_QA = 0
C_KA = C_QA + D_ATT_A
C_VA = C_KA + D_KV_A
C_QI = C_VA + D_KV_A
C_KW = C_QI + D_QIDX
C_UB = C_KW + LANES
C_QM = C_UB + D_POOL
C_END = C_QM + D_MEM_ATT


def _cparams(sem):
    return pltpu.CompilerParams(dimension_semantics=sem, vmem_limit_bytes=VMEM_LIMIT)


def _rms(x, g):
    ms = jnp.mean(x * x, axis=-1, keepdims=True)
    return x * lax.rsqrt(ms + EPS) * g


def _group_sumsq(x, bd):
    sq = x * x
    hi = sq.astype(BF16)
    lo = (sq - hi.astype(F32)).astype(BF16)
    return (jnp.dot(hi, bd, preferred_element_type=F32)
            + jnp.dot(lo, bd, preferred_element_type=F32))


def _rope_lanes(x, cos, sin, half):
    lane = lax.broadcasted_iota(jnp.int32, x.shape, 1)
    first = (lane % (2 * half)) < half
    left = pltpu.roll(x, LANES - half, 1)
    right = pltpu.roll(x, half, 1)
    rot = jnp.where(first, -left, right)
    return x * cos + rot * sin


def _proj_kernel(x_ref, gmix_ref, w_ref, cosa_ref, sina_ref, cosi_ref, sini_ref,
                 gqa_ref, gka_ref, gki_ref, gqm_ref, bd64_ref, bdki_ref,
                 qa_ref, ka_ref, va_ref, qi_ref, kw_ref, ub_ref, qm_ref):
    x = x_ref[...]
    h = _rms(x, gmix_ref[...]).astype(BF16)
    p = jnp.dot(h, w_ref[...], preferred_element_type=F32)
    cosa, sina = cosa_ref[...], sina_ref[...]
    cosi, sini = cosi_ref[...], sini_ref[...]
    bd64 = bd64_ref[...]

    qa = p[:, C_QA:C_QA + D_ATT_A]
    ss = _group_sumsq(qa, bd64)
    qa = qa * lax.rsqrt(ss * (1.0 / HD_A) + EPS) * gqa_ref[...]
    for c in range(D_ATT_A // LANES):
        blk = _rope_lanes(qa[:, c * LANES:(c + 1) * LANES], cosa, sina, HD_A // 2)
        qa_ref[:, c * LANES:(c + 1) * LANES] = (blk * (HD_A ** -0.5)).astype(BF16)

    ka = p[:, C_KA:C_KA + D_KV_A]
    ss = _group_sumsq(ka, bd64[:D_KV_A, :D_KV_A])
    ka = ka * lax.rsqrt(ss * (1.0 / HD_A) + EPS) * gka_ref[...]
    ka_ref[...] = _rope_lanes(ka, cosa, sina, HD_A // 2)
    va_ref[...] = p[:, C_VA:C_VA + D_KV_A]

    for c in range(D_QIDX // LANES):
        blk = p[:, C_QI + c * LANES:C_QI + (c + 1) * LANES]
        qi_ref[:, c * LANES:(c + 1) * LANES] = _rope_lanes(blk, cosi, sini, D_IDX // 2).astype(BF16)

    kw = p[:, C_KW:C_KW + LANES]
    ss = _group_sumsq(kw, bdki_ref[...])
    kin = kw * lax.rsqrt(ss * (1.0 / D_IDX) + EPS) * gki_ref[...]
    kin = _rope_lanes(kin, cosi, sini, D_IDX // 2)
    lane = lax.broadcasted_iota(jnp.int32, kw.shape, 1)
    kw_ref[...] = jnp.where(lane < D_IDX, kin, kw * IDX_SCALE)

    ub_ref[...] = p[:, C_UB:C_UB + D_POOL]

    gqm = gqm_ref[...]
    for hd in range(N_HEADS_M):
        blk = p[:, C_QM + hd * HD_M:C_QM + (hd + 1) * HD_M]
        qm_ref[:, hd * HD_M:(hd + 1) * HD_M] = _rms(blk, gqm).astype(BF16)


def _proj_call(x2d, gmix, w_cat, tabs, gqa_t, gka_t, gki_t, gqm, bd64, bdki, tm, n_tab_blocks):
    n = x2d.shape[0]
    d = x2d.shape[1]
    grid = (n // tm,)
    row = lambda i: (i, 0)
    const = lambda i: (0, 0)
    tab = lambda i: (i % n_tab_blocks, 0)
    in_specs = [
        pl.BlockSpec((tm, d), row),
        pl.BlockSpec((1, d), const),
        pl.BlockSpec((d, C_END), const),
        pl.BlockSpec((tm, LANES), tab), pl.BlockSpec((tm, LANES), tab),
        pl.BlockSpec((tm, LANES), tab), pl.BlockSpec((tm, LANES), tab),
        pl.BlockSpec((1, D_ATT_A), const), pl.BlockSpec((1, D_KV_A), const),
        pl.BlockSpec((1, LANES), const), pl.BlockSpec((1, HD_M), const),
        pl.BlockSpec((D_ATT_A, D_ATT_A), const), pl.BlockSpec((LANES, LANES), const),
    ]
    out_shape = [
        jax.ShapeDtypeStruct((n, D_ATT_A), BF16),
        jax.ShapeDtypeStruct((n, D_KV_A), F32),
        jax.ShapeDtypeStruct((n, D_KV_A), F32),
        jax.ShapeDtypeStruct((n, D_QIDX), BF16),
        jax.ShapeDtypeStruct((n, LANES), F32),
        jax.ShapeDtypeStruct((n, D_POOL), F32),
        jax.ShapeDtypeStruct((n, D_MEM_ATT), BF16),
    ]
    out_specs = [pl.BlockSpec((tm, s.shape[1]), row) for s in out_shape]
    return pl.pallas_call(
        _proj_kernel, grid=grid, in_specs=in_specs, out_specs=out_specs, out_shape=out_shape,
        compiler_params=_cparams(("parallel",)), name="proj",
    )(x2d, gmix, w_cat, *tabs, gqa_t, gka_t, gki_t, gqm, bd64, bdki)


def _ordered_bits_to_float(u):
    k = u ^ jnp.int32(-2147483648)
    bits = jnp.where(k >= 0, k, k ^ jnp.int32(0x7FFFFFFF))
    return lax.bitcast_convert_type(bits, F32)


def _dsa_kernel(qs_ref, qi_ref, wi_ref, kt_ref, v_ref, kit_ref, o_ref, sc_ref, m_ref, acc_ref,
                *, tq, kb, nkb_total, causal, s_valid, topk):
    qblk = pl.program_id(1)
    nkb = qblk + 1 if causal else nkb_total
    row = lax.broadcasted_iota(jnp.int32, (tq, 1), 0)
    if causal:
        lim = (lax.shift_right_logical(qblk * tq + row, 6) + 1) * CHUNK
    else:
        lim = jnp.full((tq, 1), s_valid, jnp.int32)
    limf = lim.astype(F32)
    col = lax.broadcasted_iota(jnp.int32, (1, kb), 1)
    kf = float(topk)
    nsub = kb // LANES

    qi = qi_ref[...]
    wcols = [jnp.broadcast_to(wi_ref[:, h:h + 1], (tq, kb)) for h in range(N_IDX_HEADS)]

    def score_body(j, carry):
        z = jnp.dot(qi, kit_ref[j], preferred_element_type=F32)
        s = jnp.zeros((tq, kb), F32)
        for h in range(N_IDX_HEADS):
            s = s + wcols[h] * jnp.maximum(z[h * tq:(h + 1) * tq], 0.0)
        sc_ref[j] = jnp.where(j * kb + col < lim, s, -jnp.inf)
        return carry

    lax.fori_loop(0, nkb, score_body, 0)

    def count(pred):
        def body(j, cnt):
            c = jnp.where(pred(sc_ref[j], j), 1.0, 0.0)
            for u in range(nsub):
                cnt = cnt + c[:, u * LANES:(u + 1) * LANES]
            return cnt
        cnt = lax.fori_loop(0, nkb, body, jnp.zeros((tq, LANES), F32))
        return jnp.sum(cnt, axis=1, keepdims=True)

    def radix_body(i, carry):
        prefix, cge = carry
        cand = prefix | lax.shift_left(jnp.int32(1), 31 - i)
        t = jnp.broadcast_to(_ordered_bits_to_float(cand), (tq, kb))
        c = count(lambda s, j: s >= t)
        take = c >= kf
        return jnp.where(take, cand, prefix), jnp.where(take, c, cge)

    prefix, cge = lax.fori_loop(0, 32, radix_body, (jnp.zeros((tq, 1), jnp.int32), limf))
    thr = jnp.where(limf >= kf, _ordered_bits_to_float(prefix), -FLT_MAX)
    thr_b = jnp.broadcast_to(thr, (tq, kb))

    any_tie = jnp.max(jnp.max(jnp.where(cge > kf, 1.0, 0.0), axis=0, keepdims=True), axis=1, keepdims=True)

    def mask_plain():
        def body(j, carry):
            sc_ref[j] = jnp.where(sc_ref[j] >= thr_b, 0.0, MASK_VALUE)
            return carry
        lax.fori_loop(0, nkb, body, 0)

    def mask_ties():
        cgt = count(lambda s, j: s > thr_b)
        need = kf - cgt
        nbits = int(nkb_total * kb - 1).bit_length()

        def idx_body(i, j0):
            cand = j0 | lax.shift_left(jnp.int32(1), nbits - 1 - i)
            c = count(lambda s, j: (s == thr_b) & (j * kb + col < cand))
            return jnp.where(c < need, cand, j0)

        j0 = lax.fori_loop(0, nbits, idx_body, jnp.zeros((tq, 1), jnp.int32))

        def body(j, carry):
            s = sc_ref[j]
            keep = (s > thr_b) | ((s == thr_b) & (j * kb + col <= j0))
            sc_ref[j] = jnp.where(keep, 0.0, MASK_VALUE)
            return carry
        lax.fori_loop(0, nkb, body, 0)

    lax.cond(any_tie[0, 0] > 0.0, mask_ties, mask_plain)

    m_ref[...] = jnp.full(m_ref.shape, MASK_VALUE, F32)
    acc_ref[...] = jnp.zeros(acc_ref.shape, F32)

    def attn_body(j, carry):
        bias = sc_ref[j]
        for g in range(N_KV_A):
            lg = jnp.dot(qs_ref[g], kt_ref[g, j], preferred_element_type=F32)
            vblk = v_ref[g, j]
            for u in range(GROUP_A):
                r0 = (g * GROUP_A + u) * tq
                l = lg[u * tq:(u + 1) * tq] + bias
                m_old = m_ref[r0:r0 + tq, :]
                m_new = jnp.maximum(m_old, jnp.max(l, axis=1, keepdims=True))
                alpha = jnp.exp(m_old - m_new)
                p = jnp.exp(l - jnp.concatenate([m_new] * nsub, axis=1))
                pv = jnp.dot(p.astype(BF16), vblk, preferred_element_type=F32)
                acc_ref[r0:r0 + tq, :] = alpha * acc_ref[r0:r0 + tq, :] + pv
                m_ref[r0:r0 + tq, :] = m_new
        return carry

    lax.fori_loop(0, nkb, attn_body, 0)

    for hd in range(N_HEADS_A):
        a = acc_ref[hd * tq:(hd + 1) * tq, :]
        o_ref[hd * tq:(hd + 1) * tq, :] = (a[:, :HD_A] / a[:, HD_A:HD_A + 1]).astype(BF16)


def _dsa_call(qs, qi, wi, kt, vaug, kit, *, tq, kb, causal, s_valid, topk):
    b, nqb = qs.shape[0], qs.shape[1]
    nkb_total = kt.shape[2]
    kern = functools.partial(_dsa_kernel, tq=tq, kb=kb, nkb_total=nkb_total, causal=causal,
                             s_valid=s_valid, topk=topk)
    in_specs = [
        pl.BlockSpec((None, None, N_KV_A, GROUP_A * tq, HD_A), lambda i, j: (i, j, 0, 0, 0)),
        pl.BlockSpec((None, None, N_IDX_HEADS * tq, D_IDX), lambda i, j: (i, j, 0, 0)),
        pl.BlockSpec((None, tq, N_IDX_HEADS), lambda i, j: (i, j, 0)),
        pl.BlockSpec((None, N_KV_A, nkb_total, HD_A, kb), lambda i, j: (i, 0, 0, 0, 0)),
        pl.BlockSpec((None, N_KV_A, nkb_total, kb, LANES), lambda i, j: (i, 0, 0, 0, 0)),
        pl.BlockSpec((None, nkb_total, D_IDX, kb), lambda i, j: (i, 0, 0, 0)),
    ]
    out_spec = pl.BlockSpec((None, None, N_HEADS_A * tq, HD_A), lambda i, j: (i, j, 0, 0))
    return pl.pallas_call(
        kern, grid=(b, nqb), in_specs=in_specs, out_specs=out_spec,
        out_shape=jax.ShapeDtypeStruct((b, nqb, N_HEADS_A * tq, HD_A), BF16),
        scratch_shapes=[pltpu.VMEM((nkb_total, tq, kb), F32),
                        pltpu.VMEM((N_HEADS_A * tq, LANES), F32),
                        pltpu.VMEM((N_HEADS_A * tq, LANES), F32)],
        compiler_params=_cparams(("parallel", "arbitrary")), name="dsa",
    )(qs, qi, wi, kt, vaug, kit)


def _memkv_kernel(mem_ref, gmem_ref, w_ref, gkm_ref, k_ref, v_ref):
    h = _rms(mem_ref[...], gmem_ref[...]).astype(BF16)
    kv = jnp.dot(h, w_ref[...], preferred_element_type=F32)
    gkm = gkm_ref[...]
    for hd in range(N_HEADS_M):
        k_ref[:, hd * HD_M:(hd + 1) * HD_M] = _rms(kv[:, hd * HD_M:(hd + 1) * HD_M], gkm)
    v_ref[...] = kv[:, D_MEM_ATT:]


def _memkv_call(mem2d, gmem, w_kv, gkm, tm):
    n, d = mem2d.shape
    row = lambda i: (i, 0)
    const = lambda i: (0, 0)
    return pl.pallas_call(
        _memkv_kernel, grid=(n // tm,),
        in_specs=[pl.BlockSpec((tm, d), row), pl.BlockSpec((1, d), const),
                  pl.BlockSpec((d, 2 * D_MEM_ATT), const), pl.BlockSpec((1, HD_M), const)],
        out_specs=[pl.BlockSpec((tm, D_MEM_ATT), row), pl.BlockSpec((tm, D_MEM_ATT), row)],
        out_shape=[jax.ShapeDtypeStruct((n, D_MEM_ATT), F32), jax.ShapeDtypeStruct((n, D_MEM_ATT), F32)],
        compiler_params=_cparams(("parallel",)), name="memkv",
    )(mem2d, gmem, w_kv, gkm)


def _merge_kernel(x_ref, a_ref, ub_ref, prev_ref, hist0_ref, qm_ref, mkt_ref, mv_ref,
                  gmix_ref, wg_ref, woa_ref, wob_ref, wom_ref, wout_ref, wpool_ref, spool_ref,
                  o_ref, ext_ref, *, tm, pos0):
    it = pl.program_id(1)
    x = x_ref[...]

    h = _rms(x, gmix_ref[...]).astype(BF16)
    gates = jax.nn.sigmoid(jnp.dot(h, wg_ref[...], preferred_element_type=F32))
    d = x.shape[1]

    ub = ub_ref[...]
    ext_ref[0:HIST_ROWS, :] = jnp.where(it == 0, hist0_ref[...], prev_ref[...])
    ext_ref[HIST_ROWS:HIST_ROWS + tm, :] = ub
    pos = pos0 + it * tm + lax.broadcasted_iota(jnp.int32, (tm, 1), 0)
    ys = []
    for g, w in enumerate(POOL_WINDOWS):
        c0 = g * POOL_GROUP
        win = ub[:, c0:c0 + POOL_GROUP]
        for k in range(1, w):
            win = win + ext_ref[HIST_ROWS - k:HIST_ROWS - k + tm, c0:c0 + POOL_GROUP]
        cnt = jnp.minimum(w, pos + 1).astype(F32)
        pg = (win / cnt - ub[:, c0:c0 + POOL_GROUP]).astype(BF16)
        ys.append(jnp.dot(pg, wpool_ref[g], preferred_element_type=F32))
    bmix = (jnp.concatenate(ys, axis=1) * spool_ref[...]).astype(BF16)

    qm = qm_ref[...]
    ms = []
    for hd in range(N_HEADS_M):
        lg = jnp.dot(qm[:, hd * HD_M:(hd + 1) * HD_M], mkt_ref[hd], preferred_element_type=F32)
        lg = lg * (HD_M ** -0.5)
        e = jnp.exp(lg - jnp.max(lg, axis=1, keepdims=True))
        pr = (e / jnp.sum(e, axis=1, keepdims=True)).astype(BF16)
        ms.append(jnp.dot(pr, mv_ref[hd], preferred_element_type=F32))
    mmix = jnp.concatenate(ms, axis=1).astype(BF16)

    mixed = (gates[:, 0:d] * jnp.dot(a_ref[...], woa_ref[...], preferred_element_type=F32)
             + gates[:, d:2 * d] * jnp.dot(bmix, wob_ref[...], preferred_element_type=F32)
             + gates[:, 2 * d:3 * d] * jnp.dot(mmix, wom_ref[...], preferred_element_type=F32))
    o_ref[...] = x + jnp.dot(mixed.astype(BF16), wout_ref[...], preferred_element_type=F32)


def _merge_call(x, a, ub, hist0, qm, mkt, mv, gmix, wg, woa, wob, wom, wout, wpool, spool, *, tm, pos0):
    b, t, d = x.shape
    nt = t // tm
    hb = tm // HIST_ROWS
    tok = lambda i, j: (i, j, 0)
    c2 = lambda i, j: (0, 0)
    c3 = lambda i, j: (0, 0, 0)
    per_b3 = lambda i, j: (i, 0, 0)
    per_b4 = lambda i, j: (i, 0, 0, 0)
    in_specs = [
        pl.BlockSpec((None, tm, d), tok),
        pl.BlockSpec((None, tm, D_ATT_A), tok),
        pl.BlockSpec((None, tm, D_POOL), tok),
        pl.BlockSpec((None, HIST_ROWS, D_POOL), lambda i, j: (i, jnp.maximum(j * hb - 1, 0), 0)),
        pl.BlockSpec((None, HIST_ROWS, D_POOL), per_b3),
        pl.BlockSpec((None, tm, D_MEM_ATT), tok),
        pl.BlockSpec((None, N_HEADS_M, HD_M, mkt.shape[3]), per_b4),
        pl.BlockSpec((None, N_HEADS_M, mv.shape[2], HD_M), per_b4),
        pl.BlockSpec((1, d), c2),
        pl.BlockSpec(wg.shape, c2),
        pl.BlockSpec(woa.shape, c2), pl.BlockSpec(wob.shape, c2), pl.BlockSpec(wom.shape, c2),
        pl.BlockSpec(wout.shape, c2),
        pl.BlockSpec(wpool.shape, c3),
        pl.BlockSpec((1, D_POOL), c2),
    ]
    return pl.pallas_call(
        functools.partial(_merge_kernel, tm=tm, pos0=pos0), grid=(b, nt), in_specs=in_specs,
        out_specs=pl.BlockSpec((None, tm, d), tok),
        out_shape=jax.ShapeDtypeStruct((b, t, d), F32),
        scratch_shapes=[pltpu.VMEM((HIST_ROWS + tm, D_POOL), F32)],
        compiler_params=_cparams(("parallel", "arbitrary")), name="merge",
    )(x, a, ub, ub, hist0, qm, mkt, mv, gmix, wg, woa, wob, wom, wout, wpool, spool)


def _ffn_kernel(x_ref, g_ref, wgate_ref, wup_ref, wdown_ref, o_ref):
    x = x_ref[...]
    h = _rms(x, g_ref[...]).astype(BF16)
    gate = jnp.dot(h, wgate_ref[...], preferred_element_type=F32)
    up = jnp.dot(h, wup_ref[...], preferred_element_type=F32)
    act = (jax.nn.silu(gate) * up).astype(BF16)
    o_ref[...] = x + jnp.dot(act, wdown_ref[...], preferred_element_type=F32)


def _ffn_call(x2d, g, wgate, wup, wdown, tm):
    n, d = x2d.shape
    row = lambda i: (i, 0)
    const = lambda i: (0, 0)
    return pl.pallas_call(
        _ffn_kernel, grid=(n // tm,),
        in_specs=[pl.BlockSpec((tm, d), row), pl.BlockSpec((1, d), const),
                  pl.BlockSpec(wgate.shape, const), pl.BlockSpec(wup.shape, const),
                  pl.BlockSpec(wdown.shape, const)],
        out_specs=pl.BlockSpec((tm, d), row),
        out_shape=jax.ShapeDtypeStruct((n, d), F32),
        compiler_params=_cparams(("parallel",)), name="ffn",
    )(x2d, g, wgate, wup, wdown)


def _rope_tables(pos, head_dim):
    half = head_dim // 2
    inv = ROPE_THETA ** (-jnp.arange(half, dtype=F32) / half)
    ang = pos.astype(F32)[:, None] * inv[None, :]
    reps = LANES // half
    return jnp.tile(jnp.cos(ang), (1, reps)), jnp.tile(jnp.sin(ang), (1, reps))


def _pack_weights(w_in, g_qa, g_ka, g_kidx, g_qm):
    d = w_in.shape[0]
    widths = (D_ATT_A, D_KV_A, D_KV_A, D_QIDX, D_IDX, N_IDX_HEADS, D_POOL, D_MEM_ATT, N_BRANCH * d)
    cuts = [int(c) for c in np.cumsum(widths)[:-1]]
    wqa, wka, wva, wqi, wki, wwi, wub, wqm, wgates = jnp.split(w_in, cuts, axis=1)
    pad = jnp.zeros((d, LANES - D_IDX - N_IDX_HEADS), w_in.dtype)
    w_cat = jnp.concatenate([wqa, wka, wva, wqi, wki, wwi, pad, wub, wqm], axis=1).astype(BF16)
    gqa_t = jnp.tile(g_qa, N_HEADS_A)[None, :]
    gka_t = jnp.tile(g_ka, N_KV_A)[None, :]
    gki_t = jnp.concatenate([g_kidx, jnp.ones((LANES - D_IDX,), g_kidx.dtype)])[None, :]
    return w_cat, wgates.astype(BF16), gqa_t, gka_t, gki_t, g_qm[None, :]


def _block_diag_ones(n, group, limit=None):
    i = np.arange(n)
    m = (i[:, None] // group) == (i[None, :] // group)
    if limit is not None:
        m = m & (i[:, None] < limit) & (i[None, :] < limit)
    return jnp.asarray(m, BF16)


def _head_major_queries(qa, qi, b, t, tq):
    nqb = t // tq
    qs = qa.reshape(b, nqb, tq, N_KV_A, GROUP_A, HD_A).transpose(0, 1, 3, 4, 2, 5)
    qs = qs.reshape(b, nqb, N_KV_A, GROUP_A * tq, HD_A)
    qis = qi.reshape(b, nqb, tq, N_IDX_HEADS, D_IDX).transpose(0, 1, 3, 2, 4)
    qis = qis.reshape(b, nqb, N_IDX_HEADS * tq, D_IDX)
    return qs, qis


def _key_blocks(k_all, v_all, ki_all, kb):
    b, s = k_all.shape[0], k_all.shape[1]
    nkb = -(-s // kb)
    pad = nkb * kb - s
    if pad:
        k_all = jnp.pad(k_all, ((0, 0), (0, pad), (0, 0), (0, 0)))
        v_all = jnp.pad(v_all, ((0, 0), (0, pad), (0, 0), (0, 0)))
        ki_all = jnp.pad(ki_all, ((0, 0), (0, pad), (0, 0)))
    kt = k_all.astype(BF16).reshape(b, nkb, kb, N_KV_A, HD_A).transpose(0, 3, 1, 4, 2)
    ones = jnp.ones(v_all.shape[:-1] + (1,), BF16)
    zeros = jnp.zeros(v_all.shape[:-1] + (LANES - HD_A - 1,), BF16)
    vaug = jnp.concatenate([v_all.astype(BF16), ones, zeros], axis=-1)
    vaug = vaug.reshape(b, nkb, kb, N_KV_A, LANES).transpose(0, 3, 1, 2, 4)
    kit = ki_all.astype(BF16).reshape(b, nkb, kb, D_IDX).transpose(0, 1, 3, 2)
    return kt, vaug, kit


def _tile(n, pref):
    t = pref
    while n % t:
        t //= 2
    return t


def _group_forward(x, pos, pos0, hist0, k_hist, v_hist, ki_hist, mkt, mv, wts, *, causal):
    (gmix, w_cat, wgates, gqa_t, gka_t, gki_t, gqm, bd64, bdki, wpool, spool,
     woa, wob, wom, wout, gffn, wgate, wup, wdown) = wts
    b, t, d = x.shape
    n = b * t
    tm = _tile(t, 512)
    cosa, sina = _rope_tables(pos, HD_A)
    cosi, sini = _rope_tables(pos, D_IDX)
    qa, ka, va, qi, kw, ub, qm = _proj_call(
        x.reshape(n, d), gmix, w_cat, (cosa, sina, cosi, sini), gqa_t, gka_t, gki_t, gqm, bd64, bdki,
        tm, t // tm)
    ka = ka.reshape(b, t, N_KV_A, HD_A)
    va = va.reshape(b, t, N_KV_A, HD_A)
    ki = kw[:, :D_IDX].reshape(b, t, D_IDX)
    wi = kw[:, D_IDX:D_IDX + N_IDX_HEADS].reshape(b, t, N_IDX_HEADS)
    ub = ub.reshape(b, t, D_POOL)

    if k_hist is None:
        k_all, v_all, ki_all = ka, va, ki
    else:
        k_all = jnp.concatenate([k_hist, ka], axis=1)
        v_all = jnp.concatenate([v_hist, va], axis=1)
        ki_all = jnp.concatenate([ki_hist, ki], axis=1)
    s = k_all.shape[1]
    topk = min(TOPK_MAX, s // 4)
    kb = 256
    tq = kb if causal else _tile(t, 256)
    kt, vaug, kit = _key_blocks(k_all, v_all, ki_all, kb)
    qs, qis = _head_major_queries(qa.reshape(b, t, D_ATT_A), qi.reshape(b, t, D_QIDX), b, t, tq)
    o = _dsa_call(qs, qis, wi, kt, vaug, kit, tq=tq, kb=kb, causal=causal, s_valid=s, topk=topk)
    a = o.reshape(b, t // tq, N_HEADS_A, tq, HD_A).transpose(0, 1, 3, 2, 4).reshape(b, t, D_ATT_A)

    tmm = _tile(t, 256)
    x2 = _merge_call(x, a, ub, hist0, qm.reshape(b, t, D_MEM_ATT), mkt, mv, gmix, wgates,
                     woa, wob, wom, wout, wpool, spool, tm=tmm, pos0=pos0)
    y = _ffn_call(x2.reshape(n, d), gffn, wgate, wup, wdown, _tile(n, 256)).reshape(b, t, d)
    return y, ka, va, ki, ub


def kernel(x_prompt, x_sample, mem_prompt, cache_a_k, cache_a_v, cache_idx_k, cache_pool, cache_mem_k,
           cache_mem_v, g_mix, w_in, g_qa, g_ka, g_kidx, g_qm, g_mem, w_mem_kv, g_km, w_pool, s_pool,
           w_oa, w_ob, w_om, w_out, g_ffn, w_gate, w_up, w_down):
    depth = w_in.shape[0]
    t = x_prompt.shape[1]
    ts = x_sample.shape[1]
    past = cache_a_k.shape[2]
    b = x_prompt.shape[0]
    n_mem = mem_prompt.shape[1]
    pos_p = jnp.arange(t, dtype=jnp.int32)
    pos_s = past + jnp.arange(ts, dtype=jnp.int32)
    bd64 = _block_diag_ones(D_ATT_A, HD_A)
    bdki = _block_diag_ones(LANES, LANES, limit=D_IDX)

    xp, xs = x_prompt, x_sample
    outs = [[] for _ in range(10)]
    for l in range(depth):
        w_cat, wgates, gqa_t, gka_t, gki_t, gqm = _pack_weights(w_in[l], g_qa[l], g_ka[l], g_kidx[l], g_qm[l])
        wts = (g_mix[l][None, :], w_cat, wgates, gqa_t, gka_t, gki_t, gqm, bd64, bdki,
               w_pool[l].astype(BF16), s_pool[l][None, :],
               w_oa[l].astype(BF16), w_ob[l].astype(BF16), w_om[l].astype(BF16), w_out[l].astype(BF16),
               g_ffn[l][None, :], w_gate[l].astype(BF16), w_up[l].astype(BF16), w_down[l].astype(BF16))

        mk, mv = _memkv_call(mem_prompt.reshape(b * n_mem, -1), g_mem[l][None, :],
                             w_mem_kv[l].astype(BF16), g_km[l][None, :], _tile(b * n_mem, 256))
        mk = mk.reshape(b, n_mem, N_HEADS_M, HD_M)
        mv = mv.reshape(b, n_mem, N_HEADS_M, HD_M)
        hist0 = jnp.zeros((b, HIST_ROWS, D_POOL), F32)
        xp, ka, va, ki, ub = _group_forward(
            xp, pos_p, 0, hist0, None, None, None,
            mk.astype(BF16).transpose(0, 2, 3, 1), mv.astype(BF16).transpose(0, 2, 1, 3), wts, causal=True)
        for lst, val in zip(outs[:6], (ka, va, ki, ub[:, -POOL_HIST:], mk, mv)):
            lst.append(val)

        bs = xs.shape[0]
        hist0 = jnp.concatenate([jnp.zeros((bs, HIST_ROWS - POOL_HIST, D_POOL), F32), cache_pool[l]], axis=1)
        xs, ka, va, ki, ub = _group_forward(
            xs, pos_s, past, hist0, cache_a_k[l], cache_a_v[l], cache_idx_k[l],
            cache_mem_k[l].astype(BF16).transpose(0, 2, 3, 1), cache_mem_v[l].astype(BF16).transpose(0, 2, 1, 3),
            wts, causal=False)
        pool_s = jnp.concatenate([cache_pool[l], ub], axis=1)[:, -POOL_HIST:]
        for lst, val in zip(outs[6:], (ka, va, ki, pool_s)):
            lst.append(val)

    stacked = [jnp.stack(o) for o in outs]
    return (xp, xs, *stacked)
```

```python
import functools

import jax
import jax.numpy as jnp
import numpy as np
from jax import lax
from jax.experimental import pallas as pl
from jax.experimental.pallas import tpu as pltpu

F32 = jnp.float32
BF16 = jnp.bfloat16

CHUNK = 64
EPS = 1e-6
ROPE_THETA = 10000.0
N_HEADS_A = 8
N_KV_A = 2
GROUP_A = N_HEADS_A // N_KV_A
HD_A = 64
D_ATT_A = N_HEADS_A * HD_A
D_KV_A = N_KV_A * HD_A
N_IDX_HEADS = 8
D_IDX = 32
D_QIDX = N_IDX_HEADS * D_IDX
IDX_SCALE = (N_IDX_HEADS * D_IDX) ** -0.5
TOPK_MAX = 256
POOL_WINDOWS = (2, 4, 8, 16)
POOL_GROUP = 128
D_POOL = len(POOL_WINDOWS) * POOL_GROUP
POOL_HIST = 15
N_HEADS_M = 4
HD_M = 128
D_MEM_ATT = N_HEADS_M * HD_M
N_BRANCH = 3

LANES = 128
SUBLANES = 8
V7X_VMEM_BYTES = 64 * 1024 * 1024
VMEM_LIMIT = 56 * 1024 * 1024

HIST_ROWS = 2 * SUBLANES
MASK_VALUE = -1e30
FLT_MAX = float(np.finfo(np.float32).max)
N_DIGITS = 4
RADIX = 256

C_QA = 0
C_KA = C_QA + D_ATT_A
C_VA = C_KA + D_KV_A
C_QI = C_VA + D_KV_A
C_KW = C_QI + D_QIDX
C_UB = C_KW + LANES
C_QM = C_UB + D_POOL
C_END = C_QM + D_MEM_ATT


def _cparams(sem):
    return pltpu.CompilerParams(dimension_semantics=sem, vmem_limit_bytes=VMEM_LIMIT)


def _rms(x, g):
    ms = jnp.mean(x * x, axis=-1, keepdims=True)
    return x * lax.rsqrt(ms + EPS) * g


def _group_sumsq(x, bd):
    sq = x * x
    hi = sq.astype(BF16)
    lo = (sq - hi.astype(F32)).astype(BF16)
    return (jnp.dot(hi, bd, preferred_element_type=F32)
            + jnp.dot(lo, bd, preferred_element_type=F32))


def _rope_lanes(x, cos, sin, half):
    lane = lax.broadcasted_iota(jnp.int32, x.shape, 1)
    first = (lane % (2 * half)) < half
    left = pltpu.roll(x, LANES - half, 1)
    right = pltpu.roll(x, half, 1)
    rot = jnp.where(first, -left, right)
    return x * cos + rot * sin


def _proj_kernel(x_ref, gmix_ref, w_ref, cosa_ref, sina_ref, cosi_ref, sini_ref,
                 gqa_ref, gka_ref, gki_ref, gqm_ref, bd64_ref, bdki_ref,
                 qa_ref, ka_ref, va_ref, qi_ref, kw_ref, ub_ref, qm_ref):
    x = x_ref[...]
    h = _rms(x, gmix_ref[...]).astype(BF16)
    p = jnp.dot(h, w_ref[...], preferred_element_type=F32)
    cosa, sina = cosa_ref[...], sina_ref[...]
    cosi, sini = cosi_ref[...], sini_ref[...]
    bd64 = bd64_ref[...]

    qa = p[:, C_QA:C_QA + D_ATT_A]
    ss = _group_sumsq(qa, bd64)
    qa = qa * lax.rsqrt(ss * (1.0 / HD_A) + EPS) * gqa_ref[...]
    for c in range(D_ATT_A // LANES):
        blk = _rope_lanes(qa[:, c * LANES:(c + 1) * LANES], cosa, sina, HD_A // 2)
        qa_ref[:, c * LANES:(c + 1) * LANES] = (blk * (HD_A ** -0.5)).astype(BF16)

    ka = p[:, C_KA:C_KA + D_KV_A]
    ss = _group_sumsq(ka, bd64[:D_KV_A, :D_KV_A])
    ka = ka * lax.rsqrt(ss * (1.0 / HD_A) + EPS) * gka_ref[...]
    ka_ref[...] = _rope_lanes(ka, cosa, sina, HD_A // 2)
    va_ref[...] = p[:, C_VA:C_VA + D_KV_A]

    for c in range(D_QIDX // LANES):
        blk = p[:, C_QI + c * LANES:C_QI + (c + 1) * LANES]
        qi_ref[:, c * LANES:(c + 1) * LANES] = _rope_lanes(blk, cosi, sini, D_IDX // 2).astype(BF16)

    kw = p[:, C_KW:C_KW + LANES]
    ss = _group_sumsq(kw, bdki_ref[...])
    kin = kw * lax.rsqrt(ss * (1.0 / D_IDX) + EPS) * gki_ref[...]
    kin = _rope_lanes(kin, cosi, sini, D_IDX // 2)
    lane = lax.broadcasted_iota(jnp.int32, kw.shape, 1)
    kw_ref[...] = jnp.where(lane < D_IDX, kin, kw * IDX_SCALE)

    ub_ref[...] = p[:, C_UB:C_UB + D_POOL]

    gqm = gqm_ref[...]
    for hd in range(N_HEADS_M):
        blk = p[:, C_QM + hd * HD_M:C_QM + (hd + 1) * HD_M]
        qm_ref[:, hd * HD_M:(hd + 1) * HD_M] = _rms(blk, gqm).astype(BF16)


def _proj_call(x2d, gmix, w_cat, tabs, gqa_t, gka_t, gki_t, gqm, bd64, bdki, tm, n_tab_blocks):
    n = x2d.shape[0]
    d = x2d.shape[1]
    grid = (n // tm,)
    row = lambda i: (i, 0)
    const = lambda i: (0, 0)
    tab = lambda i: (i % n_tab_blocks, 0)
    in_specs = [
        pl.BlockSpec((tm, d), row),
        pl.BlockSpec((1, d), const),
        pl.BlockSpec((d, C_END), const),
        pl.BlockSpec((tm, LANES), tab), pl.BlockSpec((tm, LANES), tab),
        pl.BlockSpec((tm, LANES), tab), pl.BlockSpec((tm, LANES), tab),
        pl.BlockSpec((1, D_ATT_A), const), pl.BlockSpec((1, D_KV_A), const),
        pl.BlockSpec((1, LANES), const), pl.BlockSpec((1, HD_M), const),
        pl.BlockSpec((D_ATT_A, D_ATT_A), const), pl.BlockSpec((LANES, LANES), const),
    ]
    out_shape = [
        jax.ShapeDtypeStruct((n, D_ATT_A), BF16),
        jax.ShapeDtypeStruct((n, D_KV_A), F32),
        jax.ShapeDtypeStruct((n, D_KV_A), F32),
        jax.ShapeDtypeStruct((n, D_QIDX), BF16),
        jax.ShapeDtypeStruct((n, LANES), F32),
        jax.ShapeDtypeStruct((n, D_POOL), F32),
        jax.ShapeDtypeStruct((n, D_MEM_ATT), BF16),
    ]
    out_specs = [pl.BlockSpec((tm, s.shape[1]), row) for s in out_shape]
    return pl.pallas_call(
        _proj_kernel, grid=grid, in_specs=in_specs, out_specs=out_specs, out_shape=out_shape,
        compiler_params=_cparams(("parallel",)), name="proj",
    )(x2d, gmix, w_cat, *tabs, gqa_t, gka_t, gki_t, gqm, bd64, bdki)


def _dsa_kernel(qs_ref, qi_ref, wi_ref, kt_ref, v_ref, kit_ref, o_ref, sc_ref, dg_ref, m_ref, acc_ref,
                *, tq, kb, nkb_total, causal, s_valid, topk):
    qblk = pl.program_id(1)
    nkb = qblk + 1 if causal else nkb_total
    row = lax.broadcasted_iota(jnp.int32, (tq, 1), 0)
    if causal:
        lim = (lax.shift_right_logical(qblk * tq + row, 6) + 1) * CHUNK
    else:
        lim = jnp.full((tq, 1), s_valid, jnp.int32)
    limf = lim.astype(F32)
    col = lax.broadcasted_iota(jnp.int32, (1, kb), 1)
    kf = float(topk)
    nsub = kb // LANES

    qi = qi_ref[...]
    wcols = [jnp.broadcast_to(wi_ref[:, h:h + 1], (tq, kb)) for h in range(N_IDX_HEADS)]

    def score_body(j, carry):
        z = jnp.dot(qi, kit_ref[j], preferred_element_type=F32)
        s = jnp.zeros((tq, kb), F32)
        for h in range(N_IDX_HEADS):
            s = s + wcols[h] * jnp.maximum(z[h * tq:(h + 1) * tq], 0.0)
        s = jnp.where(j * kb + col < lim, s, -jnp.inf)
        sc_ref[j] = s
        bits = lax.bitcast_convert_type(s, jnp.int32)
        key = bits ^ (lax.shift_right_arithmetic(bits, 31) & jnp.int32(0x7FFFFFFF))
        dg_ref[0, j] = (lax.shift_right_arithmetic(key, 24) + RADIX // 2).astype(F32).astype(BF16)
        for lvl in range(1, N_DIGITS):
            sh = 8 * (N_DIGITS - 1 - lvl)
            dg_ref[lvl, j] = (lax.shift_right_logical(key, sh) & (RADIX - 1)).astype(F32).astype(BF16)
        return carry

    lax.fori_loop(0, nkb, score_body, 0)

    one_b = jnp.asarray(1, BF16)
    zero_b = jnp.asarray(0, BF16)

    def count_ge(cand):
        cb = jnp.broadcast_to(cand, (tq, LANES)).astype(BF16)

        def body(j, acc):
            x = dg_ref[0, j]
            for u in range(nsub):
                acc = acc + jnp.where(x[:, u * LANES:(u + 1) * LANES] >= cb, one_b, zero_b)
            return acc
        acc = lax.fori_loop(0, nkb, body, jnp.zeros((tq, LANES), BF16))
        return jnp.sum(acc.astype(F32), axis=1, keepdims=True)

    def digit_select(want, cge0):
        def bit_body(i, carry):
            prefix, c_gt, c_ge = carry
            cand = prefix + lax.shift_left(jnp.int32(1), 7 - i).astype(F32)
            c = count_ge(cand)
            take = c >= want
            return (jnp.where(take, cand, prefix), jnp.where(take, c_gt, c), jnp.where(take, c, c_ge))
        z = jnp.zeros((tq, 1), F32)
        return lax.fori_loop(0, 8, bit_body, (z, z, cge0))

    want = jnp.full((tq, 1), kf, F32)
    keybits = jnp.zeros((tq, 1), jnp.int32)
    cge = limf
    for lvl in range(N_DIGITS):
        last = lvl == N_DIGITS - 1
        cge0 = count_ge(jnp.zeros((tq, 1), F32)) if last else jnp.zeros((tq, 1), F32)
        digit, c_gt, c_ge = digit_select(want, cge0)
        di = digit.astype(jnp.int32) - (RADIX // 2 if lvl == 0 else 0)
        keybits = keybits | lax.shift_left(di, 8 * (N_DIGITS - 1 - lvl))
        if last:
            cge = c_ge
        else:
            want = want - c_gt
            db = jnp.broadcast_to(digit, (tq, kb)).astype(BF16)
            nxt = lvl + 1

            def narrow(j, carry, db=db, nxt=nxt):
                dg_ref[0, j] = jnp.where(dg_ref[0, j] == db, dg_ref[nxt, j], -one_b)
                return carry
            lax.fori_loop(0, nkb, narrow, 0)

    thr_bits = jnp.where(keybits >= 0, keybits, keybits ^ jnp.int32(0x7FFFFFFF))
    few = limf < kf
    thr = jnp.where(few, -FLT_MAX, lax.bitcast_convert_type(thr_bits, F32))
    thr_b = jnp.broadcast_to(thr, (tq, kb))
    tied = jnp.where(few, 0.0, jnp.where(cge > want, 1.0, 0.0))

    def count(pred):
        def body(j, cnt):
            c = jnp.where(pred(sc_ref[j], j), 1.0, 0.0)
            for u in range(nsub):
                cnt = cnt + c[:, u * LANES:(u + 1) * LANES]
            return cnt
        cnt = lax.fori_loop(0, nkb, body, jnp.zeros((tq, LANES), F32))
        return jnp.sum(cnt, axis=1, keepdims=True)

    any_tie = jnp.max(jnp.max(tied, axis=0, keepdims=True), axis=1, keepdims=True)

    def mask_plain():
        def body(j, carry):
            sc_ref[j] = jnp.where(sc_ref[j] >= thr_b, 0.0, MASK_VALUE)
            return carry
        lax.fori_loop(0, nkb, body, 0)

    def mask_ties():
        cgt = count(lambda s, j: s > thr_b)
        need = kf - cgt
        nbits = int(nkb_total * kb - 1).bit_length()

        def idx_body(i, j0):
            cand = j0 | lax.shift_left(jnp.int32(1), nbits - 1 - i)
            c = count(lambda s, j: (s == thr_b) & (j * kb + col < cand))
            return jnp.where(c < need, cand, j0)

        j0 = lax.fori_loop(0, nbits, idx_body, jnp.zeros((tq, 1), jnp.int32))

        def body(j, carry):
            s = sc_ref[j]
            keep = (s > thr_b) | ((s == thr_b) & (j * kb + col <= j0))
            sc_ref[j] = jnp.where(keep, 0.0, MASK_VALUE)
            return carry
        lax.fori_loop(0, nkb, body, 0)

    lax.cond(any_tie[0, 0] > 0.0, mask_ties, mask_plain)

    m_ref[...] = jnp.full(m_ref.shape, MASK_VALUE, F32)
    acc_ref[...] = jnp.zeros(acc_ref.shape, F32)

    def attn_body(j, carry):
        bias = sc_ref[j]
        for g in range(N_KV_A):
            lg = jnp.dot(qs_ref[g], kt_ref[g, j], preferred_element_type=F32)
            vblk = v_ref[g, j]
            for u in range(GROUP_A):
                r0 = (g * GROUP_A + u) * tq
                l = lg[u * tq:(u + 1) * tq] + bias
                m_old = m_ref[r0:r0 + tq, :]
                m_new = jnp.maximum(m_old, jnp.max(l, axis=1, keepdims=True))
                alpha = jnp.exp(m_old - m_new)
                p = jnp.exp(l - jnp.concatenate([m_new] * nsub, axis=1))
                pv = jnp.dot(p.astype(BF16), vblk, preferred_element_type=F32)
                acc_ref[r0:r0 + tq, :] = alpha * acc_ref[r0:r0 + tq, :] + pv
                m_ref[r0:r0 + tq, :] = m_new
        return carry

    lax.fori_loop(0, nkb, attn_body, 0)

    for hd in range(N_HEADS_A):
        a = acc_ref[hd * tq:(hd + 1) * tq, :]
        o_ref[hd * tq:(hd + 1) * tq, :] = (a[:, :HD_A] / a[:, HD_A:HD_A + 1]).astype(BF16)


def _dsa_call(qs, qi, wi, kt, vaug, kit, *, tq, kb, causal, s_valid, topk):
    b, nqb = qs.shape[0], qs.shape[1]
    nkb_total = kt.shape[2]
    kern = functools.partial(_dsa_kernel, tq=tq, kb=kb, nkb_total=nkb_total, causal=causal,
                             s_valid=s_valid, topk=topk)
    in_specs = [
        pl.BlockSpec((None, None, N_KV_A, GROUP_A * tq, HD_A), lambda i, j: (i, j, 0, 0, 0)),
        pl.BlockSpec((None, None, N_IDX_HEADS * tq, D_IDX), lambda i, j: (i, j, 0, 0)),
        pl.BlockSpec((None, tq, N_IDX_HEADS), lambda i, j: (i, j, 0)),
        pl.BlockSpec((None, N_KV_A, nkb_total, HD_A, kb), lambda i, j: (i, 0, 0, 0, 0)),
        pl.BlockSpec((None, N_KV_A, nkb_total, kb, LANES), lambda i, j: (i, 0, 0, 0, 0)),
        pl.BlockSpec((None, nkb_total, D_IDX, kb), lambda i, j: (i, 0, 0, 0)),
    ]
    out_spec = pl.BlockSpec((None, None, N_HEADS_A * tq, HD_A), lambda i, j: (i, j, 0, 0))
    return pl.pallas_call(
        kern, grid=(b, nqb), in_specs=in_specs, out_specs=out_spec,
        out_shape=jax.ShapeDtypeStruct((b, nqb, N_HEADS_A * tq, HD_A), BF16),
        scratch_shapes=[pltpu.VMEM((nkb_total, tq, kb), F32),
                        pltpu.VMEM((N_DIGITS, nkb_total, tq, kb), BF16),
                        pltpu.VMEM((N_HEADS_A * tq, LANES), F32),
                        pltpu.VMEM((N_HEADS_A * tq, LANES), F32)],
        compiler_params=_cparams(("parallel", "arbitrary")), name="dsa",
    )(qs, qi, wi, kt, vaug, kit)


def _memkv_kernel(mem_ref, gmem_ref, w_ref, gkm_ref, k_ref, v_ref):
    h = _rms(mem_ref[...], gmem_ref[...]).astype(BF16)
    kv = jnp.dot(h, w_ref[...], preferred_element_type=F32)
    gkm = gkm_ref[...]
    for hd in range(N_HEADS_M):
        k_ref[:, hd * HD_M:(hd + 1) * HD_M] = _rms(kv[:, hd * HD_M:(hd + 1) * HD_M], gkm)
    v_ref[...] = kv[:, D_MEM_ATT:]


def _memkv_call(mem2d, gmem, w_kv, gkm, tm):
    n, d = mem2d.shape
    row = lambda i: (i, 0)
    const = lambda i: (0, 0)
    return pl.pallas_call(
        _memkv_kernel, grid=(n // tm,),
        in_specs=[pl.BlockSpec((tm, d), row), pl.BlockSpec((1, d), const),
                  pl.BlockSpec((d, 2 * D_MEM_ATT), const), pl.BlockSpec((1, HD_M), const)],
        out_specs=[pl.BlockSpec((tm, D_MEM_ATT), row), pl.BlockSpec((tm, D_MEM_ATT), row)],
        out_shape=[jax.ShapeDtypeStruct((n, D_MEM_ATT), F32), jax.ShapeDtypeStruct((n, D_MEM_ATT), F32)],
        compiler_params=_cparams(("parallel",)), name="memkv",
    )(mem2d, gmem, w_kv, gkm)


def _merge_kernel(x_ref, a_ref, ub_ref, prev_ref, hist0_ref, qm_ref, mkt_ref, mv_ref,
                  gmix_ref, wg_ref, woa_ref, wob_ref, wom_ref, wout_ref, wpool_ref, spool_ref,
                  o_ref, ext_ref, *, tm, pos0):
    it = pl.program_id(1)
    x = x_ref[...]

    h = _rms(x, gmix_ref[...]).astype(BF16)
    gates = jax.nn.sigmoid(jnp.dot(h, wg_ref[...], preferred_element_type=F32))
    d = x.shape[1]

    ub = ub_ref[...]
    ext_ref[0:HIST_ROWS, :] = jnp.where(it == 0, hist0_ref[...], prev_ref[...])
    ext_ref[HIST_ROWS:HIST_ROWS + tm, :] = ub
    pos = pos0 + it * tm + lax.broadcasted_iota(jnp.int32, (tm, 1), 0)
    ys = []
    for g, w in enumerate(POOL_WINDOWS):
        c0 = g * POOL_GROUP
        win = ub[:, c0:c0 + POOL_GROUP]
        for k in range(1, w):
            win = win + ext_ref[HIST_ROWS - k:HIST_ROWS - k + tm, c0:c0 + POOL_GROUP]
        cnt = jnp.minimum(w, pos + 1).astype(F32)
        pg = (win / cnt - ub[:, c0:c0 + POOL_GROUP]).astype(BF16)
        ys.append(jnp.dot(pg, wpool_ref[g], preferred_element_type=F32))
    bmix = (jnp.concatenate(ys, axis=1) * spool_ref[...]).astype(BF16)

    qm = qm_ref[...]
    ms = []
    for hd in range(N_HEADS_M):
        lg = jnp.dot(qm[:, hd * HD_M:(hd + 1) * HD_M], mkt_ref[hd], preferred_element_type=F32)
        lg = lg * (HD_M ** -0.5)
        e = jnp.exp(lg - jnp.max(lg, axis=1, keepdims=True))
        pr = (e / jnp.sum(e, axis=1, keepdims=True)).astype(BF16)
        ms.append(jnp.dot(pr, mv_ref[hd], preferred_element_type=F32))
    mmix = jnp.concatenate(ms, axis=1).astype(BF16)

    mixed = (gates[:, 0:d] * jnp.dot(a_ref[...], woa_ref[...], preferred_element_type=F32)
             + gates[:, d:2 * d] * jnp.dot(bmix, wob_ref[...], preferred_element_type=F32)
             + gates[:, 2 * d:3 * d] * jnp.dot(mmix, wom_ref[...], preferred_element_type=F32))
    o_ref[...] = x + jnp.dot(mixed.astype(BF16), wout_ref[...], preferred_element_type=F32)


def _merge_call(x, a, ub, hist0, qm, mkt, mv, gmix, wg, woa, wob, wom, wout, wpool, spool, *, tm, pos0):
    b, t, d = x.shape
    nt = t // tm
    hb = tm // HIST_ROWS
    tok = lambda i, j: (i, j, 0)
    c2 = lambda i, j: (0, 0)
    c3 = lambda i, j: (0, 0, 0)
    per_b3 = lambda i, j: (i, 0, 0)
    per_b4 = lambda i, j: (i, 0, 0, 0)
    in_specs = [
        pl.BlockSpec((None, tm, d), tok),
        pl.BlockSpec((None, tm, D_ATT_A), tok),
        pl.BlockSpec((None, tm, D_POOL), tok),
        pl.BlockSpec((None, HIST_ROWS, D_POOL), lambda i, j: (i, jnp.maximum(j * hb - 1, 0), 0)),
        pl.BlockSpec((None, HIST_ROWS, D_POOL), per_b3),
        pl.BlockSpec((None, tm, D_MEM_ATT), tok),
        pl.BlockSpec((None, N_HEADS_M, HD_M, mkt.shape[3]), per_b4),
        pl.BlockSpec((None, N_HEADS_M, mv.shape[2], HD_M), per_b4),
        pl.BlockSpec((1, d), c2),
        pl.BlockSpec(wg.shape, c2),
        pl.BlockSpec(woa.shape, c2), pl.BlockSpec(wob.shape, c2), pl.BlockSpec(wom.shape, c2),
        pl.BlockSpec(wout.shape, c2),
        pl.BlockSpec(wpool.shape, c3),
        pl.BlockSpec((1, D_POOL), c2),
    ]
    return pl.pallas_call(
        functools.partial(_merge_kernel, tm=tm, pos0=pos0), grid=(b, nt), in_specs=in_specs,
        out_specs=pl.BlockSpec((None, tm, d), tok),
        out_shape=jax.ShapeDtypeStruct((b, t, d), F32),
        scratch_shapes=[pltpu.VMEM((HIST_ROWS + tm, D_POOL), F32)],
        compiler_params=_cparams(("parallel", "arbitrary")), name="merge",
    )(x, a, ub, ub, hist0, qm, mkt, mv, gmix, wg, woa, wob, wom, wout, wpool, spool)


def _ffn_kernel(x_ref, g_ref, wgate_ref, wup_ref, wdown_ref, o_ref):
    x = x_ref[...]
    h = _rms(x, g_ref[...]).astype(BF16)
    gate = jnp.dot(h, wgate_ref[...], preferred_element_type=F32)
    up = jnp.dot(h, wup_ref[...], preferred_element_type=F32)
    act = (jax.nn.silu(gate) * up).astype(BF16)
    o_ref[...] = x + jnp.dot(act, wdown_ref[...], preferred_element_type=F32)


def _ffn_call(x2d, g, wgate, wup, wdown, tm):
    n, d = x2d.shape
    row = lambda i: (i, 0)
    const = lambda i: (0, 0)
    return pl.pallas_call(
        _ffn_kernel, grid=(n // tm,),
        in_specs=[pl.BlockSpec((tm, d), row), pl.BlockSpec((1, d), const),
                  pl.BlockSpec(wgate.shape, const), pl.BlockSpec(wup.shape, const),
                  pl.BlockSpec(wdown.shape, const)],
        out_specs=pl.BlockSpec((tm, d), row),
        out_shape=jax.ShapeDtypeStruct((n, d), F32),
        compiler_params=_cparams(("parallel",)), name="ffn",
    )(x2d, g, wgate, wup, wdown)


def _rope_tables(pos, head_dim):
    half = head_dim // 2
    inv = ROPE_THETA ** (-jnp.arange(half, dtype=F32) / half)
    ang = pos.astype(F32)[:, None] * inv[None, :]
    reps = LANES // half
    return jnp.tile(jnp.cos(ang), (1, reps)), jnp.tile(jnp.sin(ang), (1, reps))


def _pack_weights(w_in, g_qa, g_ka, g_kidx, g_qm):
    d = w_in.shape[0]
    widths = (D_ATT_A, D_KV_A, D_KV_A, D_QIDX, D_IDX, N_IDX_HEADS, D_POOL, D_MEM_ATT, N_BRANCH * d)
    cuts = [int(c) for c in np.cumsum(widths)[:-1]]
    wqa, wka, wva, wqi, wki, wwi, wub, wqm, wgates = jnp.split(w_in, cuts, axis=1)
    pad = jnp.zeros((d, LANES - D_IDX - N_IDX_HEADS), w_in.dtype)
    w_cat = jnp.concatenate([wqa, wka, wva, wqi, wki, wwi, pad, wub, wqm], axis=1).astype(BF16)
    gqa_t = jnp.tile(g_qa, N_HEADS_A)[None, :]
    gka_t = jnp.tile(g_ka, N_KV_A)[None, :]
    gki_t = jnp.concatenate([g_kidx, jnp.ones((LANES - D_IDX,), g_kidx.dtype)])[None, :]
    return w_cat, wgates.astype(BF16), gqa_t, gka_t, gki_t, g_qm[None, :]


def _block_diag_ones(n, group, limit=None):
    i = np.arange(n)
    m = (i[:, None] // group) == (i[None, :] // group)
    if limit is not None:
        m = m & (i[:, None] < limit) & (i[None, :] < limit)
    return jnp.asarray(m, BF16)


def _head_major_queries(qa, qi, b, t, tq):
    nqb = t // tq
    qs = qa.reshape(b, nqb, tq, N_KV_A, GROUP_A, HD_A).transpose(0, 1, 3, 4, 2, 5)
    qs = qs.reshape(b, nqb, N_KV_A, GROUP_A * tq, HD_A)
    qis = qi.reshape(b, nqb, tq, N_IDX_HEADS, D_IDX).transpose(0, 1, 3, 2, 4)
    qis = qis.reshape(b, nqb, N_IDX_HEADS * tq, D_IDX)
    return qs, qis


def _key_blocks(k_all, v_all, ki_all, kb):
    b, s = k_all.shape[0], k_all.shape[1]
    nkb = -(-s // kb)
    pad = nkb * kb - s
    if pad:
        k_all = jnp.pad(k_all, ((0, 0), (0, pad), (0, 0), (0, 0)))
        v_all = jnp.pad(v_all, ((0, 0), (0, pad), (0, 0), (0, 0)))
        ki_all = jnp.pad(ki_all, ((0, 0), (0, pad), (0, 0)))
    kt = k_all.astype(BF16).reshape(b, nkb, kb, N_KV_A, HD_A).transpose(0, 3, 1, 4, 2)
    ones = jnp.ones(v_all.shape[:-1] + (1,), BF16)
    zeros = jnp.zeros(v_all.shape[:-1] + (LANES - HD_A - 1,), BF16)
    vaug = jnp.concatenate([v_all.astype(BF16), ones, zeros], axis=-1)
    vaug = vaug.reshape(b, nkb, kb, N_KV_A, LANES).transpose(0, 3, 1, 2, 4)
    kit = ki_all.astype(BF16).reshape(b, nkb, kb, D_IDX).transpose(0, 1, 3, 2)
    return kt, vaug, kit


def _tile(n, pref):
    t = pref
    while n % t:
        t //= 2
    return t


def _group_forward(x, pos, pos0, hist0, k_hist, v_hist, ki_hist, mkt, mv, wts, *, causal):
    (gmix, w_cat, wgates, gqa_t, gka_t, gki_t, gqm, bd64, bdki, wpool, spool,
     woa, wob, wom, wout, gffn, wgate, wup, wdown) = wts
    b, t, d = x.shape
    n = b * t
    tm = _tile(t, 512)
    cosa, sina = _rope_tables(pos, HD_A)
    cosi, sini = _rope_tables(pos, D_IDX)
    qa, ka, va, qi, kw, ub, qm = _proj_call(
        x.reshape(n, d), gmix, w_cat, (cosa, sina, cosi, sini), gqa_t, gka_t, gki_t, gqm, bd64, bdki,
        tm, t // tm)
    ka = ka.reshape(b, t, N_KV_A, HD_A)
    va = va.reshape(b, t, N_KV_A, HD_A)
    ki = kw[:, :D_IDX].reshape(b, t, D_IDX)
    wi = kw[:, D_IDX:D_IDX + N_IDX_HEADS].reshape(b, t, N_IDX_HEADS)
    ub = ub.reshape(b, t, D_POOL)

    if k_hist is None:
        k_all, v_all, ki_all = ka, va, ki
    else:
        k_all = jnp.concatenate([k_hist, ka], axis=1)
        v_all = jnp.concatenate([v_hist, va], axis=1)
        ki_all = jnp.concatenate([ki_hist, ki], axis=1)
    s = k_all.shape[1]
    topk = min(TOPK_MAX, s // 4)
    kb = 256
    tq = kb if causal else _tile(t, 256)
    kt, vaug, kit = _key_blocks(k_all, v_all, ki_all, kb)
    qs, qis = _head_major_queries(qa.reshape(b, t, D_ATT_A), qi.reshape(b, t, D_QIDX), b, t, tq)
    o = _dsa_call(qs, qis, wi, kt, vaug, kit, tq=tq, kb=kb, causal=causal, s_valid=s, topk=topk)
    a = o.reshape(b, t // tq, N_HEADS_A, tq, HD_A).transpose(0, 1, 3, 2, 4).reshape(b, t, D_ATT_A)

    tmm = _tile(t, 256)
    x2 = _merge_call(x, a, ub, hist0, qm.reshape(b, t, D_MEM_ATT), mkt, mv, gmix, wgates,
                     woa, wob, wom, wout, wpool, spool, tm=tmm, pos0=pos0)
    y = _ffn_call(x2.reshape(n, d), gffn, wgate, wup, wdown, _tile(n, 256)).reshape(b, t, d)
    return y, ka, va, ki, ub


def kernel(x_prompt, x_sample, mem_prompt, cache_a_k, cache_a_v, cache_idx_k, cache_pool, cache_mem_k,
           cache_mem_v, g_mix, w_in, g_qa, g_ka, g_kidx, g_qm, g_mem, w_mem_kv, g_km, w_pool, s_pool,
           w_oa, w_ob, w_om, w_out, g_ffn, w_gate, w_up, w_down):
    depth = w_in.shape[0]
    t = x_prompt.shape[1]
    ts = x_sample.shape[1]
    past = cache_a_k.shape[2]
    b = x_prompt.shape[0]
    n_mem = mem_prompt.shape[1]
    pos_p = jnp.arange(t, dtype=jnp.int32)
    pos_s = past + jnp.arange(ts, dtype=jnp.int32)
    bd64 = _block_diag_ones(D_ATT_A, HD_A)
    bdki = _block_diag_ones(LANES, LANES, limit=D_IDX)

    xp, xs = x_prompt, x_sample
    outs = [[] for _ in range(10)]
    for l in range(depth):
        w_cat, wgates, gqa_t, gka_t, gki_t, gqm = _pack_weights(w_in[l], g_qa[l], g_ka[l], g_kidx[l], g_qm[l])
        wts = (g_mix[l][None, :], w_cat, wgates, gqa_t, gka_t, gki_t, gqm, bd64, bdki,
               w_pool[l].astype(BF16), s_pool[l][None, :],
               w_oa[l].astype(BF16), w_ob[l].astype(BF16), w_om[l].astype(BF16), w_out[l].astype(BF16),
               g_ffn[l][None, :], w_gate[l].astype(BF16), w_up[l].astype(BF16), w_down[l].astype(BF16))

        mk, mv = _memkv_call(mem_prompt.reshape(b * n_mem, -1), g_mem[l][None, :],
                             w_mem_kv[l].astype(BF16), g_km[l][None, :], _tile(b * n_mem, 256))
        mk = mk.reshape(b, n_mem, N_HEADS_M, HD_M)
        mv = mv.reshape(b, n_mem, N_HEADS_M, HD_M)
        hist0 = jnp.zeros((b, HIST_ROWS, D_POOL), F32)
        xp, ka, va, ki, ub = _group_forward(
            xp, pos_p, 0, hist0, None, None, None,
            mk.astype(BF16).transpose(0, 2, 3, 1), mv.astype(BF16).transpose(0, 2, 1, 3), wts, causal=True)
        for lst, val in zip(outs[:6], (ka, va, ki, ub[:, -POOL_HIST:], mk, mv)):
            lst.append(val)

        bs = xs.shape[0]
        hist0 = jnp.concatenate([jnp.zeros((bs, HIST_ROWS - POOL_HIST, D_POOL), F32), cache_pool[l]], axis=1)
        xs, ka, va, ki, ub = _group_forward(
            xs, pos_s, past, hist0, cache_a_k[l], cache_a_v[l], cache_idx_k[l],
            cache_mem_k[l].astype(BF16).transpose(0, 2, 3, 1), cache_mem_v[l].astype(BF16).transpose(0, 2, 1, 3),
            wts, causal=False)
        pool_s = jnp.concatenate([cache_pool[l], ub], axis=1)[:, -POOL_HIST:]
        for lst, val in zip(outs[6:], (ka, va, ki, pool_s)):
            lst.append(val)

    stacked = [jnp.stack(o) for o in outs]
    return (xp, xs, *stacked)
```

```python
import functools

import jax
import jax.numpy as jnp
import numpy as np
from jax import lax
from jax.experimental import pallas as pl
from jax.experimental.pallas import tpu as pltpu

F32 = jnp.float32
BF16 = jnp.bfloat16

CHUNK = 64
EPS = 1e-6
ROPE_THETA = 10000.0
N_HEADS_A = 8
N_KV_A = 2
GROUP_A = N_HEADS_A // N_KV_A
HD_A = 64
D_ATT_A = N_HEADS_A * HD_A
D_KV_A = N_KV_A * HD_A
N_IDX_HEADS = 8
D_IDX = 32
D_QIDX = N_IDX_HEADS * D_IDX
IDX_SCALE = (N_IDX_HEADS * D_IDX) ** -0.5
TOPK_MAX = 256
POOL_WINDOWS = (2, 4, 8, 16)
POOL_GROUP = 128
D_POOL = len(POOL_WINDOWS) * POOL_GROUP
POOL_HIST = 15
N_HEADS_M = 4
HD_M = 128
D_MEM_ATT = N_HEADS_M * HD_M
N_BRANCH = 3

LANES = 128
SUBLANES = 8
V7X_VMEM_BYTES = 64 * 1024 * 1024
VMEM_LIMIT = 56 * 1024 * 1024

HIST_ROWS = 2 * SUBLANES
MASK_VALUE = -1e30
FLT_MAX = float(np.finfo(np.float32).max)
KEY_BITS = 32
Q_SCALE = HD_A ** -0.5 * float(np.log2(np.e))
SUM_FLOOR = 1e-30
COUNT_ROWS = 4 * SUBLANES
COUNT_CHAINS = 2
V_ROWS = HD_A + 2 * SUBLANES

C_QA = 0
C_KA = C_QA + D_ATT_A
C_VA = C_KA + D_KV_A
C_QI = C_VA + D_KV_A
C_KW = C_QI + D_QIDX
C_UB = C_KW + LANES
C_QM = C_UB + D_POOL
C_END = C_QM + D_MEM_ATT


def _cparams(sem):
    return pltpu.CompilerParams(dimension_semantics=sem, vmem_limit_bytes=VMEM_LIMIT)


def _rms(x, g):
    ms = jnp.mean(x * x, axis=-1, keepdims=True)
    return x * lax.rsqrt(ms + EPS) * g


def _group_sumsq(x, bd):
    sq = x * x
    hi = sq.astype(BF16)
    lo = (sq - hi.astype(F32)).astype(BF16)
    return (jnp.dot(hi, bd, preferred_element_type=F32)
            + jnp.dot(lo, bd, preferred_element_type=F32))


def _rope_lanes(x, cos, sin, half):
    lane = lax.broadcasted_iota(jnp.int32, x.shape, 1)
    first = (lane % (2 * half)) < half
    left = pltpu.roll(x, LANES - half, 1)
    right = pltpu.roll(x, half, 1)
    rot = jnp.where(first, -left, right)
    return x * cos + rot * sin


def _proj_kernel(x_ref, gmix_ref, w_ref, cosa_ref, sina_ref, cosi_ref, sini_ref,
                 gqa_ref, gka_ref, gki_ref, gqm_ref, bd64_ref, bdki_ref,
                 qa_ref, ka_ref, va_ref, qi_ref, kw_ref, ub_ref, qm_ref):
    x = x_ref[...]
    h = _rms(x, gmix_ref[...]).astype(BF16)
    p = jnp.dot(h, w_ref[...], preferred_element_type=F32)
    cosa, sina = cosa_ref[...], sina_ref[...]
    cosi, sini = cosi_ref[...], sini_ref[...]
    bd64 = bd64_ref[...]

    qa = p[:, C_QA:C_QA + D_ATT_A]
    ss = _group_sumsq(qa, bd64)
    qa = qa * lax.rsqrt(ss * (1.0 / HD_A) + EPS) * gqa_ref[...]
    for c in range(D_ATT_A // LANES):
        blk = _rope_lanes(qa[:, c * LANES:(c + 1) * LANES], cosa, sina, HD_A // 2)
        qa_ref[:, c * LANES:(c + 1) * LANES] = (blk * Q_SCALE).astype(BF16)

    ka = p[:, C_KA:C_KA + D_KV_A]
    ss = _group_sumsq(ka, bd64[:D_KV_A, :D_KV_A])
    ka = ka * lax.rsqrt(ss * (1.0 / HD_A) + EPS) * gka_ref[...]
    ka_ref[...] = _rope_lanes(ka, cosa, sina, HD_A // 2)
    va_ref[...] = p[:, C_VA:C_VA + D_KV_A]

    for c in range(D_QIDX // LANES):
        blk = p[:, C_QI + c * LANES:C_QI + (c + 1) * LANES]
        qi_ref[:, c * LANES:(c + 1) * LANES] = _rope_lanes(blk, cosi, sini, D_IDX // 2).astype(BF16)

    kw = p[:, C_KW:C_KW + LANES]
    ss = _group_sumsq(kw, bdki_ref[...])
    kin = kw * lax.rsqrt(ss * (1.0 / D_IDX) + EPS) * gki_ref[...]
    kin = _rope_lanes(kin, cosi, sini, D_IDX // 2)
    lane = lax.broadcasted_iota(jnp.int32, kw.shape, 1)
    kw_ref[...] = jnp.where(lane < D_IDX, kin, kw * IDX_SCALE)

    ub_ref[...] = p[:, C_UB:C_UB + D_POOL]

    gqm = gqm_ref[...]
    for hd in range(N_HEADS_M):
        blk = p[:, C_QM + hd * HD_M:C_QM + (hd + 1) * HD_M]
        qm_ref[:, hd * HD_M:(hd + 1) * HD_M] = _rms(blk, gqm).astype(BF16)


def _proj_call(x2d, gmix, w_cat, tabs, gqa_t, gka_t, gki_t, gqm, bd64, bdki, tm, n_tab_blocks):
    n = x2d.shape[0]
    d = x2d.shape[1]
    grid = (n // tm,)
    row = lambda i: (i, 0)
    const = lambda i: (0, 0)
    tab = lambda i: (i % n_tab_blocks, 0)
    in_specs = [
        pl.BlockSpec((tm, d), row),
        pl.BlockSpec((1, d), const),
        pl.BlockSpec((d, C_END), const),
        pl.BlockSpec((tm, LANES), tab), pl.BlockSpec((tm, LANES), tab),
        pl.BlockSpec((tm, LANES), tab), pl.BlockSpec((tm, LANES), tab),
        pl.BlockSpec((1, D_ATT_A), const), pl.BlockSpec((1, D_KV_A), const),
        pl.BlockSpec((1, LANES), const), pl.BlockSpec((1, HD_M), const),
        pl.BlockSpec((D_ATT_A, D_ATT_A), const), pl.BlockSpec((LANES, LANES), const),
    ]
    out_shape = [
        jax.ShapeDtypeStruct((n, D_ATT_A), BF16),
        jax.ShapeDtypeStruct((n, D_KV_A), F32),
        jax.ShapeDtypeStruct((n, D_KV_A), F32),
        jax.ShapeDtypeStruct((n, D_QIDX), BF16),
        jax.ShapeDtypeStruct((n, LANES), F32),
        jax.ShapeDtypeStruct((n, D_POOL), F32),
        jax.ShapeDtypeStruct((n, D_MEM_ATT), BF16),
    ]
    out_specs = [pl.BlockSpec((tm, s.shape[1]), row) for s in out_shape]
    return pl.pallas_call(
        _proj_kernel, grid=grid, in_specs=in_specs, out_specs=out_specs, out_shape=out_shape,
        compiler_params=_cparams(("parallel",)), name="proj",
    )(x2d, gmix, w_cat, *tabs, gqa_t, gka_t, gki_t, gqm, bd64, bdki)


def _key_bits_to_float(u):
    k = u ^ jnp.int32(-2147483648)
    bits = jnp.where(k >= 0, k, k ^ jnp.int32(0x7FFFFFFF))
    return lax.bitcast_convert_type(bits, F32)


def _dsa_kernel(qt_ref, qit_ref, wit_ref, k_ref, vt_ref, ki_ref, o_ref,
                sc_ref, m_ref, acc_ref, qaug_ref, km_ref,
                *, tq, kb, nkb_total, causal, s_valid, topk):
    qblk = pl.program_id(1)
    if causal:
        nkb = lax.div(qblk + kb // tq, jnp.int32(kb // tq))
    else:
        nkb = nkb_total
    qpos = qblk * tq + lax.broadcasted_iota(jnp.int32, (1, tq), 1)
    if causal:
        lim = (lax.shift_right_logical(qpos, 6) + 1) * CHUNK
    else:
        lim = jnp.full((1, tq), s_valid, jnp.int32)
    limf = lim.astype(F32)
    krow = lax.broadcasted_iota(jnp.int32, (kb, tq), 0)
    kf = float(topk)

    qit = qit_ref[...]
    wit = wit_ref[...]

    def score_body(j, carry):
        z = jnp.dot(ki_ref[j], qit, preferred_element_type=F32)
        s = jnp.zeros((kb, tq), F32)
        for h in range(N_IDX_HEADS):
            s = s + wit[h:h + 1, :] * jnp.maximum(z[:, h * tq:(h + 1) * tq], 0.0)
        sc_ref[j] = jnp.where(j * kb + krow < lim, s, -jnp.inf)
        return carry

    lax.fori_loop(0, nkb, score_body, 0)

    crow = lax.broadcasted_iota(jnp.int32, (COUNT_ROWS, tq), 0)

    def count(pred):
        def body(j, accs):
            accs = list(accs)
            for ci, r in enumerate(range(0, kb, COUNT_ROWS)):
                hit = pred(sc_ref[j, r:r + COUNT_ROWS, :], j * kb + r + crow)
                accs[ci % COUNT_CHAINS] = accs[ci % COUNT_CHAINS] + jnp.where(hit, 1.0, 0.0)
            return tuple(accs)
        zero = jnp.zeros((COUNT_ROWS, tq), F32)
        accs = lax.fori_loop(0, nkb, body, (zero,) * COUNT_CHAINS)
        return jnp.sum(sum(accs[1:], accs[0]), axis=0, keepdims=True)

    def radix_body(i, carry):
        prefix, cge = carry
        cand = prefix | lax.shift_left(jnp.int32(1), KEY_BITS - 1 - i)
        t = _key_bits_to_float(cand)
        c = count(lambda s, idx: s >= t)
        take = c >= kf
        return jnp.where(take, cand, prefix), jnp.where(take, c, cge)

    prefix, cge = lax.fori_loop(0, KEY_BITS, radix_body, (jnp.zeros((1, tq), jnp.int32), limf))
    thr = jnp.where(limf >= kf, _key_bits_to_float(prefix), -FLT_MAX)

    any_tie = jnp.max(jnp.where(cge > kf, 1.0, 0.0), axis=1, keepdims=True)

    def mask_plain():
        def body(j, carry):
            sc_ref[j] = jnp.where(sc_ref[j] >= thr, 0.0, MASK_VALUE)
            return carry
        lax.fori_loop(0, nkb, body, 0)

    def mask_ties():
        cgt = count(lambda s, idx: s > thr)
        need = kf - cgt
        nbits = int(nkb_total * kb - 1).bit_length()

        def idx_body(i, j0):
            cand = j0 | lax.shift_left(jnp.int32(1), nbits - 1 - i)
            c = count(lambda s, idx: (s == thr) & (idx < cand))
            return jnp.where(c < need, cand, j0)

        j0 = lax.fori_loop(0, nbits, idx_body, jnp.zeros((1, tq), jnp.int32))

        def body(j, carry):
            s = sc_ref[j]
            keep = (s > thr) | ((s == thr) & (j * kb + krow <= j0))
            sc_ref[j] = jnp.where(keep, 0.0, MASK_VALUE)
            return carry
        lax.fori_loop(0, nkb, body, 0)

    lax.cond(any_tie[0, 0] > 0.0, mask_ties, mask_plain)

    @pl.when(qblk == 0)
    def _():
        for g in range(N_KV_A):
            def norm_body(j, mx, g=g):
                kk = k_ref[g, j].astype(F32)
                return jnp.maximum(mx, jnp.sum(kk * kk, axis=1, keepdims=True))
            mx = lax.fori_loop(0, nkb_total, norm_body, jnp.zeros((kb, 1), F32))
            km_ref[g] = jnp.broadcast_to(jnp.max(mx, axis=0, keepdims=True), (SUBLANES, LANES))

    aug_row = lax.broadcasted_iota(jnp.int32, (2 * SUBLANES, GROUP_A * tq), 0)
    for g in range(N_KV_A):
        q = qt_ref[g]
        qaug_ref[g] = q
        qf = q.astype(F32)
        qn2 = jnp.sum(qf * qf, axis=0, keepdims=True)
        kmax2 = jnp.concatenate([km_ref[g][0:1, :]] * (GROUP_A * tq // LANES), axis=1)
        shift = jnp.sqrt(qn2 * kmax2)
        qaug_ref[g, HD_A:HD_A + 2 * SUBLANES, :] = jnp.where(aug_row == 0, -shift, 0.0).astype(BF16)

    acc_ref[...] = jnp.zeros(acc_ref.shape, F32)

    def attn_body(j, carry):
        bias = sc_ref[j]
        for g in range(N_KV_A):
            lg = jnp.dot(k_ref[g, j], qaug_ref[g], preferred_element_type=F32)
            vt = vt_ref[g, j]
            for u in range(GROUP_A):
                hd = g * GROUP_A + u
                p = jnp.exp2(lg[:, u * tq:(u + 1) * tq] + bias).astype(BF16)
                acc_ref[hd] += jnp.dot(vt, p, preferred_element_type=F32)
        return carry

    lax.fori_loop(0, nkb, attn_body, 0)

    sums_ok = jnp.ones((1, tq), F32)
    for hd in range(N_HEADS_A):
        sums_ok = jnp.where(acc_ref[hd][HD_A:HD_A + 1, :] > SUM_FLOOR, sums_ok, 0.0)
    all_ok = jnp.min(sums_ok, axis=1, keepdims=True)

    @pl.when(all_ok[0, 0] < 1.0)
    def _():
        m_ref[...] = jnp.full(m_ref.shape, MASK_VALUE, F32)
        acc_ref[...] = jnp.zeros(acc_ref.shape, F32)

        def online_body(j, carry):
            bias = sc_ref[j]
            for g in range(N_KV_A):
                lg = jnp.dot(k_ref[g, j], qaug_ref[g], preferred_element_type=F32)
                vt = vt_ref[g, j]
                for u in range(GROUP_A):
                    hd = g * GROUP_A + u
                    l = lg[:, u * tq:(u + 1) * tq] + bias
                    m_old = m_ref[hd:hd + 1, :]
                    m_new = jnp.maximum(m_old, jnp.max(l, axis=0, keepdims=True))
                    p = jnp.exp2(l - m_new).astype(BF16)
                    pv = jnp.dot(vt, p, preferred_element_type=F32)
                    acc_ref[hd] = jnp.exp2(m_old - m_new) * acc_ref[hd] + pv
                    m_ref[hd:hd + 1, :] = m_new
            return carry

        lax.fori_loop(0, nkb, online_body, 0)

    for hd in range(N_HEADS_A):
        a = acc_ref[hd]
        o_ref[hd * HD_A:(hd + 1) * HD_A, :] = (a[:HD_A, :] / a[HD_A:HD_A + 1, :]).astype(BF16)


def _dsa_call(qt, qit, wit, k, vt, ki, *, tq, kb, causal, s_valid, topk):
    b, nqb = qt.shape[0], qt.shape[1]
    nkb_total = k.shape[2]
    kern = functools.partial(_dsa_kernel, tq=tq, kb=kb, nkb_total=nkb_total, causal=causal,
                             s_valid=s_valid, topk=topk)
    in_specs = [
        pl.BlockSpec((None, None, N_KV_A, LANES, GROUP_A * tq), lambda i, j: (i, j, 0, 0, 0)),
        pl.BlockSpec((None, None, D_IDX, N_IDX_HEADS * tq), lambda i, j: (i, j, 0, 0)),
        pl.BlockSpec((None, None, N_IDX_HEADS, tq), lambda i, j: (i, j, 0, 0)),
        pl.BlockSpec((None, N_KV_A, nkb_total, kb, LANES), lambda i, j: (i, 0, 0, 0, 0)),
        pl.BlockSpec((None, N_KV_A, nkb_total, V_ROWS, kb), lambda i, j: (i, 0, 0, 0, 0)),
        pl.BlockSpec((None, nkb_total, kb, D_IDX), lambda i, j: (i, 0, 0, 0)),
    ]
    out_spec = pl.BlockSpec((None, None, N_HEADS_A * HD_A, tq), lambda i, j: (i, j, 0, 0))
    return pl.pallas_call(
        kern, grid=(b, nqb), in_specs=in_specs, out_specs=out_spec,
        out_shape=jax.ShapeDtypeStruct((b, nqb, N_HEADS_A * HD_A, tq), BF16),
        scratch_shapes=[pltpu.VMEM((nkb_total, kb, tq), F32),
                        pltpu.VMEM((N_HEADS_A, tq), F32),
                        pltpu.VMEM((N_HEADS_A, V_ROWS, tq), F32),
                        pltpu.VMEM((N_KV_A, LANES, GROUP_A * tq), BF16),
                        pltpu.VMEM((N_KV_A, SUBLANES, LANES), F32)],
        compiler_params=_cparams(("parallel", "arbitrary")), name="dsa",
    )(qt, qit, wit, k, vt, ki)


def _memkv_kernel(mem_ref, gmem_ref, w_ref, gkm_ref, k_ref, v_ref):
    h = _rms(mem_ref[...], gmem_ref[...]).astype(BF16)
    kv = jnp.dot(h, w_ref[...], preferred_element_type=F32)
    gkm = gkm_ref[...]
    for hd in range(N_HEADS_M):
        k_ref[:, hd * HD_M:(hd + 1) * HD_M] = _rms(kv[:, hd * HD_M:(hd + 1) * HD_M], gkm)
    v_ref[...] = kv[:, D_MEM_ATT:]


def _memkv_call(mem2d, gmem, w_kv, gkm, tm):
    n, d = mem2d.shape
    row = lambda i: (i, 0)
    const = lambda i: (0, 0)
    return pl.pallas_call(
        _memkv_kernel, grid=(n // tm,),
        in_specs=[pl.BlockSpec((tm, d), row), pl.BlockSpec((1, d), const),
                  pl.BlockSpec((d, 2 * D_MEM_ATT), const), pl.BlockSpec((1, HD_M), const)],
        out_specs=[pl.BlockSpec((tm, D_MEM_ATT), row), pl.BlockSpec((tm, D_MEM_ATT), row)],
        out_shape=[jax.ShapeDtypeStruct((n, D_MEM_ATT), F32), jax.ShapeDtypeStruct((n, D_MEM_ATT), F32)],
        compiler_params=_cparams(("parallel",)), name="memkv",
    )(mem2d, gmem, w_kv, gkm)


def _merge_kernel(x_ref, a_ref, ub_ref, prev_ref, hist0_ref, qm_ref, mkt_ref, mv_ref,
                  gmix_ref, wg_ref, woa_ref, wob_ref, wom_ref, wout_ref, wpool_ref, spool_ref,
                  o_ref, ext_ref, *, tm, pos0):
    it = pl.program_id(1)
    x = x_ref[...]

    h = _rms(x, gmix_ref[...]).astype(BF16)
    gates = jax.nn.sigmoid(jnp.dot(h, wg_ref[...], preferred_element_type=F32))
    d = x.shape[1]

    ub = ub_ref[...]
    ext_ref[0:HIST_ROWS, :] = jnp.where(it == 0, hist0_ref[...], prev_ref[...])
    ext_ref[HIST_ROWS:HIST_ROWS + tm, :] = ub
    pos = pos0 + it * tm + lax.broadcasted_iota(jnp.int32, (tm, 1), 0)
    ys = []
    for g, w in enumerate(POOL_WINDOWS):
        c0 = g * POOL_GROUP
        win = ub[:, c0:c0 + POOL_GROUP]
        for k in range(1, w):
            win = win + ext_ref[HIST_ROWS - k:HIST_ROWS - k + tm, c0:c0 + POOL_GROUP]
        cnt = jnp.minimum(w, pos + 1).astype(F32)
        pg = (win / cnt - ub[:, c0:c0 + POOL_GROUP]).astype(BF16)
        ys.append(jnp.dot(pg, wpool_ref[g], preferred_element_type=F32))
    bmix = (jnp.concatenate(ys, axis=1) * spool_ref[...]).astype(BF16)

    qm = qm_ref[...]
    ms = []
    for hd in range(N_HEADS_M):
        lg = jnp.dot(qm[:, hd * HD_M:(hd + 1) * HD_M], mkt_ref[hd], preferred_element_type=F32)
        lg = lg * (HD_M ** -0.5)
        e = jnp.exp(lg - jnp.max(lg, axis=1, keepdims=True))
        pr = (e / jnp.sum(e, axis=1, keepdims=True)).astype(BF16)
        ms.append(jnp.dot(pr, mv_ref[hd], preferred_element_type=F32))
    mmix = jnp.concatenate(ms, axis=1).astype(BF16)

    mixed = (gates[:, 0:d] * jnp.dot(a_ref[...], woa_ref[...], preferred_element_type=F32)
             + gates[:, d:2 * d] * jnp.dot(bmix, wob_ref[...], preferred_element_type=F32)
             + gates[:, 2 * d:3 * d] * jnp.dot(mmix, wom_ref[...], preferred_element_type=F32))
    o_ref[...] = x + jnp.dot(mixed.astype(BF16), wout_ref[...], preferred_element_type=F32)


def _merge_call(x, a, ub, hist0, qm, mkt, mv, gmix, wg, woa, wob, wom, wout, wpool, spool, *, tm, pos0):
    b, t, d = x.shape
    nt = t // tm
    hb = tm // HIST_ROWS
    tok = lambda i, j: (i, j, 0)
    c2 = lambda i, j: (0, 0)
    c3 = lambda i, j: (0, 0, 0)
    per_b3 = lambda i, j: (i, 0, 0)
    per_b4 = lambda i, j: (i, 0, 0, 0)
    in_specs = [
        pl.BlockSpec((None, tm, d), tok),
        pl.BlockSpec((None, tm, D_ATT_A), tok),
        pl.BlockSpec((None, tm, D_POOL), tok),
        pl.BlockSpec((None, HIST_ROWS, D_POOL), lambda i, j: (i, jnp.maximum(j * hb - 1, 0), 0)),
        pl.BlockSpec((None, HIST_ROWS, D_POOL), per_b3),
        pl.BlockSpec((None, tm, D_MEM_ATT), tok),
        pl.BlockSpec((None, N_HEADS_M, HD_M, mkt.shape[3]), per_b4),
        pl.BlockSpec((None, N_HEADS_M, mv.shape[2], HD_M), per_b4),
        pl.BlockSpec((1, d), c2),
        pl.BlockSpec(wg.shape, c2),
        pl.BlockSpec(woa.shape, c2), pl.BlockSpec(wob.shape, c2), pl.BlockSpec(wom.shape, c2),
        pl.BlockSpec(wout.shape, c2),
        pl.BlockSpec(wpool.shape, c3),
        pl.BlockSpec((1, D_POOL), c2),
    ]
    return pl.pallas_call(
        functools.partial(_merge_kernel, tm=tm, pos0=pos0), grid=(b, nt), in_specs=in_specs,
        out_specs=pl.BlockSpec((None, tm, d), tok),
        out_shape=jax.ShapeDtypeStruct((b, t, d), F32),
        scratch_shapes=[pltpu.VMEM((HIST_ROWS + tm, D_POOL), F32)],
        compiler_params=_cparams(("parallel", "arbitrary")), name="merge",
    )(x, a, ub, ub, hist0, qm, mkt, mv, gmix, wg, woa, wob, wom, wout, wpool, spool)


def _ffn_kernel(x_ref, g_ref, wgate_ref, wup_ref, wdown_ref, o_ref):
    x = x_ref[...]
    h = _rms(x, g_ref[...]).astype(BF16)
    gate = jnp.dot(h, wgate_ref[...], preferred_element_type=F32)
    up = jnp.dot(h, wup_ref[...], preferred_element_type=F32)
    act = (jax.nn.silu(gate) * up).astype(BF16)
    o_ref[...] = x + jnp.dot(act, wdown_ref[...], preferred_element_type=F32)


def _ffn_call(x2d, g, wgate, wup, wdown, tm):
    n, d = x2d.shape
    row = lambda i: (i, 0)
    const = lambda i: (0, 0)
    return pl.pallas_call(
        _ffn_kernel, grid=(n // tm,),
        in_specs=[pl.BlockSpec((tm, d), row), pl.BlockSpec((1, d), const),
                  pl.BlockSpec(wgate.shape, const), pl.BlockSpec(wup.shape, const),
                  pl.BlockSpec(wdown.shape, const)],
        out_specs=pl.BlockSpec((tm, d), row),
        out_shape=jax.ShapeDtypeStruct((n, d), F32),
        compiler_params=_cparams(("parallel",)), name="ffn",
    )(x2d, g, wgate, wup, wdown)


def _rope_tables(pos, head_dim):
    half = head_dim // 2
    inv = ROPE_THETA ** (-jnp.arange(half, dtype=F32) / half)
    ang = pos.astype(F32)[:, None] * inv[None, :]
    reps = LANES // half
    return jnp.tile(jnp.cos(ang), (1, reps)), jnp.tile(jnp.sin(ang), (1, reps))


def _pack_weights(w_in, g_qa, g_ka, g_kidx, g_qm):
    d = w_in.shape[0]
    widths = (D_ATT_A, D_KV_A, D_KV_A, D_QIDX, D_IDX, N_IDX_HEADS, D_POOL, D_MEM_ATT, N_BRANCH * d)
    cuts = [int(c) for c in np.cumsum(widths)[:-1]]
    wqa, wka, wva, wqi, wki, wwi, wub, wqm, wgates = jnp.split(w_in, cuts, axis=1)
    pad = jnp.zeros((d, LANES - D_IDX - N_IDX_HEADS), w_in.dtype)
    w_cat = jnp.concatenate([wqa, wka, wva, wqi, wki, wwi, pad, wub, wqm], axis=1).astype(BF16)
    gqa_t = jnp.tile(g_qa, N_HEADS_A)[None, :]
    gka_t = jnp.tile(g_ka, N_KV_A)[None, :]
    gki_t = jnp.concatenate([g_kidx, jnp.ones((LANES - D_IDX,), g_kidx.dtype)])[None, :]
    return w_cat, wgates.astype(BF16), gqa_t, gka_t, gki_t, g_qm[None, :]


def _block_diag_ones(n, group, limit=None):
    i = np.arange(n)
    m = (i[:, None] // group) == (i[None, :] // group)
    if limit is not None:
        m = m & (i[:, None] < limit) & (i[None, :] < limit)
    return jnp.asarray(m, BF16)


def _query_blocks(qa, qi, wi, b, t, tq):
    nqb = t // tq
    qt = qa.reshape(b, nqb, tq, N_KV_A, GROUP_A, HD_A).transpose(0, 1, 3, 5, 4, 2)
    qt = qt.reshape(b, nqb, N_KV_A, HD_A, GROUP_A * tq)
    qt = jnp.pad(qt, ((0, 0), (0, 0), (0, 0), (0, LANES - HD_A), (0, 0)))
    qit = qi.reshape(b, nqb, tq, N_IDX_HEADS, D_IDX).transpose(0, 1, 4, 3, 2)
    qit = qit.reshape(b, nqb, D_IDX, N_IDX_HEADS * tq)
    wit = wi.reshape(b, nqb, tq, N_IDX_HEADS).transpose(0, 1, 3, 2)
    return qt, qit, wit


def _key_blocks(k_all, v_all, ki_all, kb):
    b, s = k_all.shape[0], k_all.shape[1]
    nkb = -(-s // kb)
    pad = nkb * kb - s
    if pad:
        k_all = jnp.pad(k_all, ((0, 0), (0, pad), (0, 0), (0, 0)))
        v_all = jnp.pad(v_all, ((0, 0), (0, pad), (0, 0), (0, 0)))
        ki_all = jnp.pad(ki_all, ((0, 0), (0, pad), (0, 0)))
    ones = jnp.ones(v_all.shape[:-1] + (1,), BF16)
    zeros = jnp.zeros(v_all.shape[:-1] + (LANES - HD_A - 1,), BF16)
    kaug = jnp.concatenate([k_all.astype(BF16), ones, zeros], axis=-1)
    k = kaug.reshape(b, nkb, kb, N_KV_A, LANES).transpose(0, 3, 1, 2, 4)
    vaug = jnp.concatenate([v_all.astype(BF16), ones, zeros[..., :V_ROWS - HD_A - 1]], axis=-1)
    vt = vaug.reshape(b, nkb, kb, N_KV_A, V_ROWS).transpose(0, 3, 1, 4, 2)
    ki = ki_all.astype(BF16).reshape(b, nkb, kb, D_IDX)
    return k, vt, ki


def _tile(n, pref):
    t = pref
    while n % t:
        t //= 2
    return t


def _group_forward(x, pos, pos0, hist0, k_hist, v_hist, ki_hist, mkt, mv, wts, *, causal):
    (gmix, w_cat, wgates, gqa_t, gka_t, gki_t, gqm, bd64, bdki, wpool, spool,
     woa, wob, wom, wout, gffn, wgate, wup, wdown) = wts
    b, t, d = x.shape
    n = b * t
    tm = _tile(t, 512)
    cosa, sina = _rope_tables(pos, HD_A)
    cosi, sini = _rope_tables(pos, D_IDX)
    qa, ka, va, qi, kw, ub, qm = _proj_call(
        x.reshape(n, d), gmix, w_cat, (cosa, sina, cosi, sini), gqa_t, gka_t, gki_t, gqm, bd64, bdki,
        tm, t // tm)
    ka = ka.reshape(b, t, N_KV_A, HD_A)
    va = va.reshape(b, t, N_KV_A, HD_A)
    ki = kw[:, :D_IDX].reshape(b, t, D_IDX)
    wi = kw[:, D_IDX:D_IDX + N_IDX_HEADS]
    ub = ub.reshape(b, t, D_POOL)

    if k_hist is None:
        k_all, v_all, ki_all = ka, va, ki
    else:
        k_all = jnp.concatenate([k_hist, ka], axis=1)
        v_all = jnp.concatenate([v_hist, va], axis=1)
        ki_all = jnp.concatenate([ki_hist, ki], axis=1)
    s = k_all.shape[1]
    topk = min(TOPK_MAX, s // 4)
    tq = _tile(t, 256)
    kb = 2 * tq if causal and s % (2 * tq) == 0 else 256
    kblk, vt, kiblk = _key_blocks(k_all, v_all, ki_all, kb)
    qt, qit, wit = _query_blocks(qa, qi, wi, b, t, tq)
    o = _dsa_call(qt, qit, wit, kblk, vt, kiblk, tq=tq, kb=kb, causal=causal, s_valid=s, topk=topk)
    a = o.transpose(0, 1, 3, 2).reshape(b, t, D_ATT_A)

    tmm = _tile(t, 256)
    x2 = _merge_call(x, a, ub, hist0, qm.reshape(b, t, D_MEM_ATT), mkt, mv, gmix, wgates,
                     woa, wob, wom, wout, wpool, spool, tm=tmm, pos0=pos0)
    y = _ffn_call(x2.reshape(n, d), gffn, wgate, wup, wdown, _tile(n, 256)).reshape(b, t, d)
    return y, ka, va, ki, ub


def kernel(x_prompt, x_sample, mem_prompt, cache_a_k, cache_a_v, cache_idx_k, cache_pool, cache_mem_k,
           cache_mem_v, g_mix, w_in, g_qa, g_ka, g_kidx, g_qm, g_mem, w_mem_kv, g_km, w_pool, s_pool,
           w_oa, w_ob, w_om, w_out, g_ffn, w_gate, w_up, w_down):
    depth = w_in.shape[0]
    t = x_prompt.shape[1]
    ts = x_sample.shape[1]
    past = cache_a_k.shape[2]
    b = x_prompt.shape[0]
    n_mem = mem_prompt.shape[1]
    pos_p = jnp.arange(t, dtype=jnp.int32)
    pos_s = past + jnp.arange(ts, dtype=jnp.int32)
    bd64 = _block_diag_ones(D_ATT_A, HD_A)
    bdki = _block_diag_ones(LANES, LANES, limit=D_IDX)

    xp, xs = x_prompt, x_sample
    outs = [[] for _ in range(10)]
    for l in range(depth):
        w_cat, wgates, gqa_t, gka_t, gki_t, gqm = _pack_weights(w_in[l], g_qa[l], g_ka[l], g_kidx[l], g_qm[l])
        wts = (g_mix[l][None, :], w_cat, wgates, gqa_t, gka_t, gki_t, gqm, bd64, bdki,
               w_pool[l].astype(BF16), s_pool[l][None, :],
               w_oa[l].astype(BF16), w_ob[l].astype(BF16), w_om[l].astype(BF16), w_out[l].astype(BF16),
               g_ffn[l][None, :], w_gate[l].astype(BF16), w_up[l].astype(BF16), w_down[l].astype(BF16))

        mk, mv = _memkv_call(mem_prompt.reshape(b * n_mem, -1), g_mem[l][None, :],
                             w_mem_kv[l].astype(BF16), g_km[l][None, :], _tile(b * n_mem, 256))
        mk = mk.reshape(b, n_mem, N_HEADS_M, HD_M)
        mv = mv.reshape(b, n_mem, N_HEADS_M, HD_M)
        hist0 = jnp.zeros((b, HIST_ROWS, D_POOL), F32)
        xp, ka, va, ki, ub = _group_forward(
            xp, pos_p, 0, hist0, None, None, None,
            mk.astype(BF16).transpose(0, 2, 3, 1), mv.astype(BF16).transpose(0, 2, 1, 3), wts, causal=True)
        for lst, val in zip(outs[:6], (ka, va, ki, ub[:, -POOL_HIST:], mk, mv)):
            lst.append(val)

        bs = xs.shape[0]
        hist0 = jnp.concatenate([jnp.zeros((bs, HIST_ROWS - POOL_HIST, D_POOL), F32), cache_pool[l]], axis=1)
        xs, ka, va, ki, ub = _group_forward(
            xs, pos_s, past, hist0, cache_a_k[l], cache_a_v[l], cache_idx_k[l],
            cache_mem_k[l].astype(BF16).transpose(0, 2, 3, 1), cache_mem_v[l].astype(BF16).transpose(0, 2, 1, 3),
            wts, causal=False)
        pool_s = jnp.concatenate([cache_pool[l], ub], axis=1)[:, -POOL_HIST:]
        for lst, val in zip(outs[6:], (ka, va, ki, pool_s)):
            lst.append(val)

    stacked = [jnp.stack(o) for o in outs]
    return (xp, xs, *stacked)
```

```python
import functools

import jax
import jax.numpy as jnp
import numpy as np
from jax import lax
from jax.experimental import pallas as pl
from jax.experimental.pallas import tpu as pltpu

F32 = jnp.float32
BF16 = jnp.bfloat16

CHUNK = 64
EPS = 1e-6
ROPE_THETA = 10000.0
N_HEADS_A = 8
N_KV_A = 2
GROUP_A = N_HEADS_A // N_KV_A
HD_A = 64
D_ATT_A = N_HEADS_A * HD_A
D_KV_A = N_KV_A * HD_A
N_IDX_HEADS = 8
D_IDX = 32
D_QIDX = N_IDX_HEADS * D_IDX
IDX_SCALE = (N_IDX_HEADS * D_IDX) ** -0.5
TOPK_MAX = 256
POOL_WINDOWS = (2, 4, 8, 16)
POOL_GROUP = 128
D_POOL = len(POOL_WINDOWS) * POOL_GROUP
POOL_HIST = 15
N_HEADS_M = 4
HD_M = 128
D_MEM_ATT = N_HEADS_M * HD_M
N_BRANCH = 3

LANES = 128
SUBLANES = 8
V7X_VMEM_BYTES = 64 * 1024 * 1024
VMEM_LIMIT = 56 * 1024 * 1024

HIST_ROWS = 2 * SUBLANES
MASK_VALUE = -1e30
FLT_MAX = float(np.finfo(np.float32).max)
KEY_BITS = 32
Q_SCALE = HD_A ** -0.5 * float(np.log2(np.e))
SUM_FLOOR = 1e-30
COUNT_ROWS = 4 * SUBLANES
COUNT_CHAINS = 2
V_ROWS = HD_A + 2 * SUBLANES

C_QA = 0
C_KA = C_QA + D_ATT_A
C_VA = C_KA + D_KV_A
C_QI = C_VA + D_KV_A
C_KW = C_QI + D_QIDX
C_UB = C_KW + LANES
C_QM = C_UB + D_POOL
C_END = C_QM + D_MEM_ATT


def _cparams(sem):
    return pltpu.CompilerParams(dimension_semantics=sem, vmem_limit_bytes=VMEM_LIMIT)


def _rms(x, g):
    ms = jnp.mean(x * x, axis=-1, keepdims=True)
    return x * lax.rsqrt(ms + EPS) * g


def _group_sumsq(x, bd):
    sq = x * x
    hi = sq.astype(BF16)
    lo = (sq - hi.astype(F32)).astype(BF16)
    return (jnp.dot(hi, bd, preferred_element_type=F32)
            + jnp.dot(lo, bd, preferred_element_type=F32))


def _rope_lanes(x, cos, sin, half):
    lane = lax.broadcasted_iota(jnp.int32, x.shape, 1)
    first = (lane % (2 * half)) < half
    left = pltpu.roll(x, LANES - half, 1)
    right = pltpu.roll(x, half, 1)
    rot = jnp.where(first, -left, right)
    return x * cos + rot * sin


def _proj_kernel(x_ref, gmix_ref, w_ref, cosa_ref, sina_ref, cosi_ref, sini_ref,
                 gqa_ref, gka_ref, gki_ref, gqm_ref, bd64_ref, bdki_ref, *out_refs, dsa_layout):
    if dsa_layout:
        ka_ref, va_ref, kw_ref, ub_ref, qm_ref, qat_ref, qit_ref, wit_ref, kaug_ref, vt_ref, kib_ref = out_refs
    else:
        qa_ref, ka_ref, va_ref, qi_ref, kw_ref, ub_ref, qm_ref = out_refs
    x = x_ref[...]
    tm = x.shape[0]
    h = _rms(x, gmix_ref[...]).astype(BF16)
    p = jnp.dot(h, w_ref[...], preferred_element_type=F32)
    cosa, sina = cosa_ref[...], sina_ref[...]
    cosi, sini = cosi_ref[...], sini_ref[...]
    bd64 = bd64_ref[...]

    qa = p[:, C_QA:C_QA + D_ATT_A]
    ss = _group_sumsq(qa, bd64)
    qa = qa * lax.rsqrt(ss * (1.0 / HD_A) + EPS) * gqa_ref[...]
    for c in range(D_ATT_A // LANES):
        blk = _rope_lanes(qa[:, c * LANES:(c + 1) * LANES], cosa, sina, HD_A // 2) * Q_SCALE
        if dsa_layout:
            qat_ref[c * LANES:(c + 1) * LANES, :] = blk.T.astype(BF16)
        else:
            qa_ref[:, c * LANES:(c + 1) * LANES] = blk.astype(BF16)

    ka = p[:, C_KA:C_KA + D_KV_A]
    ss = _group_sumsq(ka, bd64[:D_KV_A, :D_KV_A])
    ka = ka * lax.rsqrt(ss * (1.0 / HD_A) + EPS) * gka_ref[...]
    ka = _rope_lanes(ka, cosa, sina, HD_A // 2)
    va = p[:, C_VA:C_VA + D_KV_A]
    ka_ref[...] = ka
    va_ref[...] = va
    if dsa_layout:
        lane = lax.broadcasted_iota(jnp.int32, ka.shape, 1)
        tail = jnp.where(lane == HD_A, 1.0, 0.0)
        kaug_ref[0] = jnp.where(lane < HD_A, ka, tail).astype(BF16)
        kaug_ref[1] = jnp.where(lane < HD_A, pltpu.roll(ka, HD_A, 1), tail).astype(BF16)
        vat = va.T
        srow = lax.broadcasted_iota(jnp.int32, (V_ROWS - HD_A, tm), 0)
        ones_rows = jnp.where(srow == 0, 1.0, 0.0).astype(BF16)
        for g in range(N_KV_A):
            vt_ref[g, 0:HD_A, :] = vat[g * HD_A:(g + 1) * HD_A, :].astype(BF16)
            vt_ref[g, HD_A:V_ROWS, :] = ones_rows

    for c in range(D_QIDX // LANES):
        blk = _rope_lanes(p[:, C_QI + c * LANES:C_QI + (c + 1) * LANES], cosi, sini, D_IDX // 2)
        if dsa_layout:
            qit_ref[c * LANES:(c + 1) * LANES, :] = blk.T.astype(BF16)
        else:
            qi_ref[:, c * LANES:(c + 1) * LANES] = blk.astype(BF16)

    kw = p[:, C_KW:C_KW + LANES]
    ss = _group_sumsq(kw, bdki_ref[...])
    kin = kw * lax.rsqrt(ss * (1.0 / D_IDX) + EPS) * gki_ref[...]
    kin = _rope_lanes(kin, cosi, sini, D_IDX // 2)
    lane = lax.broadcasted_iota(jnp.int32, kw.shape, 1)
    kw = jnp.where(lane < D_IDX, kin, kw * IDX_SCALE)
    kw_ref[...] = kw
    if dsa_layout:
        kib_ref[...] = kw[:, :D_IDX].astype(BF16)
        wit_ref[...] = kw.T[D_IDX:D_IDX + N_IDX_HEADS, :]

    ub_ref[...] = p[:, C_UB:C_UB + D_POOL]

    gqm = gqm_ref[...]
    for hd in range(N_HEADS_M):
        blk = p[:, C_QM + hd * HD_M:C_QM + (hd + 1) * HD_M]
        qm_ref[:, hd * HD_M:(hd + 1) * HD_M] = _rms(blk, gqm).astype(BF16)


def _proj_call(x2d, gmix, w_cat, tabs, gqa_t, gka_t, gki_t, gqm, bd64, bdki, tm, n_tab_blocks, dsa_layout):
    n = x2d.shape[0]
    d = x2d.shape[1]
    nb = n // (tm * n_tab_blocks)
    grid = (n // tm,)
    row = lambda i: (i, 0)
    const = lambda i: (0, 0)
    tab = lambda i: (i % n_tab_blocks, 0)
    in_specs = [
        pl.BlockSpec((tm, d), row),
        pl.BlockSpec((1, d), const),
        pl.BlockSpec((d, C_END), const),
        pl.BlockSpec((tm, LANES), tab), pl.BlockSpec((tm, LANES), tab),
        pl.BlockSpec((tm, LANES), tab), pl.BlockSpec((tm, LANES), tab),
        pl.BlockSpec((1, D_ATT_A), const), pl.BlockSpec((1, D_KV_A), const),
        pl.BlockSpec((1, LANES), const), pl.BlockSpec((1, HD_M), const),
        pl.BlockSpec((D_ATT_A, D_ATT_A), const), pl.BlockSpec((LANES, LANES), const),
    ]
    rows = lambda width, dtype: (jax.ShapeDtypeStruct((n, width), dtype), pl.BlockSpec((tm, width), row))
    per_token = [rows(D_KV_A, F32), rows(D_KV_A, F32), rows(LANES, F32), rows(D_POOL, F32),
                 rows(D_MEM_ATT, BF16)]
    if dsa_layout:
        t = tm * n_tab_blocks
        cols = lambda i: (i // n_tab_blocks, 0, i % n_tab_blocks)
        tposed = lambda r, dtype: (jax.ShapeDtypeStruct((nb, r, t), dtype), pl.BlockSpec((None, r, tm), cols))
        outs = per_token + [
            tposed(D_ATT_A, BF16), tposed(D_QIDX, BF16), tposed(N_IDX_HEADS, F32),
            (jax.ShapeDtypeStruct((N_KV_A, n, LANES), BF16), pl.BlockSpec((N_KV_A, tm, LANES), lambda i: (0, i, 0))),
            (jax.ShapeDtypeStruct((nb, N_KV_A, n_tab_blocks, V_ROWS, tm), BF16),
             pl.BlockSpec((None, N_KV_A, None, V_ROWS, tm),
                          lambda i: (i // n_tab_blocks, 0, i % n_tab_blocks, 0, 0))),
            rows(D_IDX, BF16),
        ]
    else:
        outs = [rows(D_ATT_A, BF16)] + per_token[:2] + [rows(D_QIDX, BF16)] + per_token[2:]
    return pl.pallas_call(
        functools.partial(_proj_kernel, dsa_layout=dsa_layout), grid=grid, in_specs=in_specs,
        out_specs=[o[1] for o in outs], out_shape=[o[0] for o in outs],
        compiler_params=_cparams(("parallel",)), name="proj",
    )(x2d, gmix, w_cat, *tabs, gqa_t, gka_t, gki_t, gqm, bd64, bdki)


def _key_bits_to_float(u):
    k = u ^ jnp.int32(-2147483648)
    bits = jnp.where(k >= 0, k, k ^ jnp.int32(0x7FFFFFFF))
    return lax.bitcast_convert_type(bits, F32)


def _dsa_kernel(qat_ref, qit_ref, wit_ref, k_ref, vt_ref, ki_ref, o_ref,
                sc_ref, m_ref, acc_ref, qaug_ref, qis_ref, km_ref,
                *, tq, kb, nkb_total, causal, s_valid, topk, rows_out):
    qblk = pl.program_id(1)
    if causal:
        nkb = lax.div(qblk + kb // tq, jnp.int32(kb // tq))
    else:
        nkb = nkb_total
    qpos = qblk * tq + lax.broadcasted_iota(jnp.int32, (1, tq), 1)
    if causal:
        lim = (lax.shift_right_logical(qpos, 6) + 1) * CHUNK
    else:
        lim = jnp.full((1, tq), s_valid, jnp.int32)
    limf = lim.astype(F32)
    krow = lax.broadcasted_iota(jnp.int32, (kb, tq), 0)
    kf = float(topk)

    for h in range(N_IDX_HEADS):
        qis_ref[:, h * tq:(h + 1) * tq] = qit_ref[h * D_IDX:(h + 1) * D_IDX, :]
    qit = qis_ref[...]
    wit = wit_ref[...]

    def score_body(j, carry):
        z = jnp.dot(ki_ref[j], qit, preferred_element_type=F32)
        s = jnp.zeros((kb, tq), F32)
        for h in range(N_IDX_HEADS):
            s = s + wit[h:h + 1, :] * jnp.maximum(z[:, h * tq:(h + 1) * tq], 0.0)
        sc_ref[j] = jnp.where(j * kb + krow < lim, s, -jnp.inf)
        return carry

    lax.fori_loop(0, nkb, score_body, 0)

    crow = lax.broadcasted_iota(jnp.int32, (COUNT_ROWS, tq), 0)

    def count(pred):
        def body(j, accs):
            accs = list(accs)
            for ci, r in enumerate(range(0, kb, COUNT_ROWS)):
                hit = pred(sc_ref[j, r:r + COUNT_ROWS, :], j * kb + r + crow)
                accs[ci % COUNT_CHAINS] = accs[ci % COUNT_CHAINS] + jnp.where(hit, 1.0, 0.0)
            return tuple(accs)
        zero = jnp.zeros((COUNT_ROWS, tq), F32)
        accs = lax.fori_loop(0, nkb, body, (zero,) * COUNT_CHAINS)
        return jnp.sum(sum(accs[1:], accs[0]), axis=0, keepdims=True)

    def radix_body(i, carry):
        prefix, cge = carry
        cand = prefix | lax.shift_left(jnp.int32(1), KEY_BITS - 1 - i)
        t = _key_bits_to_float(cand)
        c = count(lambda s, idx: s >= t)
        take = c >= kf
        return jnp.where(take, cand, prefix), jnp.where(take, c, cge)

    prefix, cge = lax.fori_loop(0, KEY_BITS, radix_body, (jnp.zeros((1, tq), jnp.int32), limf))
    thr = jnp.where(limf >= kf, _key_bits_to_float(prefix), -FLT_MAX)

    any_tie = jnp.max(jnp.where(cge > kf, 1.0, 0.0), axis=1, keepdims=True)

    def mask_plain():
        def body(j, carry):
            sc_ref[j] = jnp.where(sc_ref[j] >= thr, 0.0, MASK_VALUE)
            return carry
        lax.fori_loop(0, nkb, body, 0)

    def mask_ties():
        cgt = count(lambda s, idx: s > thr)
        need = kf - cgt
        nbits = int(nkb_total * kb - 1).bit_length()

        def idx_body(i, j0):
            cand = j0 | lax.shift_left(jnp.int32(1), nbits - 1 - i)
            c = count(lambda s, idx: (s == thr) & (idx < cand))
            return jnp.where(c < need, cand, j0)

        j0 = lax.fori_loop(0, nbits, idx_body, jnp.zeros((1, tq), jnp.int32))

        def body(j, carry):
            s = sc_ref[j]
            keep = (s > thr) | ((s == thr) & (j * kb + krow <= j0))
            sc_ref[j] = jnp.where(keep, 0.0, MASK_VALUE)
            return carry
        lax.fori_loop(0, nkb, body, 0)

    lax.cond(any_tie[0, 0] > 0.0, mask_ties, mask_plain)

    @pl.when(qblk == 0)
    def _():
        for g in range(N_KV_A):
            def norm_body(j, mx, g=g):
                kk = k_ref[g, j].astype(F32)
                return jnp.maximum(mx, jnp.sum(kk * kk, axis=1, keepdims=True))
            mx = lax.fori_loop(0, nkb_total, norm_body, jnp.zeros((kb, 1), F32))
            km_ref[g] = jnp.broadcast_to(jnp.max(mx, axis=0, keepdims=True), (SUBLANES, LANES))

    aug_row = lax.broadcasted_iota(jnp.int32, (LANES - HD_A, tq), 0)
    for g in range(N_KV_A):
        kmax2 = km_ref[g][0:1, :]
        kmax2 = jnp.concatenate([kmax2] * (tq // LANES), axis=1) if tq >= LANES else kmax2[:, :tq]
        for u in range(GROUP_A):
            hd = g * GROUP_A + u
            q = qat_ref[hd * HD_A:(hd + 1) * HD_A, :]
            qf = q.astype(F32)
            shift = jnp.sqrt(jnp.sum(qf * qf, axis=0, keepdims=True) * kmax2)
            qaug_ref[g, 0:HD_A, u * tq:(u + 1) * tq] = q
            qaug_ref[g, HD_A:LANES, u * tq:(u + 1) * tq] = jnp.where(aug_row == 0, -shift, 0.0).astype(BF16)

    acc_ref[...] = jnp.zeros(acc_ref.shape, F32)

    def attn_body(j, carry):
        bias = sc_ref[j]
        for g in range(N_KV_A):
            lg = jnp.dot(k_ref[g, j], qaug_ref[g], preferred_element_type=F32)
            vt = vt_ref[g, j]
            for u in range(GROUP_A):
                hd = g * GROUP_A + u
                p = jnp.exp2(lg[:, u * tq:(u + 1) * tq] + bias).astype(BF16)
                acc_ref[hd] += jnp.dot(vt, p, preferred_element_type=F32)
        return carry

    lax.fori_loop(0, nkb, attn_body, 0)

    sums_ok = jnp.ones((1, tq), F32)
    for hd in range(N_HEADS_A):
        sums_ok = jnp.where(acc_ref[hd][HD_A:HD_A + 1, :] > SUM_FLOOR, sums_ok, 0.0)
    all_ok = jnp.min(sums_ok, axis=1, keepdims=True)

    @pl.when(all_ok[0, 0] < 1.0)
    def _():
        m_ref[...] = jnp.full(m_ref.shape, MASK_VALUE, F32)
        acc_ref[...] = jnp.zeros(acc_ref.shape, F32)

        def online_body(j, carry):
            bias = sc_ref[j]
            for g in range(N_KV_A):
                lg = jnp.dot(k_ref[g, j], qaug_ref[g], preferred_element_type=F32)
                vt = vt_ref[g, j]
                for u in range(GROUP_A):
                    hd = g * GROUP_A + u
                    l = lg[:, u * tq:(u + 1) * tq] + bias
                    m_old = m_ref[hd:hd + 1, :]
                    m_new = jnp.maximum(m_old, jnp.max(l, axis=0, keepdims=True))
                    p = jnp.exp2(l - m_new).astype(BF16)
                    pv = jnp.dot(vt, p, preferred_element_type=F32)
                    acc_ref[hd] = jnp.exp2(m_old - m_new) * acc_ref[hd] + pv
                    m_ref[hd:hd + 1, :] = m_new
            return carry

        lax.fori_loop(0, nkb, online_body, 0)

    heads = []
    for hd in range(N_HEADS_A):
        a = acc_ref[hd]
        heads.append(a[:HD_A, :] / a[HD_A:HD_A + 1, :])
    out_t = jnp.concatenate(heads, axis=0)
    o_ref[...] = (out_t.T if rows_out else out_t).astype(BF16)


def _dsa_call(qat, qit, wit, k, vt, ki, *, tq, kb, causal, s_valid, topk):
    b, t = qat.shape[0], qat.shape[2]
    nkb_total = k.shape[2]
    rows_out = tq % LANES == 0
    kern = functools.partial(_dsa_kernel, tq=tq, kb=kb, nkb_total=nkb_total, causal=causal,
                             s_valid=s_valid, topk=topk, rows_out=rows_out)
    qcols = lambda i, j: (i, 0, j)
    in_specs = [
        pl.BlockSpec((None, D_ATT_A, tq), qcols),
        pl.BlockSpec((None, D_QIDX, tq), qcols),
        pl.BlockSpec((None, N_IDX_HEADS, tq), qcols),
        pl.BlockSpec((N_KV_A, None, nkb_total, kb, LANES), lambda i, j: (0, i, 0, 0, 0)),
        pl.BlockSpec((None, N_KV_A, nkb_total, V_ROWS, kb), lambda i, j: (i, 0, 0, 0, 0)),
        pl.BlockSpec((None, nkb_total, kb, D_IDX), lambda i, j: (i, 0, 0, 0)),
    ]
    if rows_out:
        out_spec = pl.BlockSpec((None, tq, D_ATT_A), lambda i, j: (i, j, 0))
        out_shape = jax.ShapeDtypeStruct((b, t, D_ATT_A), BF16)
    else:
        out_spec = pl.BlockSpec((None, D_ATT_A, tq), qcols)
        out_shape = jax.ShapeDtypeStruct((b, D_ATT_A, t), BF16)
    return pl.pallas_call(
        kern, grid=(b, t // tq), in_specs=in_specs, out_specs=out_spec, out_shape=out_shape,
        scratch_shapes=[pltpu.VMEM((nkb_total, kb, tq), F32),
                        pltpu.VMEM((N_HEADS_A, tq), F32),
                        pltpu.VMEM((N_HEADS_A, V_ROWS, tq), F32),
                        pltpu.VMEM((N_KV_A, LANES, GROUP_A * tq), BF16),
                        pltpu.VMEM((D_IDX, N_IDX_HEADS * tq), BF16),
                        pltpu.VMEM((N_KV_A, SUBLANES, LANES), F32)],
        compiler_params=_cparams(("parallel", "arbitrary")), name="dsa",
    )(qat, qit, wit, k, vt, ki)


def _memkv_kernel(mem_ref, gmem_ref, w_ref, gkm_ref, k_ref, v_ref):
    h = _rms(mem_ref[...], gmem_ref[...]).astype(BF16)
    kv = jnp.dot(h, w_ref[...], preferred_element_type=F32)
    gkm = gkm_ref[...]
    for hd in range(N_HEADS_M):
        k_ref[:, hd * HD_M:(hd + 1) * HD_M] = _rms(kv[:, hd * HD_M:(hd + 1) * HD_M], gkm)
    v_ref[...] = kv[:, D_MEM_ATT:]


def _memkv_call(mem2d, gmem, w_kv, gkm, tm):
    n, d = mem2d.shape
    row = lambda i: (i, 0)
    const = lambda i: (0, 0)
    return pl.pallas_call(
        _memkv_kernel, grid=(n // tm,),
        in_specs=[pl.BlockSpec((tm, d), row), pl.BlockSpec((1, d), const),
                  pl.BlockSpec((d, 2 * D_MEM_ATT), const), pl.BlockSpec((1, HD_M), const)],
        out_specs=[pl.BlockSpec((tm, D_MEM_ATT), row), pl.BlockSpec((tm, D_MEM_ATT), row)],
        out_shape=[jax.ShapeDtypeStruct((n, D_MEM_ATT), F32), jax.ShapeDtypeStruct((n, D_MEM_ATT), F32)],
        compiler_params=_cparams(("parallel",)), name="memkv",
    )(mem2d, gmem, w_kv, gkm)


def _merge_kernel(x_ref, a_ref, ub_ref, prev_ref, hist0_ref, qm_ref, mkt_ref, mv_ref,
                  gmix_ref, wg_ref, woa_ref, wob_ref, wom_ref, wout_ref, wpool_ref, spool_ref,
                  o_ref, ext_ref, *, tm, pos0):
    it = pl.program_id(1)
    x = x_ref[...]

    h = _rms(x, gmix_ref[...]).astype(BF16)
    gates = jax.nn.sigmoid(jnp.dot(h, wg_ref[...], preferred_element_type=F32))
    d = x.shape[1]

    ub = ub_ref[...]
    ext_ref[0:HIST_ROWS, :] = jnp.where(it == 0, hist0_ref[...], prev_ref[...])
    ext_ref[HIST_ROWS:HIST_ROWS + tm, :] = ub
    pos = pos0 + it * tm + lax.broadcasted_iota(jnp.int32, (tm, 1), 0)
    ys = []
    for g, w in enumerate(POOL_WINDOWS):
        c0 = g * POOL_GROUP
        win = ub[:, c0:c0 + POOL_GROUP]
        for k in range(1, w):
            win = win + ext_ref[HIST_ROWS - k:HIST_ROWS - k + tm, c0:c0 + POOL_GROUP]
        cnt = jnp.minimum(w, pos + 1).astype(F32)
        pg = (win / cnt - ub[:, c0:c0 + POOL_GROUP]).astype(BF16)
        ys.append(jnp.dot(pg, wpool_ref[g], preferred_element_type=F32))
    bmix = (jnp.concatenate(ys, axis=1) * spool_ref[...]).astype(BF16)

    qm = qm_ref[...]
    ms = []
    for hd in range(N_HEADS_M):
        lg = jnp.dot(qm[:, hd * HD_M:(hd + 1) * HD_M], mkt_ref[hd], preferred_element_type=F32)
        lg = lg * (HD_M ** -0.5)
        e = jnp.exp(lg - jnp.max(lg, axis=1, keepdims=True))
        pr = (e / jnp.sum(e, axis=1, keepdims=True)).astype(BF16)
        ms.append(jnp.dot(pr, mv_ref[hd], preferred_element_type=F32))
    mmix = jnp.concatenate(ms, axis=1).astype(BF16)

    mixed = (gates[:, 0:d] * jnp.dot(a_ref[...], woa_ref[...], preferred_element_type=F32)
             + gates[:, d:2 * d] * jnp.dot(bmix, wob_ref[...], preferred_element_type=F32)
             + gates[:, 2 * d:3 * d] * jnp.dot(mmix, wom_ref[...], preferred_element_type=F32))
    o_ref[...] = x + jnp.dot(mixed.astype(BF16), wout_ref[...], preferred_element_type=F32)


def _merge_call(x, a, ub, hist0, qm, mkt, mv, gmix, wg, woa, wob, wom, wout, wpool, spool, *, tm, pos0):
    b, t, d = x.shape
    nt = t // tm
    hb = tm // HIST_ROWS
    tok = lambda i, j: (i, j, 0)
    c2 = lambda i, j: (0, 0)
    c3 = lambda i, j: (0, 0, 0)
    per_b3 = lambda i, j: (i, 0, 0)
    per_b4 = lambda i, j: (i, 0, 0, 0)
    in_specs = [
        pl.BlockSpec((None, tm, d), tok),
        pl.BlockSpec((None, tm, D_ATT_A), tok),
        pl.BlockSpec((None, tm, D_POOL), tok),
        pl.BlockSpec((None, HIST_ROWS, D_POOL), lambda i, j: (i, jnp.maximum(j * hb - 1, 0), 0)),
        pl.BlockSpec((None, HIST_ROWS, D_POOL), per_b3),
        pl.BlockSpec((None, tm, D_MEM_ATT), tok),
        pl.BlockSpec((None, N_HEADS_M, HD_M, mkt.shape[3]), per_b4),
        pl.BlockSpec((None, N_HEADS_M, mv.shape[2], HD_M), per_b4),
        pl.BlockSpec((1, d), c2),
        pl.BlockSpec(wg.shape, c2),
        pl.BlockSpec(woa.shape, c2), pl.BlockSpec(wob.shape, c2), pl.BlockSpec(wom.shape, c2),
        pl.BlockSpec(wout.shape, c2),
        pl.BlockSpec(wpool.shape, c3),
        pl.BlockSpec((1, D_POOL), c2),
    ]
    return pl.pallas_call(
        functools.partial(_merge_kernel, tm=tm, pos0=pos0), grid=(b, nt), in_specs=in_specs,
        out_specs=pl.BlockSpec((None, tm, d), tok),
        out_shape=jax.ShapeDtypeStruct((b, t, d), F32),
        scratch_shapes=[pltpu.VMEM((HIST_ROWS + tm, D_POOL), F32)],
        compiler_params=_cparams(("parallel", "arbitrary")), name="merge",
    )(x, a, ub, ub, hist0, qm, mkt, mv, gmix, wg, woa, wob, wom, wout, wpool, spool)


def _ffn_kernel(x_ref, g_ref, wgate_ref, wup_ref, wdown_ref, o_ref):
    x = x_ref[...]
    h = _rms(x, g_ref[...]).astype(BF16)
    gate = jnp.dot(h, wgate_ref[...], preferred_element_type=F32)
    up = jnp.dot(h, wup_ref[...], preferred_element_type=F32)
    act = (jax.nn.silu(gate) * up).astype(BF16)
    o_ref[...] = x + jnp.dot(act, wdown_ref[...], preferred_element_type=F32)


def _ffn_call(x2d, g, wgate, wup, wdown, tm):
    n, d = x2d.shape
    row = lambda i: (i, 0)
    const = lambda i: (0, 0)
    return pl.pallas_call(
        _ffn_kernel, grid=(n // tm,),
        in_specs=[pl.BlockSpec((tm, d), row), pl.BlockSpec((1, d), const),
                  pl.BlockSpec(wgate.shape, const), pl.BlockSpec(wup.shape, const),
                  pl.BlockSpec(wdown.shape, const)],
        out_specs=pl.BlockSpec((tm, d), row),
        out_shape=jax.ShapeDtypeStruct((n, d), F32),
        compiler_params=_cparams(("parallel",)), name="ffn",
    )(x2d, g, wgate, wup, wdown)


def _rope_tables(pos, head_dim):
    half = head_dim // 2
    inv = ROPE_THETA ** (-jnp.arange(half, dtype=F32) / half)
    ang = pos.astype(F32)[:, None] * inv[None, :]
    reps = LANES // half
    return jnp.tile(jnp.cos(ang), (1, reps)), jnp.tile(jnp.sin(ang), (1, reps))


def _pack_weights(w_in, g_qa, g_ka, g_kidx, g_qm):
    d = w_in.shape[0]
    widths = (D_ATT_A, D_KV_A, D_KV_A, D_QIDX, D_IDX, N_IDX_HEADS, D_POOL, D_MEM_ATT, N_BRANCH * d)
    cuts = [int(c) for c in np.cumsum(widths)[:-1]]
    wqa, wka, wva, wqi, wki, wwi, wub, wqm, wgates = jnp.split(w_in, cuts, axis=1)
    pad = jnp.zeros((d, LANES - D_IDX - N_IDX_HEADS), w_in.dtype)
    w_cat = jnp.concatenate([wqa, wka, wva, wqi, wki, wwi, pad, wub, wqm], axis=1).astype(BF16)
    gqa_t = jnp.tile(g_qa, N_HEADS_A)[None, :]
    gka_t = jnp.tile(g_ka, N_KV_A)[None, :]
    gki_t = jnp.concatenate([g_kidx, jnp.ones((LANES - D_IDX,), g_kidx.dtype)])[None, :]
    return w_cat, wgates.astype(BF16), gqa_t, gka_t, gki_t, g_qm[None, :]


def _block_diag_ones(n, group, limit=None):
    i = np.arange(n)
    m = (i[:, None] // group) == (i[None, :] // group)
    if limit is not None:
        m = m & (i[:, None] < limit) & (i[None, :] < limit)
    return jnp.asarray(m, BF16)


def _key_blocks(k_all, v_all, ki_all, kb):
    b, s = k_all.shape[0], k_all.shape[1]
    nkb = -(-s // kb)
    pad = nkb * kb - s
    if pad:
        k_all = jnp.pad(k_all, ((0, 0), (0, pad), (0, 0), (0, 0)))
        v_all = jnp.pad(v_all, ((0, 0), (0, pad), (0, 0), (0, 0)))
        ki_all = jnp.pad(ki_all, ((0, 0), (0, pad), (0, 0)))
    ones = jnp.ones(v_all.shape[:-1] + (1,), BF16)
    zeros = jnp.zeros(v_all.shape[:-1] + (LANES - HD_A - 1,), BF16)
    kaug = jnp.concatenate([k_all.astype(BF16), ones, zeros], axis=-1)
    k = kaug.reshape(b, nkb, kb, N_KV_A, LANES).transpose(3, 0, 1, 2, 4)
    vaug = jnp.concatenate([v_all.astype(BF16), ones, zeros[..., :V_ROWS - HD_A - 1]], axis=-1)
    vt = vaug.reshape(b, nkb, kb, N_KV_A, V_ROWS).transpose(0, 3, 1, 4, 2)
    ki = ki_all.astype(BF16).reshape(b, nkb, kb, D_IDX)
    return k, vt, ki


def _tile(n, pref):
    t = pref
    while n % t:
        t //= 2
    return t


def _group_forward(x, pos, pos0, hist0, k_hist, v_hist, ki_hist, mkt, mv, wts, *, causal):
    (gmix, w_cat, wgates, gqa_t, gka_t, gki_t, gqm, bd64, bdki, wpool, spool,
     woa, wob, wom, wout, gffn, wgate, wup, wdown) = wts
    b, t, d = x.shape
    n = b * t
    tm = _tile(t, 512)
    cosa, sina = _rope_tables(pos, HD_A)
    cosi, sini = _rope_tables(pos, D_IDX)
    tq = _tile(t, 256)
    proj_args = (x.reshape(n, d), gmix, w_cat, (cosa, sina, cosi, sini), gqa_t, gka_t, gki_t, gqm, bd64, bdki,
                 tm, t // tm)
    if causal:
        ka, va, kw, ub, qm, qat, qit, wit, kaug, vt, kib = _proj_call(*proj_args, dsa_layout=True)
        kb, s = tm, t
        kblk = kaug.reshape(N_KV_A, b, t // kb, kb, LANES)
        kiblk = kib.reshape(b, t // kb, kb, D_IDX)
    else:
        qa, ka, va, qi, kw, ub, qm = _proj_call(*proj_args, dsa_layout=False)
        qat = qa.reshape(b, t, D_ATT_A).transpose(0, 2, 1)
        qit = qi.reshape(b, t, D_QIDX).transpose(0, 2, 1)
        wit = kw[:, D_IDX:D_IDX + N_IDX_HEADS].reshape(b, t, N_IDX_HEADS).transpose(0, 2, 1)
    ka = ka.reshape(b, t, N_KV_A, HD_A)
    va = va.reshape(b, t, N_KV_A, HD_A)
    ki = kw[:, :D_IDX].reshape(b, t, D_IDX)
    ub = ub.reshape(b, t, D_POOL)
    if not causal:
        k_all = jnp.concatenate([k_hist, ka], axis=1)
        s, kb = k_all.shape[1], 256
        kblk, vt, kiblk = _key_blocks(k_all, jnp.concatenate([v_hist, va], axis=1),
                                      jnp.concatenate([ki_hist, ki], axis=1), kb)
    topk = min(TOPK_MAX, s // 4)
    a = _dsa_call(qat, qit, wit, kblk, vt, kiblk, tq=tq, kb=kb, causal=causal, s_valid=s, topk=topk)
    if tq % LANES:
        a = a.transpose(0, 2, 1)

    tmm = _tile(t, 256)
    x2 = _merge_call(x, a, ub, hist0, qm.reshape(b, t, D_MEM_ATT), mkt, mv, gmix, wgates,
                     woa, wob, wom, wout, wpool, spool, tm=tmm, pos0=pos0)
    y = _ffn_call(x2.reshape(n, d), gffn, wgate, wup, wdown, _tile(n, 256)).reshape(b, t, d)
    return y, ka, va, ki, ub


def kernel(x_prompt, x_sample, mem_prompt, cache_a_k, cache_a_v, cache_idx_k, cache_pool, cache_mem_k,
           cache_mem_v, g_mix, w_in, g_qa, g_ka, g_kidx, g_qm, g_mem, w_mem_kv, g_km, w_pool, s_pool,
           w_oa, w_ob, w_om, w_out, g_ffn, w_gate, w_up, w_down):
    depth = w_in.shape[0]
    t = x_prompt.shape[1]
    ts = x_sample.shape[1]
    past = cache_a_k.shape[2]
    b = x_prompt.shape[0]
    n_mem = mem_prompt.shape[1]
    pos_p = jnp.arange(t, dtype=jnp.int32)
    pos_s = past + jnp.arange(ts, dtype=jnp.int32)
    bd64 = _block_diag_ones(D_ATT_A, HD_A)
    bdki = _block_diag_ones(LANES, LANES, limit=D_IDX)

    xp, xs = x_prompt, x_sample
    outs = [[] for _ in range(10)]
    for l in range(depth):
        w_cat, wgates, gqa_t, gka_t, gki_t, gqm = _pack_weights(w_in[l], g_qa[l], g_ka[l], g_kidx[l], g_qm[l])
        wts = (g_mix[l][None, :], w_cat, wgates, gqa_t, gka_t, gki_t, gqm, bd64, bdki,
               w_pool[l].astype(BF16), s_pool[l][None, :],
               w_oa[l].astype(BF16), w_ob[l].astype(BF16), w_om[l].astype(BF16), w_out[l].astype(BF16),
               g_ffn[l][None, :], w_gate[l].astype(BF16), w_up[l].astype(BF16), w_down[l].astype(BF16))

        mk, mv = _memkv_call(mem_prompt.reshape(b * n_mem, -1), g_mem[l][None, :],
                             w_mem_kv[l].astype(BF16), g_km[l][None, :], _tile(b * n_mem, 256))
        mk = mk.reshape(b, n_mem, N_HEADS_M, HD_M)
        mv = mv.reshape(b, n_mem, N_HEADS_M, HD_M)
        hist0 = jnp.zeros((b, HIST_ROWS, D_POOL), F32)
        xp, ka, va, ki, ub = _group_forward(
            xp, pos_p, 0, hist0, None, None, None,
            mk.astype(BF16).transpose(0, 2, 3, 1), mv.astype(BF16).transpose(0, 2, 1, 3), wts, causal=True)
        for lst, val in zip(outs[:6], (ka, va, ki, ub[:, -POOL_HIST:], mk, mv)):
            lst.append(val)

        bs = xs.shape[0]
        hist0 = jnp.concatenate([jnp.zeros((bs, HIST_ROWS - POOL_HIST, D_POOL), F32), cache_pool[l]], axis=1)
        xs, ka, va, ki, ub = _group_forward(
            xs, pos_s, past, hist0, cache_a_k[l], cache_a_v[l], cache_idx_k[l],
            cache_mem_k[l].astype(BF16).transpose(0, 2, 3, 1), cache_mem_v[l].astype(BF16).transpose(0, 2, 1, 3),
            wts, causal=False)
        pool_s = jnp.concatenate([cache_pool[l], ub], axis=1)[:, -POOL_HIST:]
        for lst, val in zip(outs[6:], (ka, va, ki, pool_s)):
            lst.append(val)

    stacked = [jnp.stack(o) for o in outs]
    return (xp, xs, *stacked)
```

```python
import functools

import jax
import jax.numpy as jnp
import numpy as np
from jax import lax
from jax.experimental import pallas as pl
from jax.experimental.pallas import tpu as pltpu

F32 = jnp.float32
BF16 = jnp.bfloat16

CHUNK = 64
EPS = 1e-6
ROPE_THETA = 10000.0
N_HEADS_A = 8
N_KV_A = 2
GROUP_A = N_HEADS_A // N_KV_A
HD_A = 64
D_ATT_A = N_HEADS_A * HD_A
D_KV_A = N_KV_A * HD_A
N_IDX_HEADS = 8
D_IDX = 32
D_QIDX = N_IDX_HEADS * D_IDX
IDX_SCALE = (N_IDX_HEADS * D_IDX) ** -0.5
TOPK_MAX = 256
POOL_WINDOWS = (2, 4, 8, 16)
POOL_GROUP = 128
D_POOL = len(POOL_WINDOWS) * POOL_GROUP
POOL_HIST = 15
N_HEADS_M = 4
HD_M = 128
D_MEM_ATT = N_HEADS_M * HD_M
N_BRANCH = 3

LANES = 128
SUBLANES = 8
V7X_VMEM_BYTES = 64 * 1024 * 1024
VMEM_LIMIT = 56 * 1024 * 1024

HIST_ROWS = 2 * SUBLANES
MASK_VALUE = -1e30
FLT_MAX = float(np.finfo(np.float32).max)
KEY_BITS = 32
Q_SCALE = HD_A ** -0.5 * float(np.log2(np.e))
SUM_FLOOR = 1e-30
COUNT_ROWS = 4 * SUBLANES
COUNT_CHAINS = 2
FUSED_PASSES = 8
V_ROWS = HD_A + 2 * SUBLANES

C_QA = 0
C_KA = C_QA + D_ATT_A
C_VA = C_KA + D_KV_A
C_QI = C_VA + D_KV_A
C_KW = C_QI + D_QIDX
C_UB = C_KW + LANES
C_QM = C_UB + D_POOL
C_END = C_QM + D_MEM_ATT


def _cparams(sem):
    return pltpu.CompilerParams(dimension_semantics=sem, vmem_limit_bytes=VMEM_LIMIT)


def _rms(x, g):
    ms = jnp.mean(x * x, axis=-1, keepdims=True)
    return x * lax.rsqrt(ms + EPS) * g


def _group_sumsq(x, bd):
    sq = x * x
    hi = sq.astype(BF16)
    lo = (sq - hi.astype(F32)).astype(BF16)
    return (jnp.dot(hi, bd, preferred_element_type=F32)
            + jnp.dot(lo, bd, preferred_element_type=F32))


def _rope_lanes(x, cos, sin, half):
    lane = lax.broadcasted_iota(jnp.int32, x.shape, 1)
    first = (lane % (2 * half)) < half
    left = pltpu.roll(x, LANES - half, 1)
    right = pltpu.roll(x, half, 1)
    rot = jnp.where(first, -left, right)
    return x * cos + rot * sin


def _proj_kernel(x_ref, gmix_ref, w_ref, cosa_ref, sina_ref, cosi_ref, sini_ref,
                 gqa_ref, gka_ref, gki_ref, gqm_ref, bd64_ref, bdki_ref, *out_refs, dsa_layout):
    if dsa_layout:
        ka_ref, va_ref, kw_ref, ub_ref, qm_ref, qat_ref, qit_ref, wit_ref, kaug_ref, vt_ref, kib_ref = out_refs
    else:
        qa_ref, ka_ref, va_ref, qi_ref, kw_ref, ub_ref, qm_ref = out_refs
    x = x_ref[...]
    tm = x.shape[0]
    h = _rms(x, gmix_ref[...]).astype(BF16)
    p = jnp.dot(h, w_ref[...], preferred_element_type=F32)
    cosa, sina = cosa_ref[...], sina_ref[...]
    cosi, sini = cosi_ref[...], sini_ref[...]
    bd64 = bd64_ref[...]

    qa = p[:, C_QA:C_QA + D_ATT_A]
    ss = _group_sumsq(qa, bd64)
    qa = qa * lax.rsqrt(ss * (1.0 / HD_A) + EPS) * gqa_ref[...]
    for c in range(D_ATT_A // LANES):
        blk = _rope_lanes(qa[:, c * LANES:(c + 1) * LANES], cosa, sina, HD_A // 2) * Q_SCALE
        if dsa_layout:
            qat_ref[c * LANES:(c + 1) * LANES, :] = blk.T.astype(BF16)
        else:
            qa_ref[:, c * LANES:(c + 1) * LANES] = blk.astype(BF16)

    ka = p[:, C_KA:C_KA + D_KV_A]
    ss = _group_sumsq(ka, bd64[:D_KV_A, :D_KV_A])
    ka = ka * lax.rsqrt(ss * (1.0 / HD_A) + EPS) * gka_ref[...]
    ka = _rope_lanes(ka, cosa, sina, HD_A // 2)
    va = p[:, C_VA:C_VA + D_KV_A]
    ka_ref[...] = ka
    va_ref[...] = va
    if dsa_layout:
        lane = lax.broadcasted_iota(jnp.int32, ka.shape, 1)
        tail = jnp.where(lane == HD_A, 1.0, 0.0)
        kaug_ref[0] = jnp.where(lane < HD_A, ka, tail).astype(BF16)
        kaug_ref[1] = jnp.where(lane < HD_A, pltpu.roll(ka, HD_A, 1), tail).astype(BF16)
        vat = va.T
        srow = lax.broadcasted_iota(jnp.int32, (V_ROWS - HD_A, tm), 0)
        ones_rows = jnp.where(srow == 0, 1.0, 0.0).astype(BF16)
        for g in range(N_KV_A):
            vt_ref[g, 0:HD_A, :] = vat[g * HD_A:(g + 1) * HD_A, :].astype(BF16)
            vt_ref[g, HD_A:V_ROWS, :] = ones_rows

    for c in range(D_QIDX // LANES):
        blk = _rope_lanes(p[:, C_QI + c * LANES:C_QI + (c + 1) * LANES], cosi, sini, D_IDX // 2)
        if dsa_layout:
            qit_ref[c * LANES:(c + 1) * LANES, :] = blk.T.astype(BF16)
        else:
            qi_ref[:, c * LANES:(c + 1) * LANES] = blk.astype(BF16)

    kw = p[:, C_KW:C_KW + LANES]
    ss = _group_sumsq(kw, bdki_ref[...])
    kin = kw * lax.rsqrt(ss * (1.0 / D_IDX) + EPS) * gki_ref[...]
    kin = _rope_lanes(kin, cosi, sini, D_IDX // 2)
    lane = lax.broadcasted_iota(jnp.int32, kw.shape, 1)
    kw = jnp.where(lane < D_IDX, kin, kw * IDX_SCALE)
    kw_ref[...] = kw
    if dsa_layout:
        kib_ref[...] = kw[:, :D_IDX].astype(BF16)
        wit_ref[...] = kw.T[D_IDX:D_IDX + N_IDX_HEADS, :]

    ub_ref[...] = p[:, C_UB:C_UB + D_POOL]

    gqm = gqm_ref[...]
    for hd in range(N_HEADS_M):
        blk = p[:, C_QM + hd * HD_M:C_QM + (hd + 1) * HD_M]
        qm_ref[:, hd * HD_M:(hd + 1) * HD_M] = _rms(blk, gqm).astype(BF16)


def _proj_call(x2d, gmix, w_cat, tabs, gqa_t, gka_t, gki_t, gqm, bd64, bdki, tm, n_tab_blocks, dsa_layout):
    n = x2d.shape[0]
    d = x2d.shape[1]
    nb = n // (tm * n_tab_blocks)
    grid = (n // tm,)
    row = lambda i: (i, 0)
    const = lambda i: (0, 0)
    tab = lambda i: (i % n_tab_blocks, 0)
    in_specs = [
        pl.BlockSpec((tm, d), row),
        pl.BlockSpec((1, d), const),
        pl.BlockSpec((d, C_END), const),
        pl.BlockSpec((tm, LANES), tab), pl.BlockSpec((tm, LANES), tab),
        pl.BlockSpec((tm, LANES), tab), pl.BlockSpec((tm, LANES), tab),
        pl.BlockSpec((1, D_ATT_A), const), pl.BlockSpec((1, D_KV_A), const),
        pl.BlockSpec((1, LANES), const), pl.BlockSpec((1, HD_M), const),
        pl.BlockSpec((D_ATT_A, D_ATT_A), const), pl.BlockSpec((LANES, LANES), const),
    ]
    rows = lambda width, dtype: (jax.ShapeDtypeStruct((n, width), dtype), pl.BlockSpec((tm, width), row))
    per_token = [rows(D_KV_A, F32), rows(D_KV_A, F32), rows(LANES, F32), rows(D_POOL, F32),
                 rows(D_MEM_ATT, BF16)]
    if dsa_layout:
        t = tm * n_tab_blocks
        cols = lambda i: (i // n_tab_blocks, 0, i % n_tab_blocks)
        tposed = lambda r, dtype: (jax.ShapeDtypeStruct((nb, r, t), dtype), pl.BlockSpec((None, r, tm), cols))
        outs = per_token + [
            tposed(D_ATT_A, BF16), tposed(D_QIDX, BF16), tposed(N_IDX_HEADS, F32),
            (jax.ShapeDtypeStruct((N_KV_A, n, LANES), BF16), pl.BlockSpec((N_KV_A, tm, LANES), lambda i: (0, i, 0))),
            (jax.ShapeDtypeStruct((nb, N_KV_A, n_tab_blocks, V_ROWS, tm), BF16),
             pl.BlockSpec((None, N_KV_A, None, V_ROWS, tm),
                          lambda i: (i // n_tab_blocks, 0, i % n_tab_blocks, 0, 0))),
            rows(D_IDX, BF16),
        ]
    else:
        outs = [rows(D_ATT_A, BF16)] + per_token[:2] + [rows(D_QIDX, BF16)] + per_token[2:]
    return pl.pallas_call(
        functools.partial(_proj_kernel, dsa_layout=dsa_layout), grid=grid, in_specs=in_specs,
        out_specs=[o[1] for o in outs], out_shape=[o[0] for o in outs],
        compiler_params=_cparams(("parallel",)), name="proj",
    )(x2d, gmix, w_cat, *tabs, gqa_t, gka_t, gki_t, gqm, bd64, bdki)


def _key_bits_to_float(u):
    k = u ^ jnp.int32(-2147483648)
    bits = jnp.where(k >= 0, k, k ^ jnp.int32(0x7FFFFFFF))
    return lax.bitcast_convert_type(bits, F32)


def _dsa_kernel(qat_ref, qit_ref, wit_ref, k_ref, vt_ref, ki_ref, o_ref,
                sc_ref, m_ref, acc_ref, qaug_ref, qis_ref, km_ref,
                *, tq, kb, nkb_total, causal, s_valid, topk, rows_out):
    step = pl.program_id(1)
    nqb = pl.num_programs(1) - 1
    cur = lax.rem(step, 2)
    prev = 1 - cur
    per = kb // tq
    if causal:
        nkb = lax.div(step + per, jnp.int32(per))
        nkb_prev = lax.div(step - 1 + per, jnp.int32(per))
    else:
        nkb = jnp.int32(nkb_total)
        nkb_prev = jnp.where(step > 0, nkb_total, 0)
    krow = lax.broadcasted_iota(jnp.int32, (kb, tq), 0)
    crow = lax.broadcasted_iota(jnp.int32, (COUNT_ROWS, tq), 0)
    kf = float(topk)

    def attend(slot, j, g):
        lg = jnp.dot(k_ref[g, j], qaug_ref[slot, g], preferred_element_type=F32)
        bias = sc_ref[slot, j]
        vt = vt_ref[g, j]
        for u in range(GROUP_A):
            p = jnp.exp2(lg[:, u * tq:(u + 1) * tq] + bias).astype(BF16)
            acc_ref[g * GROUP_A + u] += jnp.dot(vt, p, preferred_element_type=F32)

    @pl.when(step == 0)
    def _():
        for g in range(N_KV_A):
            def norm_body(j, mx, g=g):
                kk = k_ref[g, j].astype(F32)
                return jnp.maximum(mx, jnp.sum(kk * kk, axis=1, keepdims=True))
            mx = lax.fori_loop(0, nkb_total, norm_body, jnp.zeros((kb, 1), F32))
            km_ref[g] = jnp.broadcast_to(jnp.max(mx, axis=0, keepdims=True), (SUBLANES, LANES))

    @pl.when(step < nqb)
    def _select():
        qpos = step * tq + lax.broadcasted_iota(jnp.int32, (1, tq), 1)
        if causal:
            lim = (lax.shift_right_logical(qpos, 6) + 1) * CHUNK
        else:
            lim = jnp.full((1, tq), s_valid, jnp.int32)
        limf = lim.astype(F32)

        for h in range(N_IDX_HEADS):
            qis_ref[:, h * tq:(h + 1) * tq] = qit_ref[h * D_IDX:(h + 1) * D_IDX, :]
        qit = qis_ref[...]
        wit = wit_ref[...]

        def score_body(j, carry):
            z = jnp.dot(ki_ref[j], qit, preferred_element_type=F32)
            s = jnp.zeros((kb, tq), F32)
            for h in range(N_IDX_HEADS):
                s = s + wit[h:h + 1, :] * jnp.maximum(z[:, h * tq:(h + 1) * tq], 0.0)
            sc_ref[cur, j] = jnp.where(j * kb + krow < lim, s, -jnp.inf)
            return carry

        lax.fori_loop(0, nkb, score_body, 0)

        aug_row = lax.broadcasted_iota(jnp.int32, (LANES - HD_A, tq), 0)
        for g in range(N_KV_A):
            kmax2 = km_ref[g][0:1, :]
            kmax2 = jnp.concatenate([kmax2] * (tq // LANES), axis=1) if tq >= LANES else kmax2[:, :tq]
            for u in range(GROUP_A):
                hd = g * GROUP_A + u
                q = qat_ref[hd * HD_A:(hd + 1) * HD_A, :]
                qf = q.astype(F32)
                shift = jnp.sqrt(jnp.sum(qf * qf, axis=0, keepdims=True) * kmax2)
                qaug_ref[cur, g, 0:HD_A, u * tq:(u + 1) * tq] = q
                qaug_ref[cur, g, HD_A:LANES, u * tq:(u + 1) * tq] = (
                    jnp.where(aug_row == 0, -shift, 0.0).astype(BF16))

        acc_ref[...] = jnp.zeros(acc_ref.shape, F32)
        no_counts = (jnp.zeros((COUNT_ROWS, tq), F32),) * COUNT_CHAINS

        def tally(j, accs, pred):
            accs = list(accs)
            for ci, r in enumerate(range(0, kb, COUNT_ROWS)):
                hit = pred(sc_ref[cur, j, r:r + COUNT_ROWS, :], j * kb + r + crow)
                accs[ci % COUNT_CHAINS] = accs[ci % COUNT_CHAINS] + jnp.where(hit, 1.0, 0.0)
            return tuple(accs)

        def total(accs):
            return jnp.sum(sum(accs[1:], accs[0]), axis=0, keepdims=True)

        def count(pred):
            return total(lax.fori_loop(0, nkb, lambda j, accs: tally(j, accs, pred), no_counts))

        def count_ge_fused(i, t):
            per_pass = lax.div(nkb + (FUSED_PASSES - 1), jnp.int32(FUSED_PASSES))
            first = i * per_pass
            n_att = jnp.clip(nkb_prev - first, 0, per_pass)

            def fused(it, accs):
                ja = first + it
                todo = list(range(FUSED_PASSES))
                for g in range(N_KV_A):
                    lg = jnp.dot(k_ref[g, ja], qaug_ref[prev, g], preferred_element_type=F32)
                    for u in range(GROUP_A):
                        share = len(todo) // (N_HEADS_A - g * GROUP_A - u)
                        for w in [todo.pop(0) for _ in range(share)]:
                            j = it * FUSED_PASSES + w
                            tj = jnp.where(j < nkb, t, jnp.inf)
                            accs = tally(jnp.minimum(j, nkb - 1), accs, lambda s, idx, tj=tj: s >= tj)
                        p = jnp.exp2(lg[:, u * tq:(u + 1) * tq] + sc_ref[prev, ja]).astype(BF16)
                        acc_ref[g * GROUP_A + u] += jnp.dot(vt_ref[g, ja], p, preferred_element_type=F32)
                return accs

            accs = lax.fori_loop(0, n_att, fused, no_counts)
            accs = lax.fori_loop(n_att * FUSED_PASSES, nkb, lambda j, a: tally(j, a, lambda s, idx: s >= t), accs)
            return total(accs)

        def radix_pass(i, carry, fused):
            prefix, cge = carry
            cand = prefix | lax.shift_left(jnp.int32(1), KEY_BITS - 1 - i)
            t = _key_bits_to_float(cand)
            c = count_ge_fused(i, t) if fused else count(lambda s, idx: s >= t)
            take = c >= kf
            return jnp.where(take, cand, prefix), jnp.where(take, c, cge)

        carry = (jnp.zeros((1, tq), jnp.int32), limf)
        carry = lax.fori_loop(0, FUSED_PASSES, lambda i, c: radix_pass(i, c, True), carry)
        prefix, cge = lax.fori_loop(FUSED_PASSES, KEY_BITS, lambda i, c: radix_pass(i, c, False), carry)
        thr = jnp.where(limf >= kf, _key_bits_to_float(prefix), -FLT_MAX)

        any_tie = jnp.max(jnp.where(cge > kf, 1.0, 0.0), axis=1, keepdims=True)

        def mask_plain():
            def body(j, carry):
                sc_ref[cur, j] = jnp.where(sc_ref[cur, j] >= thr, 0.0, MASK_VALUE)
                return carry
            lax.fori_loop(0, nkb, body, 0)

        def mask_ties():
            cgt = count(lambda s, idx: s > thr)
            need = kf - cgt
            nbits = int(nkb_total * kb - 1).bit_length()

            def idx_body(i, j0):
                cand = j0 | lax.shift_left(jnp.int32(1), nbits - 1 - i)
                c = count(lambda s, idx: (s == thr) & (idx < cand))
                return jnp.where(c < need, cand, j0)

            j0 = lax.fori_loop(0, nbits, idx_body, jnp.zeros((1, tq), jnp.int32))

            def body(j, carry):
                s = sc_ref[cur, j]
                keep = (s > thr) | ((s == thr) & (j * kb + krow <= j0))
                sc_ref[cur, j] = jnp.where(keep, 0.0, MASK_VALUE)
                return carry
            lax.fori_loop(0, nkb, body, 0)

        lax.cond(any_tie[0, 0] > 0.0, mask_ties, mask_plain)

    @pl.when(step == nqb)
    def _attend_last():
        acc_ref[...] = jnp.zeros(acc_ref.shape, F32)

        def body(j, carry):
            for g in range(N_KV_A):
                attend(prev, j, g)
            return carry
        lax.fori_loop(0, nkb_prev, body, 0)

    @pl.when(step > 0)
    def _finish_prev():
        sums_ok = jnp.ones((1, tq), F32)
        for hd in range(N_HEADS_A):
            sums_ok = jnp.where(acc_ref[hd][HD_A:HD_A + 1, :] > SUM_FLOOR, sums_ok, 0.0)
        all_ok = jnp.min(sums_ok, axis=1, keepdims=True)

        @pl.when(all_ok[0, 0] < 1.0)
        def _():
            m_ref[...] = jnp.full(m_ref.shape, MASK_VALUE, F32)
            acc_ref[...] = jnp.zeros(acc_ref.shape, F32)

            def online_body(j, carry):
                bias = sc_ref[prev, j]
                for g in range(N_KV_A):
                    lg = jnp.dot(k_ref[g, j], qaug_ref[prev, g], preferred_element_type=F32)
                    vt = vt_ref[g, j]
                    for u in range(GROUP_A):
                        hd = g * GROUP_A + u
                        l = lg[:, u * tq:(u + 1) * tq] + bias
                        m_old = m_ref[hd:hd + 1, :]
                        m_new = jnp.maximum(m_old, jnp.max(l, axis=0, keepdims=True))
                        p = jnp.exp2(l - m_new).astype(BF16)
                        pv = jnp.dot(vt, p, preferred_element_type=F32)
                        acc_ref[hd] = jnp.exp2(m_old - m_new) * acc_ref[hd] + pv
                        m_ref[hd:hd + 1, :] = m_new
                return carry

            lax.fori_loop(0, nkb_prev, online_body, 0)

        heads = []
        for hd in range(N_HEADS_A):
            a = acc_ref[hd]
            heads.append(a[:HD_A, :] / a[HD_A:HD_A + 1, :])
        out_t = jnp.concatenate(heads, axis=0)
        o_ref[...] = (out_t.T if rows_out else out_t).astype(BF16)


def _dsa_call(qat, qit, wit, k, vt, ki, *, tq, kb, causal, s_valid, topk):
    b, t = qat.shape[0], qat.shape[2]
    nkb_total = k.shape[2]
    rows_out = tq % LANES == 0
    kern = functools.partial(_dsa_kernel, tq=tq, kb=kb, nkb_total=nkb_total, causal=causal,
                             s_valid=s_valid, topk=topk, rows_out=rows_out)
    nqb = t // tq
    qcols = lambda i, j: (i, 0, jnp.minimum(j, nqb - 1))
    ocols = lambda i, j: (i, 0, jnp.maximum(j - 1, 0))
    in_specs = [
        pl.BlockSpec((None, D_ATT_A, tq), qcols),
        pl.BlockSpec((None, D_QIDX, tq), qcols),
        pl.BlockSpec((None, N_IDX_HEADS, tq), qcols),
        pl.BlockSpec((N_KV_A, None, nkb_total, kb, LANES), lambda i, j: (0, i, 0, 0, 0)),
        pl.BlockSpec((None, N_KV_A, nkb_total, V_ROWS, kb), lambda i, j: (i, 0, 0, 0, 0)),
        pl.BlockSpec((None, nkb_total, kb, D_IDX), lambda i, j: (i, 0, 0, 0)),
    ]
    if rows_out:
        out_spec = pl.BlockSpec((None, tq, D_ATT_A), lambda i, j: (i, jnp.maximum(j - 1, 0), 0))
        out_shape = jax.ShapeDtypeStruct((b, t, D_ATT_A), BF16)
    else:
        out_spec = pl.BlockSpec((None, D_ATT_A, tq), ocols)
        out_shape = jax.ShapeDtypeStruct((b, D_ATT_A, t), BF16)
    return pl.pallas_call(
        kern, grid=(b, nqb + 1), in_specs=in_specs, out_specs=out_spec, out_shape=out_shape,
        scratch_shapes=[pltpu.VMEM((2, nkb_total, kb, tq), F32),
                        pltpu.VMEM((N_HEADS_A, tq), F32),
                        pltpu.VMEM((N_HEADS_A, V_ROWS, tq), F32),
                        pltpu.VMEM((2, N_KV_A, LANES, GROUP_A * tq), BF16),
                        pltpu.VMEM((D_IDX, N_IDX_HEADS * tq), BF16),
                        pltpu.VMEM((N_KV_A, SUBLANES, LANES), F32)],
        compiler_params=_cparams(("parallel", "arbitrary")), name="dsa",
    )(qat, qit, wit, k, vt, ki)


def _memkv_kernel(mem_ref, gmem_ref, w_ref, gkm_ref, k_ref, v_ref):
    h = _rms(mem_ref[...], gmem_ref[...]).astype(BF16)
    kv = jnp.dot(h, w_ref[...], preferred_element_type=F32)
    gkm = gkm_ref[...]
    for hd in range(N_HEADS_M):
        k_ref[:, hd * HD_M:(hd + 1) * HD_M] = _rms(kv[:, hd * HD_M:(hd + 1) * HD_M], gkm)
    v_ref[...] = kv[:, D_MEM_ATT:]


def _memkv_call(mem2d, gmem, w_kv, gkm, tm):
    n, d = mem2d.shape
    row = lambda i: (i, 0)
    const = lambda i: (0, 0)
    return pl.pallas_call(
        _memkv_kernel, grid=(n // tm,),
        in_specs=[pl.BlockSpec((tm, d), row), pl.BlockSpec((1, d), const),
                  pl.BlockSpec((d, 2 * D_MEM_ATT), const), pl.BlockSpec((1, HD_M), const)],
        out_specs=[pl.BlockSpec((tm, D_MEM_ATT), row), pl.BlockSpec((tm, D_MEM_ATT), row)],
        out_shape=[jax.ShapeDtypeStruct((n, D_MEM_ATT), F32), jax.ShapeDtypeStruct((n, D_MEM_ATT), F32)],
        compiler_params=_cparams(("parallel",)), name="memkv",
    )(mem2d, gmem, w_kv, gkm)


def _merge_kernel(x_ref, a_ref, ub_ref, prev_ref, hist0_ref, qm_ref, mkt_ref, mv_ref,
                  gmix_ref, wg_ref, woa_ref, wob_ref, wom_ref, wout_ref, wpool_ref, spool_ref,
                  o_ref, ext_ref, *, tm, pos0):
    it = pl.program_id(1)
    x = x_ref[...]

    h = _rms(x, gmix_ref[...]).astype(BF16)
    gates = jax.nn.sigmoid(jnp.dot(h, wg_ref[...], preferred_element_type=F32))
    d = x.shape[1]

    ub = ub_ref[...]
    ext_ref[0:HIST_ROWS, :] = jnp.where(it == 0, hist0_ref[...], prev_ref[...])
    ext_ref[HIST_ROWS:HIST_ROWS + tm, :] = ub
    pos = pos0 + it * tm + lax.broadcasted_iota(jnp.int32, (tm, 1), 0)
    ys = []
    for g, w in enumerate(POOL_WINDOWS):
        c0 = g * POOL_GROUP
        win = ub[:, c0:c0 + POOL_GROUP]
        for k in range(1, w):
            win = win + ext_ref[HIST_ROWS - k:HIST_ROWS - k + tm, c0:c0 + POOL_GROUP]
        cnt = jnp.minimum(w, pos + 1).astype(F32)
        pg = (win / cnt - ub[:, c0:c0 + POOL_GROUP]).astype(BF16)
        ys.append(jnp.dot(pg, wpool_ref[g], preferred_element_type=F32))
    bmix = (jnp.concatenate(ys, axis=1) * spool_ref[...]).astype(BF16)

    qm = qm_ref[...]
    ms = []
    for hd in range(N_HEADS_M):
        lg = jnp.dot(qm[:, hd * HD_M:(hd + 1) * HD_M], mkt_ref[hd], preferred_element_type=F32)
        lg = lg * (HD_M ** -0.5)
        e = jnp.exp(lg - jnp.max(lg, axis=1, keepdims=True))
        pr = (e / jnp.sum(e, axis=1, keepdims=True)).astype(BF16)
        ms.append(jnp.dot(pr, mv_ref[hd], preferred_element_type=F32))
    mmix = jnp.concatenate(ms, axis=1).astype(BF16)

    mixed = (gates[:, 0:d] * jnp.dot(a_ref[...], woa_ref[...], preferred_element_type=F32)
             + gates[:, d:2 * d] * jnp.dot(bmix, wob_ref[...], preferred_element_type=F32)
             + gates[:, 2 * d:3 * d] * jnp.dot(mmix, wom_ref[...], preferred_element_type=F32))
    o_ref[...] = x + jnp.dot(mixed.astype(BF16), wout_ref[...], preferred_element_type=F32)


def _merge_call(x, a, ub, hist0, qm, mkt, mv, gmix, wg, woa, wob, wom, wout, wpool, spool, *, tm, pos0):
    b, t, d = x.shape
    nt = t // tm
    hb = tm // HIST_ROWS
    tok = lambda i, j: (i, j, 0)
    c2 = lambda i, j: (0, 0)
    c3 = lambda i, j: (0, 0, 0)
    per_b3 = lambda i, j: (i, 0, 0)
    per_b4 = lambda i, j: (i, 0, 0, 0)
    in_specs = [
        pl.BlockSpec((None, tm, d), tok),
        pl.BlockSpec((None, tm, D_ATT_A), tok),
        pl.BlockSpec((None, tm, D_POOL), tok),
        pl.BlockSpec((None, HIST_ROWS, D_POOL), lambda i, j: (i, jnp.maximum(j * hb - 1, 0), 0)),
        pl.BlockSpec((None, HIST_ROWS, D_POOL), per_b3),
        pl.BlockSpec((None, tm, D_MEM_ATT), tok),
        pl.BlockSpec((None, N_HEADS_M, HD_M, mkt.shape[3]), per_b4),
        pl.BlockSpec((None, N_HEADS_M, mv.shape[2], HD_M), per_b4),
        pl.BlockSpec((1, d), c2),
        pl.BlockSpec(wg.shape, c2),
        pl.BlockSpec(woa.shape, c2), pl.BlockSpec(wob.shape, c2), pl.BlockSpec(wom.shape, c2),
        pl.BlockSpec(wout.shape, c2),
        pl.BlockSpec(wpool.shape, c3),
        pl.BlockSpec((1, D_POOL), c2),
    ]
    return pl.pallas_call(
        functools.partial(_merge_kernel, tm=tm, pos0=pos0), grid=(b, nt), in_specs=in_specs,
        out_specs=pl.BlockSpec((None, tm, d), tok),
        out_shape=jax.ShapeDtypeStruct((b, t, d), F32),
        scratch_shapes=[pltpu.VMEM((HIST_ROWS + tm, D_POOL), F32)],
        compiler_params=_cparams(("parallel", "arbitrary")), name="merge",
    )(x, a, ub, ub, hist0, qm, mkt, mv, gmix, wg, woa, wob, wom, wout, wpool, spool)


def _ffn_kernel(x_ref, g_ref, wgate_ref, wup_ref, wdown_ref, o_ref):
    x = x_ref[...]
    h = _rms(x, g_ref[...]).astype(BF16)
    gate = jnp.dot(h, wgate_ref[...], preferred_element_type=F32)
    up = jnp.dot(h, wup_ref[...], preferred_element_type=F32)
    act = (jax.nn.silu(gate) * up).astype(BF16)
    o_ref[...] = x + jnp.dot(act, wdown_ref[...], preferred_element_type=F32)


def _ffn_call(x2d, g, wgate, wup, wdown, tm):
    n, d = x2d.shape
    row = lambda i: (i, 0)
    const = lambda i: (0, 0)
    return pl.pallas_call(
        _ffn_kernel, grid=(n // tm,),
        in_specs=[pl.BlockSpec((tm, d), row), pl.BlockSpec((1, d), const),
                  pl.BlockSpec(wgate.shape, const), pl.BlockSpec(wup.shape, const),
                  pl.BlockSpec(wdown.shape, const)],
        out_specs=pl.BlockSpec((tm, d), row),
        out_shape=jax.ShapeDtypeStruct((n, d), F32),
        compiler_params=_cparams(("parallel",)), name="ffn",
    )(x2d, g, wgate, wup, wdown)


def _rope_tables(pos, head_dim):
    half = head_dim // 2
    inv = ROPE_THETA ** (-jnp.arange(half, dtype=F32) / half)
    ang = pos.astype(F32)[:, None] * inv[None, :]
    reps = LANES // half
    return jnp.tile(jnp.cos(ang), (1, reps)), jnp.tile(jnp.sin(ang), (1, reps))


def _pack_weights(w_in, g_qa, g_ka, g_kidx, g_qm):
    d = w_in.shape[0]
    widths = (D_ATT_A, D_KV_A, D_KV_A, D_QIDX, D_IDX, N_IDX_HEADS, D_POOL, D_MEM_ATT, N_BRANCH * d)
    cuts = [int(c) for c in np.cumsum(widths)[:-1]]
    wqa, wka, wva, wqi, wki, wwi, wub, wqm, wgates = jnp.split(w_in, cuts, axis=1)
    pad = jnp.zeros((d, LANES - D_IDX - N_IDX_HEADS), w_in.dtype)
    w_cat = jnp.concatenate([wqa, wka, wva, wqi, wki, wwi, pad, wub, wqm], axis=1).astype(BF16)
    gqa_t = jnp.tile(g_qa, N_HEADS_A)[None, :]
    gka_t = jnp.tile(g_ka, N_KV_A)[None, :]
    gki_t = jnp.concatenate([g_kidx, jnp.ones((LANES - D_IDX,), g_kidx.dtype)])[None, :]
    return w_cat, wgates.astype(BF16), gqa_t, gka_t, gki_t, g_qm[None, :]


def _block_diag_ones(n, group, limit=None):
    i = np.arange(n)
    m = (i[:, None] // group) == (i[None, :] // group)
    if limit is not None:
        m = m & (i[:, None] < limit) & (i[None, :] < limit)
    return jnp.asarray(m, BF16)


def _key_blocks(k_all, v_all, ki_all, kb):
    b, s = k_all.shape[0], k_all.shape[1]
    nkb = -(-s // kb)
    pad = nkb * kb - s
    if pad:
        k_all = jnp.pad(k_all, ((0, 0), (0, pad), (0, 0), (0, 0)))
        v_all = jnp.pad(v_all, ((0, 0), (0, pad), (0, 0), (0, 0)))
        ki_all = jnp.pad(ki_all, ((0, 0), (0, pad), (0, 0)))
    ones = jnp.ones(v_all.shape[:-1] + (1,), BF16)
    zeros = jnp.zeros(v_all.shape[:-1] + (LANES - HD_A - 1,), BF16)
    kaug = jnp.concatenate([k_all.astype(BF16), ones, zeros], axis=-1)
    k = kaug.reshape(b, nkb, kb, N_KV_A, LANES).transpose(3, 0, 1, 2, 4)
    vaug = jnp.concatenate([v_all.astype(BF16), ones, zeros[..., :V_ROWS - HD_A - 1]], axis=-1)
    vt = vaug.reshape(b, nkb, kb, N_KV_A, V_ROWS).transpose(0, 3, 1, 4, 2)
    ki = ki_all.astype(BF16).reshape(b, nkb, kb, D_IDX)
    return k, vt, ki


def _tile(n, pref):
    t = pref
    while n % t:
        t //= 2
    return t


def _group_forward(x, pos, pos0, hist0, k_hist, v_hist, ki_hist, mkt, mv, wts, *, causal):
    (gmix, w_cat, wgates, gqa_t, gka_t, gki_t, gqm, bd64, bdki, wpool, spool,
     woa, wob, wom, wout, gffn, wgate, wup, wdown) = wts
    b, t, d = x.shape
    n = b * t
    tm = _tile(t, 512)
    cosa, sina = _rope_tables(pos, HD_A)
    cosi, sini = _rope_tables(pos, D_IDX)
    tq = _tile(t, 256)
    proj_args = (x.reshape(n, d), gmix, w_cat, (cosa, sina, cosi, sini), gqa_t, gka_t, gki_t, gqm, bd64, bdki,
                 tm, t // tm)
    if causal:
        ka, va, kw, ub, qm, qat, qit, wit, kaug, vt, kib = _proj_call(*proj_args, dsa_layout=True)
        kb, s = tm, t
        kblk = kaug.reshape(N_KV_A, b, t // kb, kb, LANES)
        kiblk = kib.reshape(b, t // kb, kb, D_IDX)
    else:
        qa, ka, va, qi, kw, ub, qm = _proj_call(*proj_args, dsa_layout=False)
        qat = qa.reshape(b, t, D_ATT_A).transpose(0, 2, 1)
        qit = qi.reshape(b, t, D_QIDX).transpose(0, 2, 1)
        wit = kw[:, D_IDX:D_IDX + N_IDX_HEADS].reshape(b, t, N_IDX_HEADS).transpose(0, 2, 1)
    ka = ka.reshape(b, t, N_KV_A, HD_A)
    va = va.reshape(b, t, N_KV_A, HD_A)
    ki = kw[:, :D_IDX].reshape(b, t, D_IDX)
    ub = ub.reshape(b, t, D_POOL)
    if not causal:
        k_all = jnp.concatenate([k_hist, ka], axis=1)
        s, kb = k_all.shape[1], 256
        kblk, vt, kiblk = _key_blocks(k_all, jnp.concatenate([v_hist, va], axis=1),
                                      jnp.concatenate([ki_hist, ki], axis=1), kb)
    topk = min(TOPK_MAX, s // 4)
    a = _dsa_call(qat, qit, wit, kblk, vt, kiblk, tq=tq, kb=kb, causal=causal, s_valid=s, topk=topk)
    if tq % LANES:
        a = a.transpose(0, 2, 1)

    tmm = _tile(t, 256)
    x2 = _merge_call(x, a, ub, hist0, qm.reshape(b, t, D_MEM_ATT), mkt, mv, gmix, wgates,
                     woa, wob, wom, wout, wpool, spool, tm=tmm, pos0=pos0)
    y = _ffn_call(x2.reshape(n, d), gffn, wgate, wup, wdown, _tile(n, 256)).reshape(b, t, d)
    return y, ka, va, ki, ub


def kernel(x_prompt, x_sample, mem_prompt, cache_a_k, cache_a_v, cache_idx_k, cache_pool, cache_mem_k,
           cache_mem_v, g_mix, w_in, g_qa, g_ka, g_kidx, g_qm, g_mem, w_mem_kv, g_km, w_pool, s_pool,
           w_oa, w_ob, w_om, w_out, g_ffn, w_gate, w_up, w_down):
    depth = w_in.shape[0]
    t = x_prompt.shape[1]
    ts = x_sample.shape[1]
    past = cache_a_k.shape[2]
    b = x_prompt.shape[0]
    n_mem = mem_prompt.shape[1]
    pos_p = jnp.arange(t, dtype=jnp.int32)
    pos_s = past + jnp.arange(ts, dtype=jnp.int32)
    bd64 = _block_diag_ones(D_ATT_A, HD_A)
    bdki = _block_diag_ones(LANES, LANES, limit=D_IDX)

    xp, xs = x_prompt, x_sample
    outs = [[] for _ in range(10)]
    for l in range(depth):
        w_cat, wgates, gqa_t, gka_t, gki_t, gqm = _pack_weights(w_in[l], g_qa[l], g_ka[l], g_kidx[l], g_qm[l])
        wts = (g_mix[l][None, :], w_cat, wgates, gqa_t, gka_t, gki_t, gqm, bd64, bdki,
               w_pool[l].astype(BF16), s_pool[l][None, :],
               w_oa[l].astype(BF16), w_ob[l].astype(BF16), w_om[l].astype(BF16), w_out[l].astype(BF16),
               g_ffn[l][None, :], w_gate[l].astype(BF16), w_up[l].astype(BF16), w_down[l].astype(BF16))

        mk, mv = _memkv_call(mem_prompt.reshape(b * n_mem, -1), g_mem[l][None, :],
                             w_mem_kv[l].astype(BF16), g_km[l][None, :], _tile(b * n_mem, 256))
        mk = mk.reshape(b, n_mem, N_HEADS_M, HD_M)
        mv = mv.reshape(b, n_mem, N_HEADS_M, HD_M)
        hist0 = jnp.zeros((b, HIST_ROWS, D_POOL), F32)
        xp, ka, va, ki, ub = _group_forward(
            xp, pos_p, 0, hist0, None, None, None,
            mk.astype(BF16).transpose(0, 2, 3, 1), mv.astype(BF16).transpose(0, 2, 1, 3), wts, causal=True)
        for lst, val in zip(outs[:6], (ka, va, ki, ub[:, -POOL_HIST:], mk, mv)):
            lst.append(val)

        bs = xs.shape[0]
        hist0 = jnp.concatenate([jnp.zeros((bs, HIST_ROWS - POOL_HIST, D_POOL), F32), cache_pool[l]], axis=1)
        xs, ka, va, ki, ub = _group_forward(
            xs, pos_s, past, hist0, cache_a_k[l], cache_a_v[l], cache_idx_k[l],
            cache_mem_k[l].astype(BF16).transpose(0, 2, 3, 1), cache_mem_v[l].astype(BF16).transpose(0, 2, 1, 3),
            wts, causal=False)
        pool_s = jnp.concatenate([cache_pool[l], ub], axis=1)[:, -POOL_HIST:]
        for lst, val in zip(outs[6:], (ka, va, ki, pool_s)):
            lst.append(val)

    stacked = [jnp.stack(o) for o in outs]
    return (xp, xs, *stacked)
```

```python
import functools

import jax
import jax.numpy as jnp
import numpy as np
from jax import lax
from jax.experimental import pallas as pl
from jax.experimental.pallas import tpu as pltpu

F32 = jnp.float32
BF16 = jnp.bfloat16

CHUNK = 64
EPS = 1e-6
ROPE_THETA = 10000.0
N_HEADS_A = 8
N_KV_A = 2
GROUP_A = N_HEADS_A // N_KV_A
HD_A = 64
D_ATT_A = N_HEADS_A * HD_A
D_KV_A = N_KV_A * HD_A
N_IDX_HEADS = 8
D_IDX = 32
D_QIDX = N_IDX_HEADS * D_IDX
IDX_SCALE = (N_IDX_HEADS * D_IDX) ** -0.5
TOPK_MAX = 256
POOL_WINDOWS = (2, 4, 8, 16)
POOL_GROUP = 128
D_POOL = len(POOL_WINDOWS) * POOL_GROUP
POOL_HIST = 15
N_HEADS_M = 4
HD_M = 128
D_MEM_ATT = N_HEADS_M * HD_M
N_BRANCH = 3

LANES = 128
SUBLANES = 8
V7X_VMEM_BYTES = 64 * 1024 * 1024
VMEM_LIMIT = 56 * 1024 * 1024

HIST_ROWS = 2 * SUBLANES
MASK_VALUE = -1e30
FLT_MAX = float(np.finfo(np.float32).max)
KEY_BITS = 32
Q_SCALE = HD_A ** -0.5 * float(np.log2(np.e))
SUM_FLOOR = 1e-30
COUNT_ROWS = 4 * SUBLANES
COUNT_CHAINS = 2
V_ROWS = HD_A + 2 * SUBLANES

C_QA = 0
C_KA = C_QA + D_ATT_A
C_VA = C_KA + D_KV_A
C_QI = C_VA + D_KV_A
C_KW = C_QI + D_QIDX
C_UB = C_KW + LANES
C_QM = C_UB + D_POOL
C_END = C_QM + D_MEM_ATT


def _cparams(sem):
    return pltpu.CompilerParams(dimension_semantics=sem, vmem_limit_bytes=VMEM_LIMIT)


def _resident(shape):
    zeros = (0,) * len(shape)
    return pl.BlockSpec(shape, lambda *_: zeros, pipeline_mode=pl.Buffered(1))


def _rms(x, g):
    ms = jnp.mean(x * x, axis=-1, keepdims=True)
    return x * lax.rsqrt(ms + EPS) * g


def _group_sumsq(x, bd):
    sq = x * x
    hi = sq.astype(BF16)
    lo = (sq - hi.astype(F32)).astype(BF16)
    return (jnp.dot(hi, bd, preferred_element_type=F32)
            + jnp.dot(lo, bd, preferred_element_type=F32))


def _rope_lanes(x, cos, sin, half):
    lane = lax.broadcasted_iota(jnp.int32, x.shape, 1)
    first = (lane % (2 * half)) < half
    left = pltpu.roll(x, LANES - half, 1)
    right = pltpu.roll(x, half, 1)
    rot = jnp.where(first, -left, right)
    return x * cos + rot * sin


def _proj_kernel(x_ref, gmix_ref, w_ref, cosa_ref, sina_ref, cosi_ref, sini_ref,
                 gqa_ref, gka_ref, gki_ref, gqm_ref, bd64_ref, bdki_ref, *out_refs, dsa_layout):
    if dsa_layout:
        ka_ref, va_ref, kw_ref, ub_ref, qm_ref, qat_ref, qit_ref, wit_ref, kaug_ref, vt_ref, kib_ref = out_refs
    else:
        qa_ref, ka_ref, va_ref, qi_ref, kw_ref, ub_ref, qm_ref = out_refs
    x = x_ref[...]
    tm = x.shape[0]
    h = _rms(x, gmix_ref[...]).astype(BF16)
    p = jnp.dot(h, w_ref[...], preferred_element_type=F32)
    cosa, sina = cosa_ref[...], sina_ref[...]
    cosi, sini = cosi_ref[...], sini_ref[...]
    bd64 = bd64_ref[...]

    qa = p[:, C_QA:C_QA + D_ATT_A]
    ss = _group_sumsq(qa, bd64)
    qa = qa * lax.rsqrt(ss * (1.0 / HD_A) + EPS) * gqa_ref[...]
    for c in range(D_ATT_A // LANES):
        blk = _rope_lanes(qa[:, c * LANES:(c + 1) * LANES], cosa, sina, HD_A // 2) * Q_SCALE
        if dsa_layout:
            qat_ref[c * LANES:(c + 1) * LANES, :] = blk.T.astype(BF16)
        else:
            qa_ref[:, c * LANES:(c + 1) * LANES] = blk.astype(BF16)

    ka = p[:, C_KA:C_KA + D_KV_A]
    ss = _group_sumsq(ka, bd64[:D_KV_A, :D_KV_A])
    ka = ka * lax.rsqrt(ss * (1.0 / HD_A) + EPS) * gka_ref[...]
    ka = _rope_lanes(ka, cosa, sina, HD_A // 2)
    va = p[:, C_VA:C_VA + D_KV_A]
    ka_ref[...] = ka
    va_ref[...] = va
    if dsa_layout:
        lane = lax.broadcasted_iota(jnp.int32, ka.shape, 1)
        tail = jnp.where(lane == HD_A, 1.0, 0.0)
        kaug_ref[0] = jnp.where(lane < HD_A, ka, tail).astype(BF16)
        kaug_ref[1] = jnp.where(lane < HD_A, pltpu.roll(ka, HD_A, 1), tail).astype(BF16)
        vat = va.T
        srow = lax.broadcasted_iota(jnp.int32, (V_ROWS - HD_A, tm), 0)
        ones_rows = jnp.where(srow == 0, 1.0, 0.0).astype(BF16)
        for g in range(N_KV_A):
            vt_ref[g, 0:HD_A, :] = vat[g * HD_A:(g + 1) * HD_A, :].astype(BF16)
            vt_ref[g, HD_A:V_ROWS, :] = ones_rows

    for c in range(D_QIDX // LANES):
        blk = _rope_lanes(p[:, C_QI + c * LANES:C_QI + (c + 1) * LANES], cosi, sini, D_IDX // 2)
        if dsa_layout:
            qit_ref[c * LANES:(c + 1) * LANES, :] = blk.T.astype(BF16)
        else:
            qi_ref[:, c * LANES:(c + 1) * LANES] = blk.astype(BF16)

    kw = p[:, C_KW:C_KW + LANES]
    ss = _group_sumsq(kw, bdki_ref[...])
    kin = kw * lax.rsqrt(ss * (1.0 / D_IDX) + EPS) * gki_ref[...]
    kin = _rope_lanes(kin, cosi, sini, D_IDX // 2)
    lane = lax.broadcasted_iota(jnp.int32, kw.shape, 1)
    kw = jnp.where(lane < D_IDX, kin, kw * IDX_SCALE)
    kw_ref[...] = kw
    if dsa_layout:
        kib_ref[...] = kw[:, :D_IDX].astype(BF16)
        wit_ref[...] = kw.T[D_IDX:D_IDX + N_IDX_HEADS, :]

    ub_ref[...] = p[:, C_UB:C_UB + D_POOL]

    gqm = gqm_ref[...]
    for hd in range(N_HEADS_M):
        blk = p[:, C_QM + hd * HD_M:C_QM + (hd + 1) * HD_M]
        qm_ref[:, hd * HD_M:(hd + 1) * HD_M] = _rms(blk, gqm).astype(BF16)


def _proj_call(x2d, gmix, w_cat, tabs, gqa_t, gka_t, gki_t, gqm, bd64, bdki, tm, n_tab_blocks, dsa_layout):
    n = x2d.shape[0]
    d = x2d.shape[1]
    nb = n // (tm * n_tab_blocks)
    grid = (n // tm,)
    row = lambda i: (i, 0)
    tab = lambda i: (i % n_tab_blocks, 0)
    in_specs = [
        pl.BlockSpec((tm, d), row),
        _resident((1, d)),
        _resident((d, C_END)),
        pl.BlockSpec((tm, LANES), tab), pl.BlockSpec((tm, LANES), tab),
        pl.BlockSpec((tm, LANES), tab), pl.BlockSpec((tm, LANES), tab),
        _resident((1, D_ATT_A)), _resident((1, D_KV_A)),
        _resident((1, LANES)), _resident((1, HD_M)),
        _resident((D_ATT_A, D_ATT_A)), _resident((LANES, LANES)),
    ]
    rows = lambda width, dtype: (jax.ShapeDtypeStruct((n, width), dtype), pl.BlockSpec((tm, width), row))
    per_token = [rows(D_KV_A, F32), rows(D_KV_A, F32), rows(LANES, F32), rows(D_POOL, F32),
                 rows(D_MEM_ATT, BF16)]
    if dsa_layout:
        t = tm * n_tab_blocks
        cols = lambda i: (i // n_tab_blocks, 0, i % n_tab_blocks)
        tposed = lambda r, dtype: (jax.ShapeDtypeStruct((nb, r, t), dtype), pl.BlockSpec((None, r, tm), cols))
        outs = per_token + [
            tposed(D_ATT_A, BF16), tposed(D_QIDX, BF16), tposed(N_IDX_HEADS, F32),
            (jax.ShapeDtypeStruct((N_KV_A, n, LANES), BF16), pl.BlockSpec((N_KV_A, tm, LANES), lambda i: (0, i, 0))),
            (jax.ShapeDtypeStruct((nb, N_KV_A, n_tab_blocks, V_ROWS, tm), BF16),
             pl.BlockSpec((None, N_KV_A, None, V_ROWS, tm),
                          lambda i: (i // n_tab_blocks, 0, i % n_tab_blocks, 0, 0))),
            rows(D_IDX, BF16),
        ]
    else:
        outs = [rows(D_ATT_A, BF16)] + per_token[:2] + [rows(D_QIDX, BF16)] + per_token[2:]
    return pl.pallas_call(
        functools.partial(_proj_kernel, dsa_layout=dsa_layout), grid=grid, in_specs=in_specs,
        out_specs=[o[1] for o in outs], out_shape=[o[0] for o in outs],
        compiler_params=_cparams(("parallel",)), name="proj",
    )(x2d, gmix, w_cat, *tabs, gqa_t, gka_t, gki_t, gqm, bd64, bdki)


def _key_bits_to_float(u):
    k = u ^ jnp.int32(-2147483648)
    bits = jnp.where(k >= 0, k, k ^ jnp.int32(0x7FFFFFFF))
    return lax.bitcast_convert_type(bits, F32)


def _dsa_kernel(qat_ref, qit_ref, wit_ref, k_ref, vt_ref, ki_ref, o_ref,
                sc_ref, m_ref, acc_ref, qaug_ref, qis_ref, km_ref,
                *, tq, kb, nkb_total, causal, s_valid, topk, rows_out):
    qblk = pl.program_id(1)
    if causal:
        nkb = lax.div(qblk + kb // tq, jnp.int32(kb // tq))
    else:
        nkb = nkb_total
    qpos = qblk * tq + lax.broadcasted_iota(jnp.int32, (1, tq), 1)
    if causal:
        lim = (lax.shift_right_logical(qpos, 6) + 1) * CHUNK
    else:
        lim = jnp.full((1, tq), s_valid, jnp.int32)
    limf = lim.astype(F32)
    krow = lax.broadcasted_iota(jnp.int32, (kb, tq), 0)
    kf = float(topk)

    for h in range(N_IDX_HEADS):
        qis_ref[:, h * tq:(h + 1) * tq] = qit_ref[h * D_IDX:(h + 1) * D_IDX, :]
    qit = qis_ref[...]
    wit = wit_ref[...]

    def score_body(j, carry):
        z = jnp.dot(ki_ref[j], qit, preferred_element_type=F32)
        s = jnp.zeros((kb, tq), F32)
        for h in range(N_IDX_HEADS):
            s = s + wit[h:h + 1, :] * jnp.maximum(z[:, h * tq:(h + 1) * tq], 0.0)
        sc_ref[j] = jnp.where(j * kb + krow < lim, s, -jnp.inf)
        return carry

    lax.fori_loop(0, nkb, score_body, 0)

    crow = lax.broadcasted_iota(jnp.int32, (COUNT_ROWS, tq), 0)

    def count(pred):
        def body(j, accs):
            accs = list(accs)
            for ci, r in enumerate(range(0, kb, COUNT_ROWS)):
                hit = pred(sc_ref[j, r:r + COUNT_ROWS, :], j * kb + r + crow)
                accs[ci % COUNT_CHAINS] = accs[ci % COUNT_CHAINS] + jnp.where(hit, 1.0, 0.0)
            return tuple(accs)
        zero = jnp.zeros((COUNT_ROWS, tq), F32)
        accs = lax.fori_loop(0, nkb, body, (zero,) * COUNT_CHAINS)
        return jnp.sum(sum(accs[1:], accs[0]), axis=0, keepdims=True)

    def radix_body(i, carry):
        prefix, cge, cgt = carry
        cand = prefix | lax.shift_left(jnp.int32(1), KEY_BITS - 1 - i)
        t = _key_bits_to_float(cand)
        c = count(lambda s, idx: s >= t)
        take = c >= kf
        return jnp.where(take, cand, prefix), jnp.where(take, c, cge), jnp.where(take, cgt, c)

    prefix, cge, cgt = lax.fori_loop(
        0, KEY_BITS, radix_body, (jnp.zeros((1, tq), jnp.int32), limf, jnp.zeros((1, tq), F32)))
    thr = jnp.where(limf >= kf, _key_bits_to_float(prefix), -FLT_MAX)

    any_tie = jnp.max(jnp.where(cge > kf, 1.0, 0.0), axis=1, keepdims=True)

    def mask_plain():
        def body(j, carry):
            sc_ref[j] = jnp.where(sc_ref[j] >= thr, 0.0, MASK_VALUE)
            return carry
        lax.fori_loop(0, nkb, body, 0)

    def mask_ties():
        need = kf - cgt
        tri = (lax.broadcasted_iota(jnp.int32, (kb, kb), 1)
               <= lax.broadcasted_iota(jnp.int32, (kb, kb), 0)).astype(BF16)

        def body(j, seen):
            s = sc_ref[j]
            eq = s == thr
            rank = jnp.dot(tri, jnp.where(eq, 1.0, 0.0).astype(BF16), preferred_element_type=F32) + seen
            keep = (s > thr) | (eq & (rank <= need))
            sc_ref[j] = jnp.where(keep, 0.0, MASK_VALUE)
            return rank[kb - 1:kb, :]
        lax.fori_loop(0, nkb, body, jnp.zeros((1, tq), F32))

    lax.cond(any_tie[0, 0] > 0.0, mask_ties, mask_plain)

    @pl.when(qblk == 0)
    def _():
        for g in range(N_KV_A):
            def norm_body(j, mx, g=g):
                kk = k_ref[g, j].astype(F32)
                return jnp.maximum(mx, jnp.sum(kk * kk, axis=1, keepdims=True))
            mx = lax.fori_loop(0, nkb_total, norm_body, jnp.zeros((kb, 1), F32))
            km_ref[g] = jnp.broadcast_to(jnp.max(mx, axis=0, keepdims=True), (SUBLANES, LANES))

    aug_row = lax.broadcasted_iota(jnp.int32, (LANES - HD_A, tq), 0)
    for g in range(N_KV_A):
        kmax2 = km_ref[g][0:1, :]
        kmax2 = jnp.concatenate([kmax2] * (tq // LANES), axis=1) if tq >= LANES else kmax2[:, :tq]
        for u in range(GROUP_A):
            hd = g * GROUP_A + u
            q = qat_ref[hd * HD_A:(hd + 1) * HD_A, :]
            qf = q.astype(F32)
            shift = jnp.sqrt(jnp.sum(qf * qf, axis=0, keepdims=True) * kmax2)
            qaug_ref[g, 0:HD_A, u * tq:(u + 1) * tq] = q
            qaug_ref[g, HD_A:LANES, u * tq:(u + 1) * tq] = jnp.where(aug_row == 0, -shift, 0.0).astype(BF16)

    acc_ref[...] = jnp.zeros(acc_ref.shape, F32)

    def attn_body(j, carry):
        bias = sc_ref[j]
        for g in range(N_KV_A):
            lg = jnp.dot(k_ref[g, j], qaug_ref[g], preferred_element_type=F32)
            vt = vt_ref[g, j]
            for u in range(GROUP_A):
                hd = g * GROUP_A + u
                p = jnp.exp2(lg[:, u * tq:(u + 1) * tq] + bias).astype(BF16)
                acc_ref[hd] += jnp.dot(vt, p, preferred_element_type=F32)
        return carry

    lax.fori_loop(0, nkb, attn_body, 0)

    sums_ok = jnp.ones((1, tq), F32)
    for hd in range(N_HEADS_A):
        sums_ok = jnp.where(acc_ref[hd][HD_A:HD_A + 1, :] > SUM_FLOOR, sums_ok, 0.0)
    all_ok = jnp.min(sums_ok, axis=1, keepdims=True)

    @pl.when(all_ok[0, 0] < 1.0)
    def _():
        m_ref[...] = jnp.full(m_ref.shape, MASK_VALUE, F32)
        acc_ref[...] = jnp.zeros(acc_ref.shape, F32)

        def online_body(j, carry):
            bias = sc_ref[j]
            for g in range(N_KV_A):
                lg = jnp.dot(k_ref[g, j], qaug_ref[g], preferred_element_type=F32)
                vt = vt_ref[g, j]
                for u in range(GROUP_A):
                    hd = g * GROUP_A + u
                    l = lg[:, u * tq:(u + 1) * tq] + bias
                    m_old = m_ref[hd:hd + 1, :]
                    m_new = jnp.maximum(m_old, jnp.max(l, axis=0, keepdims=True))
                    p = jnp.exp2(l - m_new).astype(BF16)
                    pv = jnp.dot(vt, p, preferred_element_type=F32)
                    acc_ref[hd] = jnp.exp2(m_old - m_new) * acc_ref[hd] + pv
                    m_ref[hd:hd + 1, :] = m_new
            return carry

        lax.fori_loop(0, nkb, online_body, 0)

    heads = []
    for hd in range(N_HEADS_A):
        a = acc_ref[hd]
        heads.append(a[:HD_A, :] / a[HD_A:HD_A + 1, :])
    out_t = jnp.concatenate(heads, axis=0)
    o_ref[...] = (out_t.T if rows_out else out_t).astype(BF16)


def _dsa_call(qat, qit, wit, k, vt, ki, *, tq, kb, causal, s_valid, topk):
    b, t = qat.shape[0], qat.shape[2]
    nkb_total = k.shape[2]
    rows_out = tq % LANES == 0
    kern = functools.partial(_dsa_kernel, tq=tq, kb=kb, nkb_total=nkb_total, causal=causal,
                             s_valid=s_valid, topk=topk, rows_out=rows_out)
    qcols = lambda i, j: (i, 0, j)
    in_specs = [
        pl.BlockSpec((None, D_ATT_A, tq), qcols),
        pl.BlockSpec((None, D_QIDX, tq), qcols),
        pl.BlockSpec((None, N_IDX_HEADS, tq), qcols),
        pl.BlockSpec((N_KV_A, None, nkb_total, kb, LANES), lambda i, j: (0, i, 0, 0, 0)),
        pl.BlockSpec((None, N_KV_A, nkb_total, V_ROWS, kb), lambda i, j: (i, 0, 0, 0, 0)),
        pl.BlockSpec((None, nkb_total, kb, D_IDX), lambda i, j: (i, 0, 0, 0)),
    ]
    if rows_out:
        out_spec = pl.BlockSpec((None, tq, D_ATT_A), lambda i, j: (i, j, 0))
        out_shape = jax.ShapeDtypeStruct((b, t, D_ATT_A), BF16)
    else:
        out_spec = pl.BlockSpec((None, D_ATT_A, tq), qcols)
        out_shape = jax.ShapeDtypeStruct((b, D_ATT_A, t), BF16)
    return pl.pallas_call(
        kern, grid=(b, t // tq), in_specs=in_specs, out_specs=out_spec, out_shape=out_shape,
        scratch_shapes=[pltpu.VMEM((nkb_total, kb, tq), F32),
                        pltpu.VMEM((N_HEADS_A, tq), F32),
                        pltpu.VMEM((N_HEADS_A, V_ROWS, tq), F32),
                        pltpu.VMEM((N_KV_A, LANES, GROUP_A * tq), BF16),
                        pltpu.VMEM((D_IDX, N_IDX_HEADS * tq), BF16),
                        pltpu.VMEM((N_KV_A, SUBLANES, LANES), F32)],
        compiler_params=_cparams(("parallel", "arbitrary")), name="dsa",
    )(qat, qit, wit, k, vt, ki)


def _memkv_kernel(mem_ref, gmem_ref, w_ref, gkm_ref, k_ref, v_ref):
    h = _rms(mem_ref[...], gmem_ref[...]).astype(BF16)
    kv = jnp.dot(h, w_ref[...], preferred_element_type=F32)
    gkm = gkm_ref[...]
    for hd in range(N_HEADS_M):
        k_ref[:, hd * HD_M:(hd + 1) * HD_M] = _rms(kv[:, hd * HD_M:(hd + 1) * HD_M], gkm)
    v_ref[...] = kv[:, D_MEM_ATT:]


def _memkv_call(mem2d, gmem, w_kv, gkm, tm):
    n, d = mem2d.shape
    row = lambda i: (i, 0)
    return pl.pallas_call(
        _memkv_kernel, grid=(n // tm,),
        in_specs=[pl.BlockSpec((tm, d), row), _resident((1, d)),
                  _resident((d, 2 * D_MEM_ATT)), _resident((1, HD_M))],
        out_specs=[pl.BlockSpec((tm, D_MEM_ATT), row), pl.BlockSpec((tm, D_MEM_ATT), row)],
        out_shape=[jax.ShapeDtypeStruct((n, D_MEM_ATT), F32), jax.ShapeDtypeStruct((n, D_MEM_ATT), F32)],
        compiler_params=_cparams(("parallel",)), name="memkv",
    )(mem2d, gmem, w_kv, gkm)


def _merge_kernel(x_ref, a_ref, ub_ref, prev_ref, hist0_ref, qm_ref, mkt_ref, mv_ref,
                  gmix_ref, wg_ref, woa_ref, wob_ref, wom_ref, wout_ref, wpool_ref, spool_ref,
                  o_ref, ext_ref, *, tm, pos0):
    it = pl.program_id(1)
    x = x_ref[...]

    h = _rms(x, gmix_ref[...]).astype(BF16)
    gates = jax.nn.sigmoid(jnp.dot(h, wg_ref[...], preferred_element_type=F32))
    d = x.shape[1]

    ub = ub_ref[...]
    ext_ref[0:HIST_ROWS, :] = jnp.where(it == 0, hist0_ref[...], prev_ref[...])
    ext_ref[HIST_ROWS:HIST_ROWS + tm, :] = ub
    pos = pos0 + it * tm + lax.broadcasted_iota(jnp.int32, (tm, 1), 0)
    ys = []
    for g, w in enumerate(POOL_WINDOWS):
        c0 = g * POOL_GROUP
        win = ub[:, c0:c0 + POOL_GROUP]
        for k in range(1, w):
            win = win + ext_ref[HIST_ROWS - k:HIST_ROWS - k + tm, c0:c0 + POOL_GROUP]
        cnt = jnp.minimum(w, pos + 1).astype(F32)
        pg = (win / cnt - ub[:, c0:c0 + POOL_GROUP]).astype(BF16)
        ys.append(jnp.dot(pg, wpool_ref[g], preferred_element_type=F32))
    bmix = (jnp.concatenate(ys, axis=1) * spool_ref[...]).astype(BF16)

    qm = qm_ref[...]
    ms = []
    for hd in range(N_HEADS_M):
        lg = jnp.dot(qm[:, hd * HD_M:(hd + 1) * HD_M], mkt_ref[hd], preferred_element_type=F32)
        lg = lg * (HD_M ** -0.5)
        e = jnp.exp(lg - jnp.max(lg, axis=1, keepdims=True))
        pr = (e / jnp.sum(e, axis=1, keepdims=True)).astype(BF16)
        ms.append(jnp.dot(pr, mv_ref[hd], preferred_element_type=F32))
    mmix = jnp.concatenate(ms, axis=1).astype(BF16)

    mixed = (gates[:, 0:d] * jnp.dot(a_ref[...], woa_ref[...], preferred_element_type=F32)
             + gates[:, d:2 * d] * jnp.dot(bmix, wob_ref[...], preferred_element_type=F32)
             + gates[:, 2 * d:3 * d] * jnp.dot(mmix, wom_ref[...], preferred_element_type=F32))
    o_ref[...] = x + jnp.dot(mixed.astype(BF16), wout_ref[...], preferred_element_type=F32)


def _merge_call(x, a, ub, hist0, qm, mkt, mv, gmix, wg, woa, wob, wom, wout, wpool, spool, *, tm, pos0):
    b, t, d = x.shape
    nt = t // tm
    hb = tm // HIST_ROWS
    tok = lambda i, j: (i, j, 0)
    per_b3 = lambda i, j: (i, 0, 0)
    per_b4 = lambda i, j: (i, 0, 0, 0)
    in_specs = [
        pl.BlockSpec((None, tm, d), tok),
        pl.BlockSpec((None, tm, D_ATT_A), tok),
        pl.BlockSpec((None, tm, D_POOL), tok),
        pl.BlockSpec((None, HIST_ROWS, D_POOL), lambda i, j: (i, jnp.maximum(j * hb - 1, 0), 0)),
        pl.BlockSpec((None, HIST_ROWS, D_POOL), per_b3),
        pl.BlockSpec((None, tm, D_MEM_ATT), tok),
        pl.BlockSpec((None, N_HEADS_M, HD_M, mkt.shape[3]), per_b4),
        pl.BlockSpec((None, N_HEADS_M, mv.shape[2], HD_M), per_b4),
        _resident((1, d)),
        _resident(wg.shape),
        _resident(woa.shape), _resident(wob.shape), _resident(wom.shape),
        _resident(wout.shape),
        _resident(wpool.shape),
        _resident((1, D_POOL)),
    ]
    return pl.pallas_call(
        functools.partial(_merge_kernel, tm=tm, pos0=pos0), grid=(b, nt), in_specs=in_specs,
        out_specs=pl.BlockSpec((None, tm, d), tok),
        out_shape=jax.ShapeDtypeStruct((b, t, d), F32),
        scratch_shapes=[pltpu.VMEM((HIST_ROWS + tm, D_POOL), F32)],
        compiler_params=_cparams(("parallel", "arbitrary")), name="merge",
    )(x, a, ub, ub, hist0, qm, mkt, mv, gmix, wg, woa, wob, wom, wout, wpool, spool)


def _ffn_kernel(x_ref, g_ref, wgate_ref, wup_ref, wdown_ref, o_ref):
    x = x_ref[...]
    h = _rms(x, g_ref[...]).astype(BF16)
    gate = jnp.dot(h, wgate_ref[...], preferred_element_type=F32)
    up = jnp.dot(h, wup_ref[...], preferred_element_type=F32)
    act = (jax.nn.silu(gate) * up).astype(BF16)
    o_ref[...] = x + jnp.dot(act, wdown_ref[...], preferred_element_type=F32)


def _ffn_call(x2d, g, wgate, wup, wdown, tm):
    n, d = x2d.shape
    row = lambda i: (i, 0)
    return pl.pallas_call(
        _ffn_kernel, grid=(n // tm,),
        in_specs=[pl.BlockSpec((tm, d), row), _resident((1, d)),
                  _resident(wgate.shape), _resident(wup.shape), _resident(wdown.shape)],
        out_specs=pl.BlockSpec((tm, d), row),
        out_shape=jax.ShapeDtypeStruct((n, d), F32),
        compiler_params=_cparams(("parallel",)), name="ffn",
    )(x2d, g, wgate, wup, wdown)


def _rope_tables(pos, head_dim):
    half = head_dim // 2
    inv = ROPE_THETA ** (-jnp.arange(half, dtype=F32) / half)
    ang = pos.astype(F32)[:, None] * inv[None, :]
    reps = LANES // half
    return jnp.tile(jnp.cos(ang), (1, reps)), jnp.tile(jnp.sin(ang), (1, reps))


def _pack_weights(w_in, g_qa, g_ka, g_kidx, g_qm):
    d = w_in.shape[0]
    widths = (D_ATT_A, D_KV_A, D_KV_A, D_QIDX, D_IDX, N_IDX_HEADS, D_POOL, D_MEM_ATT, N_BRANCH * d)
    cuts = [int(c) for c in np.cumsum(widths)[:-1]]
    wqa, wka, wva, wqi, wki, wwi, wub, wqm, wgates = jnp.split(w_in, cuts, axis=1)
    pad = jnp.zeros((d, LANES - D_IDX - N_IDX_HEADS), w_in.dtype)
    w_cat = jnp.concatenate([wqa, wka, wva, wqi, wki, wwi, pad, wub, wqm], axis=1).astype(BF16)
    gqa_t = jnp.tile(g_qa, N_HEADS_A)[None, :]
    gka_t = jnp.tile(g_ka, N_KV_A)[None, :]
    gki_t = jnp.concatenate([g_kidx, jnp.ones((LANES - D_IDX,), g_kidx.dtype)])[None, :]
    return w_cat, wgates.astype(BF16), gqa_t, gka_t, gki_t, g_qm[None, :]


def _block_diag_ones(n, group, limit=None):
    i = np.arange(n)
    m = (i[:, None] // group) == (i[None, :] // group)
    if limit is not None:
        m = m & (i[:, None] < limit) & (i[None, :] < limit)
    return jnp.asarray(m, BF16)


def _key_blocks(k_all, v_all, ki_all, kb):
    b, s = k_all.shape[0], k_all.shape[1]
    nkb = -(-s // kb)
    pad = nkb * kb - s
    if pad:
        k_all = jnp.pad(k_all, ((0, 0), (0, pad), (0, 0), (0, 0)))
        v_all = jnp.pad(v_all, ((0, 0), (0, pad), (0, 0), (0, 0)))
        ki_all = jnp.pad(ki_all, ((0, 0), (0, pad), (0, 0)))
    ones = jnp.ones(v_all.shape[:-1] + (1,), BF16)
    zeros = jnp.zeros(v_all.shape[:-1] + (LANES - HD_A - 1,), BF16)
    kaug = jnp.concatenate([k_all.astype(BF16), ones, zeros], axis=-1)
    k = kaug.reshape(b, nkb, kb, N_KV_A, LANES).transpose(3, 0, 1, 2, 4)
    vaug = jnp.concatenate([v_all.astype(BF16), ones, zeros[..., :V_ROWS - HD_A - 1]], axis=-1)
    vt = vaug.reshape(b, nkb, kb, N_KV_A, V_ROWS).transpose(0, 3, 1, 4, 2)
    ki = ki_all.astype(BF16).reshape(b, nkb, kb, D_IDX)
    return k, vt, ki


def _tile(n, pref):
    t = pref
    while n % t:
        t //= 2
    return t


def _group_forward(x, pos, pos0, hist0, k_hist, v_hist, ki_hist, mkt, mv, wts, *, causal):
    (gmix, w_cat, wgates, gqa_t, gka_t, gki_t, gqm, bd64, bdki, wpool, spool,
     woa, wob, wom, wout, gffn, wgate, wup, wdown) = wts
    b, t, d = x.shape
    n = b * t
    tm = _tile(t, 512)
    cosa, sina = _rope_tables(pos, HD_A)
    cosi, sini = _rope_tables(pos, D_IDX)
    tq = _tile(t, 256)
    proj_args = (x.reshape(n, d), gmix, w_cat, (cosa, sina, cosi, sini), gqa_t, gka_t, gki_t, gqm, bd64, bdki,
                 tm, t // tm)
    if causal:
        ka, va, kw, ub, qm, qat, qit, wit, kaug, vt, kib = _proj_call(*proj_args, dsa_layout=True)
        kb, s = tm, t
        kblk = kaug.reshape(N_KV_A, b, t // kb, kb, LANES)
        kiblk = kib.reshape(b, t // kb, kb, D_IDX)
    else:
        qa, ka, va, qi, kw, ub, qm = _proj_call(*proj_args, dsa_layout=False)
        qat = qa.reshape(b, t, D_ATT_A).transpose(0, 2, 1)
        qit = qi.reshape(b, t, D_QIDX).transpose(0, 2, 1)
        wit = kw[:, D_IDX:D_IDX + N_IDX_HEADS].reshape(b, t, N_IDX_HEADS).transpose(0, 2, 1)
    ka = ka.reshape(b, t, N_KV_A, HD_A)
    va = va.reshape(b, t, N_KV_A, HD_A)
    ki = kw[:, :D_IDX].reshape(b, t, D_IDX)
    ub = ub.reshape(b, t, D_POOL)
    if not causal:
        k_all = jnp.concatenate([k_hist, ka], axis=1)
        s, kb = k_all.shape[1], 256
        kblk, vt, kiblk = _key_blocks(k_all, jnp.concatenate([v_hist, va], axis=1),
                                      jnp.concatenate([ki_hist, ki], axis=1), kb)
    topk = min(TOPK_MAX, s // 4)
    a = _dsa_call(qat, qit, wit, kblk, vt, kiblk, tq=tq, kb=kb, causal=causal, s_valid=s, topk=topk)
    if tq % LANES:
        a = a.transpose(0, 2, 1)

    tmm = _tile(t, 512)
    x2 = _merge_call(x, a, ub, hist0, qm.reshape(b, t, D_MEM_ATT), mkt, mv, gmix, wgates,
                     woa, wob, wom, wout, wpool, spool, tm=tmm, pos0=pos0)
    y = _ffn_call(x2.reshape(n, d), gffn, wgate, wup, wdown, _tile(n, 512)).reshape(b, t, d)
    return y, ka, va, ki, ub


def kernel(x_prompt, x_sample, mem_prompt, cache_a_k, cache_a_v, cache_idx_k, cache_pool, cache_mem_k,
           cache_mem_v, g_mix, w_in, g_qa, g_ka, g_kidx, g_qm, g_mem, w_mem_kv, g_km, w_pool, s_pool,
           w_oa, w_ob, w_om, w_out, g_ffn, w_gate, w_up, w_down):
    depth = w_in.shape[0]
    t = x_prompt.shape[1]
    ts = x_sample.shape[1]
    past = cache_a_k.shape[2]
    b = x_prompt.shape[0]
    n_mem = mem_prompt.shape[1]
    pos_p = jnp.arange(t, dtype=jnp.int32)
    pos_s = past + jnp.arange(ts, dtype=jnp.int32)
    bd64 = _block_diag_ones(D_ATT_A, HD_A)
    bdki = _block_diag_ones(LANES, LANES, limit=D_IDX)

    xp, xs = x_prompt, x_sample
    outs = [[] for _ in range(10)]
    for l in range(depth):
        w_cat, wgates, gqa_t, gka_t, gki_t, gqm = _pack_weights(w_in[l], g_qa[l], g_ka[l], g_kidx[l], g_qm[l])
        wts = (g_mix[l][None, :], w_cat, wgates, gqa_t, gka_t, gki_t, gqm, bd64, bdki,
               w_pool[l].astype(BF16), s_pool[l][None, :],
               w_oa[l].astype(BF16), w_ob[l].astype(BF16), w_om[l].astype(BF16), w_out[l].astype(BF16),
               g_ffn[l][None, :], w_gate[l].astype(BF16), w_up[l].astype(BF16), w_down[l].astype(BF16))

        mk, mv = _memkv_call(mem_prompt.reshape(b * n_mem, -1), g_mem[l][None, :],
                             w_mem_kv[l].astype(BF16), g_km[l][None, :], _tile(b * n_mem, 256))
        mk = mk.reshape(b, n_mem, N_HEADS_M, HD_M)
        mv = mv.reshape(b, n_mem, N_HEADS_M, HD_M)
        hist0 = jnp.zeros((b, HIST_ROWS, D_POOL), F32)
        xp, ka, va, ki, ub = _group_forward(
            xp, pos_p, 0, hist0, None, None, None,
            mk.astype(BF16).transpose(0, 2, 3, 1), mv.astype(BF16).transpose(0, 2, 1, 3), wts, causal=True)
        for lst, val in zip(outs[:6], (ka, va, ki, ub[:, -POOL_HIST:], mk, mv)):
            lst.append(val)

        bs = xs.shape[0]
        hist0 = jnp.concatenate([jnp.zeros((bs, HIST_ROWS - POOL_HIST, D_POOL), F32), cache_pool[l]], axis=1)
        xs, ka, va, ki, ub = _group_forward(
            xs, pos_s, past, hist0, cache_a_k[l], cache_a_v[l], cache_idx_k[l],
            cache_mem_k[l].astype(BF16).transpose(0, 2, 3, 1), cache_mem_v[l].astype(BF16).transpose(0, 2, 1, 3),
            wts, causal=False)
        pool_s = jnp.concatenate([cache_pool[l], ub], axis=1)[:, -POOL_HIST:]
        for lst, val in zip(outs[6:], (ka, va, ki, pool_s)):
            lst.append(val)

    stacked = [jnp.stack(o) for o in outs]
    return (xp, xs, *stacked)
```

```python
import functools

import jax
import jax.numpy as jnp
import numpy as np
from jax import lax
from jax.experimental import pallas as pl
from jax.experimental.pallas import tpu as pltpu

F32 = jnp.float32
BF16 = jnp.bfloat16

CHUNK = 64
EPS = 1e-6
ROPE_THETA = 10000.0
N_HEADS_A = 8
N_KV_A = 2
GROUP_A = N_HEADS_A // N_KV_A
HD_A = 64
D_ATT_A = N_HEADS_A * HD_A
D_KV_A = N_KV_A * HD_A
N_IDX_HEADS = 8
D_IDX = 32
D_QIDX = N_IDX_HEADS * D_IDX
IDX_SCALE = (N_IDX_HEADS * D_IDX) ** -0.5
TOPK_MAX = 256
POOL_WINDOWS = (2, 4, 8, 16)
POOL_GROUP = 128
D_POOL = len(POOL_WINDOWS) * POOL_GROUP
POOL_HIST = 15
N_HEADS_M = 4
HD_M = 128
D_MEM_ATT = N_HEADS_M * HD_M
N_BRANCH = 3

LANES = 128
SUBLANES = 8
V7X_VMEM_BYTES = 64 * 1024 * 1024
VMEM_LIMIT = 56 * 1024 * 1024

HIST_ROWS = 2 * SUBLANES
MASK_VALUE = -1e30
FLT_MAX = float(np.finfo(np.float32).max)
KEY_BITS = 32
Q_SCALE = HD_A ** -0.5 * float(np.log2(np.e))
SUM_FLOOR = 1e-30
COUNT_ROWS = 4 * SUBLANES
COUNT_CHAINS = 2
V_ROWS = HD_A + 2 * SUBLANES

C_QA = 0
C_KA = C_QA + D_ATT_A
C_VA = C_KA + D_KV_A
C_QI = C_VA + D_KV_A
C_KW = C_QI + D_QIDX
C_UB = C_KW + LANES
C_QM = C_UB + D_POOL
C_END = C_QM + D_MEM_ATT


def _cparams(sem):
    return pltpu.CompilerParams(dimension_semantics=sem, vmem_limit_bytes=VMEM_LIMIT)


def _resident(shape):
    zeros = (0,) * len(shape)
    return pl.BlockSpec(shape, lambda *_: zeros, pipeline_mode=pl.Buffered(1))


def _rms(x, g):
    ms = jnp.mean(x * x, axis=-1, keepdims=True)
    return x * lax.rsqrt(ms + EPS) * g


def _group_sumsq(x, bd):
    sq = x * x
    hi = sq.astype(BF16)
    lo = (sq - hi.astype(F32)).astype(BF16)
    return (jnp.dot(hi, bd, preferred_element_type=F32)
            + jnp.dot(lo, bd, preferred_element_type=F32))


def _rope_lanes(x, cos, sin, half):
    lane = lax.broadcasted_iota(jnp.int32, x.shape, 1)
    first = (lane % (2 * half)) < half
    left = pltpu.roll(x, LANES - half, 1)
    right = pltpu.roll(x, half, 1)
    rot = jnp.where(first, -left, right)
    return x * cos + rot * sin


def _proj_kernel(x_ref, gmix_ref, w_ref, cosa_ref, sina_ref, cosi_ref, sini_ref,
                 gqa_ref, gka_ref, gki_ref, gqm_ref, bd64_ref, bdki_ref, *out_refs, dsa_layout):
    if dsa_layout:
        ka_ref, va_ref, kw_ref, ub_ref, qm_ref, qat_ref, qit_ref, wit_ref, kaug_ref, vt_ref, kib_ref = out_refs
    else:
        qa_ref, ka_ref, va_ref, qi_ref, kw_ref, ub_ref, qm_ref = out_refs
    x = x_ref[...]
    tm = x.shape[0]
    h = _rms(x, gmix_ref[...]).astype(BF16)
    p = jnp.dot(h, w_ref[...], preferred_element_type=F32)
    cosa, sina = cosa_ref[...], sina_ref[...]
    cosi, sini = cosi_ref[...], sini_ref[...]
    bd64 = bd64_ref[...]

    qa = p[:, C_QA:C_QA + D_ATT_A]
    ss = _group_sumsq(qa, bd64)
    qa = qa * lax.rsqrt(ss * (1.0 / HD_A) + EPS) * gqa_ref[...]
    for c in range(D_ATT_A // LANES):
        blk = _rope_lanes(qa[:, c * LANES:(c + 1) * LANES], cosa, sina, HD_A // 2) * Q_SCALE
        if dsa_layout:
            qat_ref[c * LANES:(c + 1) * LANES, :] = blk.T.astype(BF16)
        else:
            qa_ref[:, c * LANES:(c + 1) * LANES] = blk.astype(BF16)

    ka = p[:, C_KA:C_KA + D_KV_A]
    ss = _group_sumsq(ka, bd64[:D_KV_A, :D_KV_A])
    ka = ka * lax.rsqrt(ss * (1.0 / HD_A) + EPS) * gka_ref[...]
    ka = _rope_lanes(ka, cosa, sina, HD_A // 2)
    va = p[:, C_VA:C_VA + D_KV_A]
    ka_ref[...] = ka
    va_ref[...] = va
    if dsa_layout:
        lane = lax.broadcasted_iota(jnp.int32, ka.shape, 1)
        tail = jnp.where(lane == HD_A, 1.0, 0.0)
        kaug_ref[0] = jnp.where(lane < HD_A, ka, tail).astype(BF16)
        kaug_ref[1] = jnp.where(lane < HD_A, pltpu.roll(ka, HD_A, 1), tail).astype(BF16)
        vat = va.T
        srow = lax.broadcasted_iota(jnp.int32, (V_ROWS - HD_A, tm), 0)
        ones_rows = jnp.where(srow == 0, 1.0, 0.0).astype(BF16)
        for g in range(N_KV_A):
            vt_ref[g, 0:HD_A, :] = vat[g * HD_A:(g + 1) * HD_A, :].astype(BF16)
            vt_ref[g, HD_A:V_ROWS, :] = ones_rows

    for c in range(D_QIDX // LANES):
        blk = _rope_lanes(p[:, C_QI + c * LANES:C_QI + (c + 1) * LANES], cosi, sini, D_IDX // 2)
        if dsa_layout:
            qit_ref[c * LANES:(c + 1) * LANES, :] = blk.T.astype(BF16)
        else:
            qi_ref[:, c * LANES:(c + 1) * LANES] = blk.astype(BF16)

    kw = p[:, C_KW:C_KW + LANES]
    ss = _group_sumsq(kw, bdki_ref[...])
    kin = kw * lax.rsqrt(ss * (1.0 / D_IDX) + EPS) * gki_ref[...]
    kin = _rope_lanes(kin, cosi, sini, D_IDX // 2)
    lane = lax.broadcasted_iota(jnp.int32, kw.shape, 1)
    kw = jnp.where(lane < D_IDX, kin, kw * IDX_SCALE)
    kw_ref[...] = kw
    if dsa_layout:
        kib_ref[...] = kw[:, :D_IDX].astype(BF16)
        wit_ref[...] = kw.T[D_IDX:D_IDX + N_IDX_HEADS, :]

    ub_ref[...] = p[:, C_UB:C_UB + D_POOL]

    gqm = gqm_ref[...]
    for hd in range(N_HEADS_M):
        blk = p[:, C_QM + hd * HD_M:C_QM + (hd + 1) * HD_M]
        qm_ref[:, hd * HD_M:(hd + 1) * HD_M] = _rms(blk, gqm).astype(BF16)


def _proj_call(x2d, gmix, w_cat, tabs, gqa_t, gka_t, gki_t, gqm, bd64, bdki, tm, n_tab_blocks, dsa_layout):
    n = x2d.shape[0]
    d = x2d.shape[1]
    nb = n // (tm * n_tab_blocks)
    grid = (n // tm,)
    row = lambda i: (i, 0)
    tab = lambda i: (i % n_tab_blocks, 0)
    in_specs = [
        pl.BlockSpec((tm, d), row),
        _resident((1, d)),
        _resident((d, C_END)),
        pl.BlockSpec((tm, LANES), tab), pl.BlockSpec((tm, LANES), tab),
        pl.BlockSpec((tm, LANES), tab), pl.BlockSpec((tm, LANES), tab),
        _resident((1, D_ATT_A)), _resident((1, D_KV_A)),
        _resident((1, LANES)), _resident((1, HD_M)),
        _resident((D_ATT_A, D_ATT_A)), _resident((LANES, LANES)),
    ]
    rows = lambda width, dtype: (jax.ShapeDtypeStruct((n, width), dtype), pl.BlockSpec((tm, width), row))
    per_token = [rows(D_KV_A, F32), rows(D_KV_A, F32), rows(LANES, F32), rows(D_POOL, F32),
                 rows(D_MEM_ATT, BF16)]
    if dsa_layout:
        t = tm * n_tab_blocks
        cols = lambda i: (i // n_tab_blocks, 0, i % n_tab_blocks)
        tposed = lambda r, dtype: (jax.ShapeDtypeStruct((nb, r, t), dtype), pl.BlockSpec((None, r, tm), cols))
        outs = per_token + [
            tposed(D_ATT_A, BF16), tposed(D_QIDX, BF16), tposed(N_IDX_HEADS, F32),
            (jax.ShapeDtypeStruct((N_KV_A, n, LANES), BF16), pl.BlockSpec((N_KV_A, tm, LANES), lambda i: (0, i, 0))),
            (jax.ShapeDtypeStruct((nb, N_KV_A, n_tab_blocks, V_ROWS, tm), BF16),
             pl.BlockSpec((None, N_KV_A, None, V_ROWS, tm),
                          lambda i: (i // n_tab_blocks, 0, i % n_tab_blocks, 0, 0))),
            rows(D_IDX, BF16),
        ]
    else:
        outs = [rows(D_ATT_A, BF16)] + per_token[:2] + [rows(D_QIDX, BF16)] + per_token[2:]
    return pl.pallas_call(
        functools.partial(_proj_kernel, dsa_layout=dsa_layout), grid=grid, in_specs=in_specs,
        out_specs=[o[1] for o in outs], out_shape=[o[0] for o in outs],
        compiler_params=_cparams(("parallel",)), name="proj",
    )(x2d, gmix, w_cat, *tabs, gqa_t, gka_t, gki_t, gqm, bd64, bdki)


def _key_bits_to_float(u):
    k = u ^ jnp.int32(-2147483648)
    bits = jnp.where(k >= 0, k, k ^ jnp.int32(0x7FFFFFFF))
    return lax.bitcast_convert_type(bits, F32)


def _dsa_kernel(qat_ref, qit_ref, wit_ref, k_ref, vt_ref, ki_ref, o_ref,
                sc_ref, m_ref, acc_ref, qaug_ref, qis_ref, km_ref,
                *, tq, kb, nkb_total, causal, s_valid, topk, rows_out):
    qblk = pl.program_id(1)
    if causal:
        nkb = lax.div(qblk + kb // tq, jnp.int32(kb // tq))
    else:
        nkb = nkb_total
    qpos = qblk * tq + lax.broadcasted_iota(jnp.int32, (1, tq), 1)
    if causal:
        lim = (lax.shift_right_logical(qpos, 6) + 1) * CHUNK
    else:
        lim = jnp.full((1, tq), s_valid, jnp.int32)
    limf = lim.astype(F32)
    krow = lax.broadcasted_iota(jnp.int32, (kb, tq), 0)
    kf = float(topk)

    def for_each_block_pairwise(fn):
        n = jnp.int32(nkb)

        def step(jj, carry):
            fn(2 * jj)
            fn(2 * jj + 1)
            return carry
        lax.fori_loop(0, lax.div(n, 2), step, 0)

        @pl.when(lax.rem(n, 2) == 1)
        def _():
            fn(n - 1)

    for h in range(N_IDX_HEADS):
        qis_ref[:, h * tq:(h + 1) * tq] = qit_ref[h * D_IDX:(h + 1) * D_IDX, :]
    qit = qis_ref[...]
    wit = wit_ref[...]

    def score_block(j):
        z = jnp.dot(ki_ref[j], qit, preferred_element_type=F32)
        s = jnp.zeros((kb, tq), F32)
        for h in range(N_IDX_HEADS):
            s = s + wit[h:h + 1, :] * jnp.maximum(z[:, h * tq:(h + 1) * tq], 0.0)
        sc_ref[j] = jnp.where(j * kb + krow < lim, s, -jnp.inf)

    for_each_block_pairwise(score_block)

    crow = lax.broadcasted_iota(jnp.int32, (COUNT_ROWS, tq), 0)

    def count(pred):
        def body(j, accs):
            accs = list(accs)
            for ci, r in enumerate(range(0, kb, COUNT_ROWS)):
                hit = pred(sc_ref[j, r:r + COUNT_ROWS, :], j * kb + r + crow)
                accs[ci % COUNT_CHAINS] = accs[ci % COUNT_CHAINS] + jnp.where(hit, 1.0, 0.0)
            return tuple(accs)
        zero = jnp.zeros((COUNT_ROWS, tq), F32)
        accs = lax.fori_loop(0, nkb, body, (zero,) * COUNT_CHAINS)
        return jnp.sum(sum(accs[1:], accs[0]), axis=0, keepdims=True)

    def radix_body(i, carry):
        prefix, cge, cgt = carry
        cand = prefix | lax.shift_left(jnp.int32(1), KEY_BITS - 1 - i)
        t = _key_bits_to_float(cand)
        c = count(lambda s, idx: s >= t)
        take = c >= kf
        return jnp.where(take, cand, prefix), jnp.where(take, c, cge), jnp.where(take, cgt, c)

    prefix, cge, cgt = lax.fori_loop(
        0, KEY_BITS, radix_body, (jnp.zeros((1, tq), jnp.int32), limf, jnp.zeros((1, tq), F32)))
    thr = jnp.where(limf >= kf, _key_bits_to_float(prefix), -FLT_MAX)

    any_tie = jnp.max(jnp.where(cge > kf, 1.0, 0.0), axis=1, keepdims=True)

    def mask_plain():
        def body(j, carry):
            sc_ref[j] = jnp.where(sc_ref[j] >= thr, 0.0, MASK_VALUE)
            return carry
        lax.fori_loop(0, nkb, body, 0)

    def mask_ties():
        need = kf - cgt
        tri = (lax.broadcasted_iota(jnp.int32, (kb, kb), 1)
               <= lax.broadcasted_iota(jnp.int32, (kb, kb), 0)).astype(BF16)

        def body(j, seen):
            s = sc_ref[j]
            eq = s == thr
            rank = jnp.dot(tri, jnp.where(eq, 1.0, 0.0).astype(BF16), preferred_element_type=F32) + seen
            keep = (s > thr) | (eq & (rank <= need))
            sc_ref[j] = jnp.where(keep, 0.0, MASK_VALUE)
            return rank[kb - 1:kb, :]
        lax.fori_loop(0, nkb, body, jnp.zeros((1, tq), F32))

    lax.cond(any_tie[0, 0] > 0.0, mask_ties, mask_plain)

    @pl.when(qblk == 0)
    def _():
        for g in range(N_KV_A):
            def norm_body(j, mx, g=g):
                kk = k_ref[g, j].astype(F32)
                return jnp.maximum(mx, jnp.sum(kk * kk, axis=1, keepdims=True))
            mx = lax.fori_loop(0, nkb_total, norm_body, jnp.zeros((kb, 1), F32))
            km_ref[g] = jnp.broadcast_to(jnp.max(mx, axis=0, keepdims=True), (SUBLANES, LANES))

    aug_row = lax.broadcasted_iota(jnp.int32, (LANES - HD_A, tq), 0)
    for g in range(N_KV_A):
        kmax2 = km_ref[g][0:1, :]
        kmax2 = jnp.concatenate([kmax2] * (tq // LANES), axis=1) if tq >= LANES else kmax2[:, :tq]
        for u in range(GROUP_A):
            hd = g * GROUP_A + u
            q = qat_ref[hd * HD_A:(hd + 1) * HD_A, :]
            qf = q.astype(F32)
            shift = jnp.sqrt(jnp.sum(qf * qf, axis=0, keepdims=True) * kmax2)
            qaug_ref[g, 0:HD_A, u * tq:(u + 1) * tq] = q
            qaug_ref[g, HD_A:LANES, u * tq:(u + 1) * tq] = jnp.where(aug_row == 0, -shift, 0.0).astype(BF16)

    acc_ref[...] = jnp.zeros(acc_ref.shape, F32)

    def attend_block(j):
        bias = sc_ref[j]
        for g in range(N_KV_A):
            lg = jnp.dot(k_ref[g, j], qaug_ref[g], preferred_element_type=F32)
            vt = vt_ref[g, j]
            for u in range(GROUP_A):
                hd = g * GROUP_A + u
                p = jnp.exp2(lg[:, u * tq:(u + 1) * tq] + bias).astype(BF16)
                acc_ref[hd] += jnp.dot(vt, p, preferred_element_type=F32)

    for_each_block_pairwise(attend_block)

    sums_ok = jnp.ones((1, tq), F32)
    for hd in range(N_HEADS_A):
        sums_ok = jnp.where(acc_ref[hd][HD_A:HD_A + 1, :] > SUM_FLOOR, sums_ok, 0.0)
    all_ok = jnp.min(sums_ok, axis=1, keepdims=True)

    @pl.when(all_ok[0, 0] < 1.0)
    def _():
        m_ref[...] = jnp.full(m_ref.shape, MASK_VALUE, F32)
        acc_ref[...] = jnp.zeros(acc_ref.shape, F32)

        def online_body(j, carry):
            bias = sc_ref[j]
            for g in range(N_KV_A):
                lg = jnp.dot(k_ref[g, j], qaug_ref[g], preferred_element_type=F32)
                vt = vt_ref[g, j]
                for u in range(GROUP_A):
                    hd = g * GROUP_A + u
                    l = lg[:, u * tq:(u + 1) * tq] + bias
                    m_old = m_ref[hd:hd + 1, :]
                    m_new = jnp.maximum(m_old, jnp.max(l, axis=0, keepdims=True))
                    p = jnp.exp2(l - m_new).astype(BF16)
                    pv = jnp.dot(vt, p, preferred_element_type=F32)
                    acc_ref[hd] = jnp.exp2(m_old - m_new) * acc_ref[hd] + pv
                    m_ref[hd:hd + 1, :] = m_new
            return carry

        lax.fori_loop(0, nkb, online_body, 0)

    heads = []
    for hd in range(N_HEADS_A):
        a = acc_ref[hd]
        heads.append(a[:HD_A, :] / a[HD_A:HD_A + 1, :])
    out_t = jnp.concatenate(heads, axis=0)
    o_ref[...] = (out_t.T if rows_out else out_t).astype(BF16)


def _dsa_call(qat, qit, wit, k, vt, ki, *, tq, kb, causal, s_valid, topk):
    b, t = qat.shape[0], qat.shape[2]
    nkb_total = k.shape[2]
    rows_out = tq % LANES == 0
    kern = functools.partial(_dsa_kernel, tq=tq, kb=kb, nkb_total=nkb_total, causal=causal,
                             s_valid=s_valid, topk=topk, rows_out=rows_out)
    qcols = lambda i, j: (i, 0, j)
    in_specs = [
        pl.BlockSpec((None, D_ATT_A, tq), qcols),
        pl.BlockSpec((None, D_QIDX, tq), qcols),
        pl.BlockSpec((None, N_IDX_HEADS, tq), qcols),
        pl.BlockSpec((N_KV_A, None, nkb_total, kb, LANES), lambda i, j: (0, i, 0, 0, 0)),
        pl.BlockSpec((None, N_KV_A, nkb_total, V_ROWS, kb), lambda i, j: (i, 0, 0, 0, 0)),
        pl.BlockSpec((None, nkb_total, kb, D_IDX), lambda i, j: (i, 0, 0, 0)),
    ]
    if rows_out:
        out_spec = pl.BlockSpec((None, tq, D_ATT_A), lambda i, j: (i, j, 0))
        out_shape = jax.ShapeDtypeStruct((b, t, D_ATT_A), BF16)
    else:
        out_spec = pl.BlockSpec((None, D_ATT_A, tq), qcols)
        out_shape = jax.ShapeDtypeStruct((b, D_ATT_A, t), BF16)
    return pl.pallas_call(
        kern, grid=(b, t // tq), in_specs=in_specs, out_specs=out_spec, out_shape=out_shape,
        scratch_shapes=[pltpu.VMEM((nkb_total, kb, tq), F32),
                        pltpu.VMEM((N_HEADS_A, tq), F32),
                        pltpu.VMEM((N_HEADS_A, V_ROWS, tq), F32),
                        pltpu.VMEM((N_KV_A, LANES, GROUP_A * tq), BF16),
                        pltpu.VMEM((D_IDX, N_IDX_HEADS * tq), BF16),
                        pltpu.VMEM((N_KV_A, SUBLANES, LANES), F32)],
        compiler_params=_cparams(("parallel", "arbitrary")), name="dsa",
    )(qat, qit, wit, k, vt, ki)


def _memkv_kernel(mem_ref, gmem_ref, w_ref, gkm_ref, k_ref, v_ref):
    h = _rms(mem_ref[...], gmem_ref[...]).astype(BF16)
    kv = jnp.dot(h, w_ref[...], preferred_element_type=F32)
    gkm = gkm_ref[...]
    for hd in range(N_HEADS_M):
        k_ref[:, hd * HD_M:(hd + 1) * HD_M] = _rms(kv[:, hd * HD_M:(hd + 1) * HD_M], gkm)
    v_ref[...] = kv[:, D_MEM_ATT:]


def _memkv_call(mem2d, gmem, w_kv, gkm, tm):
    n, d = mem2d.shape
    row = lambda i: (i, 0)
    return pl.pallas_call(
        _memkv_kernel, grid=(n // tm,),
        in_specs=[pl.BlockSpec((tm, d), row), _resident((1, d)),
                  _resident((d, 2 * D_MEM_ATT)), _resident((1, HD_M))],
        out_specs=[pl.BlockSpec((tm, D_MEM_ATT), row), pl.BlockSpec((tm, D_MEM_ATT), row)],
        out_shape=[jax.ShapeDtypeStruct((n, D_MEM_ATT), F32), jax.ShapeDtypeStruct((n, D_MEM_ATT), F32)],
        compiler_params=_cparams(("parallel",)), name="memkv",
    )(mem2d, gmem, w_kv, gkm)


def _merge_kernel(x_ref, a_ref, ub_ref, prev_ref, hist0_ref, qm_ref, mkt_ref, mv_ref,
                  gmix_ref, wg_ref, woa_ref, wob_ref, wom_ref, wout_ref, wpool_ref, spool_ref,
                  o_ref, ext_ref, *, tm, pos0):
    it = pl.program_id(1)
    x = x_ref[...]

    h = _rms(x, gmix_ref[...]).astype(BF16)
    gates = jax.nn.sigmoid(jnp.dot(h, wg_ref[...], preferred_element_type=F32))
    d = x.shape[1]

    ub = ub_ref[...]
    ext_ref[0:HIST_ROWS, :] = jnp.where(it == 0, hist0_ref[...], prev_ref[...])
    ext_ref[HIST_ROWS:HIST_ROWS + tm, :] = ub
    pos = pos0 + it * tm + lax.broadcasted_iota(jnp.int32, (tm, 1), 0)
    ys = []
    for g, w in enumerate(POOL_WINDOWS):
        c0 = g * POOL_GROUP
        win = ub[:, c0:c0 + POOL_GROUP]
        for k in range(1, w):
            win = win + ext_ref[HIST_ROWS - k:HIST_ROWS - k + tm, c0:c0 + POOL_GROUP]
        cnt = jnp.minimum(w, pos + 1).astype(F32)
        pg = (win / cnt - ub[:, c0:c0 + POOL_GROUP]).astype(BF16)
        ys.append(jnp.dot(pg, wpool_ref[g], preferred_element_type=F32))
    bmix = (jnp.concatenate(ys, axis=1) * spool_ref[...]).astype(BF16)

    qm = qm_ref[...]
    ms = []
    for hd in range(N_HEADS_M):
        lg = jnp.dot(qm[:, hd * HD_M:(hd + 1) * HD_M], mkt_ref[hd], preferred_element_type=F32)
        lg = lg * (HD_M ** -0.5)
        e = jnp.exp(lg - jnp.max(lg, axis=1, keepdims=True))
        pr = (e / jnp.sum(e, axis=1, keepdims=True)).astype(BF16)
        ms.append(jnp.dot(pr, mv_ref[hd], preferred_element_type=F32))
    mmix = jnp.concatenate(ms, axis=1).astype(BF16)

    mixed = (gates[:, 0:d] * jnp.dot(a_ref[...], woa_ref[...], preferred_element_type=F32)
             + gates[:, d:2 * d] * jnp.dot(bmix, wob_ref[...], preferred_element_type=F32)
             + gates[:, 2 * d:3 * d] * jnp.dot(mmix, wom_ref[...], preferred_element_type=F32))
    o_ref[...] = x + jnp.dot(mixed.astype(BF16), wout_ref[...], preferred_element_type=F32)


def _merge_call(x, a, ub, hist0, qm, mkt, mv, gmix, wg, woa, wob, wom, wout, wpool, spool, *, tm, pos0):
    b, t, d = x.shape
    nt = t // tm
    hb = tm // HIST_ROWS
    tok = lambda i, j: (i, j, 0)
    per_b3 = lambda i, j: (i, 0, 0)
    per_b4 = lambda i, j: (i, 0, 0, 0)
    in_specs = [
        pl.BlockSpec((None, tm, d), tok),
        pl.BlockSpec((None, tm, D_ATT_A), tok),
        pl.BlockSpec((None, tm, D_POOL), tok),
        pl.BlockSpec((None, HIST_ROWS, D_POOL), lambda i, j: (i, jnp.maximum(j * hb - 1, 0), 0)),
        pl.BlockSpec((None, HIST_ROWS, D_POOL), per_b3),
        pl.BlockSpec((None, tm, D_MEM_ATT), tok),
        pl.BlockSpec((None, N_HEADS_M, HD_M, mkt.shape[3]), per_b4),
        pl.BlockSpec((None, N_HEADS_M, mv.shape[2], HD_M), per_b4),
        _resident((1, d)),
        _resident(wg.shape),
        _resident(woa.shape), _resident(wob.shape), _resident(wom.shape),
        _resident(wout.shape),
        _resident(wpool.shape),
        _resident((1, D_POOL)),
    ]
    return pl.pallas_call(
        functools.partial(_merge_kernel, tm=tm, pos0=pos0), grid=(b, nt), in_specs=in_specs,
        out_specs=pl.BlockSpec((None, tm, d), tok),
        out_shape=jax.ShapeDtypeStruct((b, t, d), F32),
        scratch_shapes=[pltpu.VMEM((HIST_ROWS + tm, D_POOL), F32)],
        compiler_params=_cparams(("parallel", "arbitrary")), name="merge",
    )(x, a, ub, ub, hist0, qm, mkt, mv, gmix, wg, woa, wob, wom, wout, wpool, spool)


def _ffn_kernel(x_ref, g_ref, wgate_ref, wup_ref, wdown_ref, o_ref):
    x = x_ref[...]
    h = _rms(x, g_ref[...]).astype(BF16)
    gate = jnp.dot(h, wgate_ref[...], preferred_element_type=F32)
    up = jnp.dot(h, wup_ref[...], preferred_element_type=F32)
    act = (jax.nn.silu(gate) * up).astype(BF16)
    o_ref[...] = x + jnp.dot(act, wdown_ref[...], preferred_element_type=F32)


def _ffn_call(x2d, g, wgate, wup, wdown, tm):
    n, d = x2d.shape
    row = lambda i: (i, 0)
    return pl.pallas_call(
        _ffn_kernel, grid=(n // tm,),
        in_specs=[pl.BlockSpec((tm, d), row), _resident((1, d)),
                  _resident(wgate.shape), _resident(wup.shape), _resident(wdown.shape)],
        out_specs=pl.BlockSpec((tm, d), row),
        out_shape=jax.ShapeDtypeStruct((n, d), F32),
        compiler_params=_cparams(("parallel",)), name="ffn",
    )(x2d, g, wgate, wup, wdown)


def _rope_tables(pos, head_dim):
    half = head_dim // 2
    inv = ROPE_THETA ** (-jnp.arange(half, dtype=F32) / half)
    ang = pos.astype(F32)[:, None] * inv[None, :]
    reps = LANES // half
    return jnp.tile(jnp.cos(ang), (1, reps)), jnp.tile(jnp.sin(ang), (1, reps))


def _pack_weights(w_in, g_qa, g_ka, g_kidx, g_qm):
    d = w_in.shape[0]
    widths = (D_ATT_A, D_KV_A, D_KV_A, D_QIDX, D_IDX, N_IDX_HEADS, D_POOL, D_MEM_ATT, N_BRANCH * d)
    cuts = [int(c) for c in np.cumsum(widths)[:-1]]
    wqa, wka, wva, wqi, wki, wwi, wub, wqm, wgates = jnp.split(w_in, cuts, axis=1)
    pad = jnp.zeros((d, LANES - D_IDX - N_IDX_HEADS), w_in.dtype)
    w_cat = jnp.concatenate([wqa, wka, wva, wqi, wki, wwi, pad, wub, wqm], axis=1).astype(BF16)
    gqa_t = jnp.tile(g_qa, N_HEADS_A)[None, :]
    gka_t = jnp.tile(g_ka, N_KV_A)[None, :]
    gki_t = jnp.concatenate([g_kidx, jnp.ones((LANES - D_IDX,), g_kidx.dtype)])[None, :]
    return w_cat, wgates.astype(BF16), gqa_t, gka_t, gki_t, g_qm[None, :]


def _block_diag_ones(n, group, limit=None):
    i = np.arange(n)
    m = (i[:, None] // group) == (i[None, :] // group)
    if limit is not None:
        m = m & (i[:, None] < limit) & (i[None, :] < limit)
    return jnp.asarray(m, BF16)


def _key_blocks(k_all, v_all, ki_all, kb):
    b, s = k_all.shape[0], k_all.shape[1]
    nkb = -(-s // kb)
    pad = nkb * kb - s
    if pad:
        k_all = jnp.pad(k_all, ((0, 0), (0, pad), (0, 0), (0, 0)))
        v_all = jnp.pad(v_all, ((0, 0), (0, pad), (0, 0), (0, 0)))
        ki_all = jnp.pad(ki_all, ((0, 0), (0, pad), (0, 0)))
    ones = jnp.ones(v_all.shape[:-1] + (1,), BF16)
    zeros = jnp.zeros(v_all.shape[:-1] + (LANES - HD_A - 1,), BF16)
    kaug = jnp.concatenate([k_all.astype(BF16), ones, zeros], axis=-1)
    k = kaug.reshape(b, nkb, kb, N_KV_A, LANES).transpose(3, 0, 1, 2, 4)
    vaug = jnp.concatenate([v_all.astype(BF16), ones, zeros[..., :V_ROWS - HD_A - 1]], axis=-1)
    vt = vaug.reshape(b, nkb, kb, N_KV_A, V_ROWS).transpose(0, 3, 1, 4, 2)
    ki = ki_all.astype(BF16).reshape(b, nkb, kb, D_IDX)
    return k, vt, ki


def _tile(n, pref):
    t = pref
    while n % t:
        t //= 2
    return t


def _group_forward(x, pos, pos0, hist0, k_hist, v_hist, ki_hist, mkt, mv, wts, *, causal):
    (gmix, w_cat, wgates, gqa_t, gka_t, gki_t, gqm, bd64, bdki, wpool, spool,
     woa, wob, wom, wout, gffn, wgate, wup, wdown) = wts
    b, t, d = x.shape
    n = b * t
    tm = _tile(t, 512)
    cosa, sina = _rope_tables(pos, HD_A)
    cosi, sini = _rope_tables(pos, D_IDX)
    tq = _tile(t, 256)
    proj_args = (x.reshape(n, d), gmix, w_cat, (cosa, sina, cosi, sini), gqa_t, gka_t, gki_t, gqm, bd64, bdki,
                 tm, t // tm)
    if causal:
        ka, va, kw, ub, qm, qat, qit, wit, kaug, vt, kib = _proj_call(*proj_args, dsa_layout=True)
        kb, s = tm, t
        kblk = kaug.reshape(N_KV_A, b, t // kb, kb, LANES)
        kiblk = kib.reshape(b, t // kb, kb, D_IDX)
    else:
        qa, ka, va, qi, kw, ub, qm = _proj_call(*proj_args, dsa_layout=False)
        qat = qa.reshape(b, t, D_ATT_A).transpose(0, 2, 1)
        qit = qi.reshape(b, t, D_QIDX).transpose(0, 2, 1)
        wit = kw[:, D_IDX:D_IDX + N_IDX_HEADS].reshape(b, t, N_IDX_HEADS).transpose(0, 2, 1)
    ka = ka.reshape(b, t, N_KV_A, HD_A)
    va = va.reshape(b, t, N_KV_A, HD_A)
    ki = kw[:, :D_IDX].reshape(b, t, D_IDX)
    ub = ub.reshape(b, t, D_POOL)
    if not causal:
        k_all = jnp.concatenate([k_hist, ka], axis=1)
        s, kb = k_all.shape[1], 256
        kblk, vt, kiblk = _key_blocks(k_all, jnp.concatenate([v_hist, va], axis=1),
                                      jnp.concatenate([ki_hist, ki], axis=1), kb)
    topk = min(TOPK_MAX, s // 4)
    a = _dsa_call(qat, qit, wit, kblk, vt, kiblk, tq=tq, kb=kb, causal=causal, s_valid=s, topk=topk)
    if tq % LANES:
        a = a.transpose(0, 2, 1)

    tmm = _tile(t, 512)
    x2 = _merge_call(x, a, ub, hist0, qm.reshape(b, t, D_MEM_ATT), mkt, mv, gmix, wgates,
                     woa, wob, wom, wout, wpool, spool, tm=tmm, pos0=pos0)
    y = _ffn_call(x2.reshape(n, d), gffn, wgate, wup, wdown, _tile(n, 512)).reshape(b, t, d)
    return y, ka, va, ki, ub


def kernel(x_prompt, x_sample, mem_prompt, cache_a_k, cache_a_v, cache_idx_k, cache_pool, cache_mem_k,
           cache_mem_v, g_mix, w_in, g_qa, g_ka, g_kidx, g_qm, g_mem, w_mem_kv, g_km, w_pool, s_pool,
           w_oa, w_ob, w_om, w_out, g_ffn, w_gate, w_up, w_down):
    depth = w_in.shape[0]
    t = x_prompt.shape[1]
    ts = x_sample.shape[1]
    past = cache_a_k.shape[2]
    b = x_prompt.shape[0]
    n_mem = mem_prompt.shape[1]
    pos_p = jnp.arange(t, dtype=jnp.int32)
    pos_s = past + jnp.arange(ts, dtype=jnp.int32)
    bd64 = _block_diag_ones(D_ATT_A, HD_A)
    bdki = _block_diag_ones(LANES, LANES, limit=D_IDX)

    xp, xs = x_prompt, x_sample
    outs = [[] for _ in range(10)]
    for l in range(depth):
        w_cat, wgates, gqa_t, gka_t, gki_t, gqm = _pack_weights(w_in[l], g_qa[l], g_ka[l], g_kidx[l], g_qm[l])
        wts = (g_mix[l][None, :], w_cat, wgates, gqa_t, gka_t, gki_t, gqm, bd64, bdki,
               w_pool[l].astype(BF16), s_pool[l][None, :],
               w_oa[l].astype(BF16), w_ob[l].astype(BF16), w_om[l].astype(BF16), w_out[l].astype(BF16),
               g_ffn[l][None, :], w_gate[l].astype(BF16), w_up[l].astype(BF16), w_down[l].astype(BF16))

        mk, mv = _memkv_call(mem_prompt.reshape(b * n_mem, -1), g_mem[l][None, :],
                             w_mem_kv[l].astype(BF16), g_km[l][None, :], _tile(b * n_mem, 256))
        mk = mk.reshape(b, n_mem, N_HEADS_M, HD_M)
        mv = mv.reshape(b, n_mem, N_HEADS_M, HD_M)
        hist0 = jnp.zeros((b, HIST_ROWS, D_POOL), F32)
        xp, ka, va, ki, ub = _group_forward(
            xp, pos_p, 0, hist0, None, None, None,
            mk.astype(BF16).transpose(0, 2, 3, 1), mv.astype(BF16).transpose(0, 2, 1, 3), wts, causal=True)
        for lst, val in zip(outs[:6], (ka, va, ki, ub[:, -POOL_HIST:], mk, mv)):
            lst.append(val)

        bs = xs.shape[0]
        hist0 = jnp.concatenate([jnp.zeros((bs, HIST_ROWS - POOL_HIST, D_POOL), F32), cache_pool[l]], axis=1)
        xs, ka, va, ki, ub = _group_forward(
            xs, pos_s, past, hist0, cache_a_k[l], cache_a_v[l], cache_idx_k[l],
            cache_mem_k[l].astype(BF16).transpose(0, 2, 3, 1), cache_mem_v[l].astype(BF16).transpose(0, 2, 1, 3),
            wts, causal=False)
        pool_s = jnp.concatenate([cache_pool[l], ub], axis=1)[:, -POOL_HIST:]
        for lst, val in zip(outs[6:], (ka, va, ki, pool_s)):
            lst.append(val)

    stacked = [jnp.stack(o) for o in outs]
    return (xp, xs, *stacked)
```

```python
import functools

import jax
import jax.numpy as jnp
import numpy as np
from jax import lax
from jax.experimental import pallas as pl
from jax.experimental.pallas import tpu as pltpu

F32 = jnp.float32
BF16 = jnp.bfloat16

CHUNK = 64
EPS = 1e-6
ROPE_THETA = 10000.0
N_HEADS_A = 8
N_KV_A = 2
GROUP_A = N_HEADS_A // N_KV_A
HD_A = 64
D_ATT_A = N_HEADS_A * HD_A
D_KV_A = N_KV_A * HD_A
N_IDX_HEADS = 8
D_IDX = 32
D_QIDX = N_IDX_HEADS * D_IDX
IDX_SCALE = (N_IDX_HEADS * D_IDX) ** -0.5
TOPK_MAX = 256
POOL_WINDOWS = (2, 4, 8, 16)
POOL_GROUP = 128
D_POOL = len(POOL_WINDOWS) * POOL_GROUP
POOL_HIST = 15
N_HEADS_M = 4
HD_M = 128
D_MEM_ATT = N_HEADS_M * HD_M
N_BRANCH = 3

LANES = 128
SUBLANES = 8
V7X_VMEM_BYTES = 64 * 1024 * 1024
VMEM_LIMIT = 56 * 1024 * 1024

HIST_ROWS = 2 * SUBLANES
MASK_VALUE = -1e30
FLT_MAX = float(np.finfo(np.float32).max)
KEY_BITS = 32
Q_SCALE = HD_A ** -0.5 * float(np.log2(np.e))
SUM_FLOOR = 1e-30
COUNT_ROWS = 4 * SUBLANES
COUNT_CHAINS = 2
V_ROWS = HD_A + 2 * SUBLANES

C_QA = 0
C_KA = C_QA + D_ATT_A
C_VA = C_KA + D_KV_A
C_QI = C_VA + D_KV_A
C_KW = C_QI + D_QIDX
C_UB = C_KW + LANES
C_QM = C_UB + D_POOL
C_END = C_QM + D_MEM_ATT


def _cparams(sem):
    return pltpu.CompilerParams(dimension_semantics=sem, vmem_limit_bytes=VMEM_LIMIT)


def _resident(shape):
    zeros = (0,) * len(shape)
    return pl.BlockSpec(shape, lambda *_: zeros, pipeline_mode=pl.Buffered(1))


def _rms(x, g):
    ms = jnp.mean(x * x, axis=-1, keepdims=True)
    return x * lax.rsqrt(ms + EPS) * g


def _group_sumsq(x, bd):
    sq = x * x
    hi = sq.astype(BF16)
    lo = (sq - hi.astype(F32)).astype(BF16)
    return (jnp.dot(hi, bd, preferred_element_type=F32)
            + jnp.dot(lo, bd, preferred_element_type=F32))


def _rope_lanes(x, cos, sin, half):
    lane = lax.broadcasted_iota(jnp.int32, x.shape, 1)
    first = (lane % (2 * half)) < half
    left = pltpu.roll(x, LANES - half, 1)
    right = pltpu.roll(x, half, 1)
    rot = jnp.where(first, -left, right)
    return x * cos + rot * sin


def _proj_kernel(x_ref, gmix_ref, w_ref, cosa_ref, sina_ref, cosi_ref, sini_ref,
                 gqa_ref, gka_ref, gki_ref, gqm_ref, bd64_ref, bdki_ref, *out_refs, dsa_layout):
    if dsa_layout:
        ka_ref, va_ref, kw_ref, ub_ref, qm_ref, qat_ref, qit_ref, wit_ref, kaug_ref, vt_ref, kib_ref = out_refs
    else:
        qa_ref, ka_ref, va_ref, qi_ref, kw_ref, ub_ref, qm_ref = out_refs
    x = x_ref[...]
    tm = x.shape[0]
    h = _rms(x, gmix_ref[...]).astype(BF16)
    p = jnp.dot(h, w_ref[...], preferred_element_type=F32)
    cosa, sina = cosa_ref[...], sina_ref[...]
    cosi, sini = cosi_ref[...], sini_ref[...]
    bd64 = bd64_ref[...]

    qa = p[:, C_QA:C_QA + D_ATT_A]
    ss = _group_sumsq(qa, bd64)
    qa = qa * lax.rsqrt(ss * (1.0 / HD_A) + EPS) * gqa_ref[...]
    for c in range(D_ATT_A // LANES):
        blk = _rope_lanes(qa[:, c * LANES:(c + 1) * LANES], cosa, sina, HD_A // 2) * Q_SCALE
        if dsa_layout:
            qat_ref[c * LANES:(c + 1) * LANES, :] = blk.T.astype(BF16)
        else:
            qa_ref[:, c * LANES:(c + 1) * LANES] = blk.astype(BF16)

    ka = p[:, C_KA:C_KA + D_KV_A]
    ss = _group_sumsq(ka, bd64[:D_KV_A, :D_KV_A])
    ka = ka * lax.rsqrt(ss * (1.0 / HD_A) + EPS) * gka_ref[...]
    ka = _rope_lanes(ka, cosa, sina, HD_A // 2)
    va = p[:, C_VA:C_VA + D_KV_A]
    for g in range(N_KV_A):
        ka_ref[:, g, :] = (ka if g == 0 else pltpu.roll(ka, LANES - g * HD_A, 1))[:, :HD_A]
        va_ref[:, g, :] = (va if g == 0 else pltpu.roll(va, LANES - g * HD_A, 1))[:, :HD_A]
    if dsa_layout:
        lane = lax.broadcasted_iota(jnp.int32, ka.shape, 1)
        tail = jnp.where(lane == HD_A, 1.0, 0.0)
        kaug_ref[0] = jnp.where(lane < HD_A, ka, tail).astype(BF16)
        kaug_ref[1] = jnp.where(lane < HD_A, pltpu.roll(ka, HD_A, 1), tail).astype(BF16)
        vat = va.T
        srow = lax.broadcasted_iota(jnp.int32, (V_ROWS - HD_A, tm), 0)
        ones_rows = jnp.where(srow == 0, 1.0, 0.0).astype(BF16)
        for g in range(N_KV_A):
            vt_ref[g, 0:HD_A, :] = vat[g * HD_A:(g + 1) * HD_A, :].astype(BF16)
            vt_ref[g, HD_A:V_ROWS, :] = ones_rows

    for c in range(D_QIDX // LANES):
        blk = _rope_lanes(p[:, C_QI + c * LANES:C_QI + (c + 1) * LANES], cosi, sini, D_IDX // 2)
        if dsa_layout:
            qit_ref[c * LANES:(c + 1) * LANES, :] = blk.T.astype(BF16)
        else:
            qi_ref[:, c * LANES:(c + 1) * LANES] = blk.astype(BF16)

    kw = p[:, C_KW:C_KW + LANES]
    ss = _group_sumsq(kw, bdki_ref[...])
    kin = kw * lax.rsqrt(ss * (1.0 / D_IDX) + EPS) * gki_ref[...]
    kin = _rope_lanes(kin, cosi, sini, D_IDX // 2)
    lane = lax.broadcasted_iota(jnp.int32, kw.shape, 1)
    kw = jnp.where(lane < D_IDX, kin, kw * IDX_SCALE)
    kw_ref[...] = kw
    if dsa_layout:
        kib_ref[...] = kw[:, :D_IDX].astype(BF16)
        wit_ref[...] = kw.T[D_IDX:D_IDX + N_IDX_HEADS, :]

    ub_ref[...] = p[:, C_UB:C_UB + D_POOL]

    gqm = gqm_ref[...]
    for hd in range(N_HEADS_M):
        blk = p[:, C_QM + hd * HD_M:C_QM + (hd + 1) * HD_M]
        qm_ref[:, hd * HD_M:(hd + 1) * HD_M] = _rms(blk, gqm).astype(BF16)


def _proj_call(x2d, gmix, w_cat, tabs, gqa_t, gka_t, gki_t, gqm, bd64, bdki, tm, n_tab_blocks, dsa_layout):
    n = x2d.shape[0]
    d = x2d.shape[1]
    nb = n // (tm * n_tab_blocks)
    grid = (n // tm,)
    row = lambda i: (i, 0)
    tab = lambda i: (i % n_tab_blocks, 0)
    in_specs = [
        pl.BlockSpec((tm, d), row),
        _resident((1, d)),
        _resident((d, C_END)),
        pl.BlockSpec((tm, LANES), tab), pl.BlockSpec((tm, LANES), tab),
        pl.BlockSpec((tm, LANES), tab), pl.BlockSpec((tm, LANES), tab),
        _resident((1, D_ATT_A)), _resident((1, D_KV_A)),
        _resident((1, LANES)), _resident((1, HD_M)),
        _resident((D_ATT_A, D_ATT_A)), _resident((LANES, LANES)),
    ]
    rows = lambda width, dtype: (jax.ShapeDtypeStruct((n, width), dtype), pl.BlockSpec((tm, width), row))
    heads = (jax.ShapeDtypeStruct((n, N_KV_A, HD_A), F32), pl.BlockSpec((tm, N_KV_A, HD_A), lambda i: (i, 0, 0)))
    per_token = [heads, heads, rows(LANES, F32), rows(D_POOL, F32),
                 rows(D_MEM_ATT, BF16)]
    if dsa_layout:
        t = tm * n_tab_blocks
        cols = lambda i: (i // n_tab_blocks, 0, i % n_tab_blocks)
        tposed = lambda r, dtype: (jax.ShapeDtypeStruct((nb, r, t), dtype), pl.BlockSpec((None, r, tm), cols))
        outs = per_token + [
            tposed(D_ATT_A, BF16), tposed(D_QIDX, BF16), tposed(N_IDX_HEADS, F32),
            (jax.ShapeDtypeStruct((N_KV_A, n, LANES), BF16), pl.BlockSpec((N_KV_A, tm, LANES), lambda i: (0, i, 0))),
            (jax.ShapeDtypeStruct((nb, N_KV_A, n_tab_blocks, V_ROWS, tm), BF16),
             pl.BlockSpec((None, N_KV_A, None, V_ROWS, tm),
                          lambda i: (i // n_tab_blocks, 0, i % n_tab_blocks, 0, 0))),
            rows(D_IDX, BF16),
        ]
    else:
        outs = [rows(D_ATT_A, BF16)] + per_token[:2] + [rows(D_QIDX, BF16)] + per_token[2:]
    return pl.pallas_call(
        functools.partial(_proj_kernel, dsa_layout=dsa_layout), grid=grid, in_specs=in_specs,
        out_specs=[o[1] for o in outs], out_shape=[o[0] for o in outs],
        compiler_params=_cparams(("parallel",)), name="proj",
    )(x2d, gmix, w_cat, *tabs, gqa_t, gka_t, gki_t, gqm, bd64, bdki)


def _key_bits_to_float(u):
    k = u ^ jnp.int32(-2147483648)
    bits = jnp.where(k >= 0, k, k ^ jnp.int32(0x7FFFFFFF))
    return lax.bitcast_convert_type(bits, F32)


def _dsa_kernel(qat_ref, qit_ref, wit_ref, k_ref, vt_ref, ki_ref, o_ref,
                sc_ref, m_ref, acc_ref, qaug_ref, qis_ref, km_ref,
                *, tq, kb, nkb_total, causal, s_valid, topk, rows_out):
    qblk = pl.program_id(1)
    if causal:
        nkb = lax.div(qblk + kb // tq, jnp.int32(kb // tq))
    else:
        nkb = nkb_total
    qpos = qblk * tq + lax.broadcasted_iota(jnp.int32, (1, tq), 1)
    if causal:
        lim = (lax.shift_right_logical(qpos, 6) + 1) * CHUNK
    else:
        lim = jnp.full((1, tq), s_valid, jnp.int32)
    limf = lim.astype(F32)
    krow = lax.broadcasted_iota(jnp.int32, (kb, tq), 0)
    kf = float(topk)

    def for_each_block_pairwise(fn):
        n = jnp.int32(nkb)

        def step(jj, carry):
            fn(2 * jj)
            fn(2 * jj + 1)
            return carry
        lax.fori_loop(0, lax.div(n, 2), step, 0)

        @pl.when(lax.rem(n, 2) == 1)
        def _():
            fn(n - 1)

    for h in range(N_IDX_HEADS):
        qis_ref[:, h * tq:(h + 1) * tq] = qit_ref[h * D_IDX:(h + 1) * D_IDX, :]
    qit = qis_ref[...]
    wit = wit_ref[...]

    def score_block(j):
        z = jnp.dot(ki_ref[j], qit, preferred_element_type=F32)
        s = jnp.zeros((kb, tq), F32)
        for h in range(N_IDX_HEADS):
            s = s + wit[h:h + 1, :] * jnp.maximum(z[:, h * tq:(h + 1) * tq], 0.0)
        sc_ref[j] = jnp.where(j * kb + krow < lim, s, -jnp.inf)

    for_each_block_pairwise(score_block)

    crow = lax.broadcasted_iota(jnp.int32, (COUNT_ROWS, tq), 0)

    def count(pred):
        def body(j, accs):
            accs = list(accs)
            for ci, r in enumerate(range(0, kb, COUNT_ROWS)):
                hit = pred(sc_ref[j, r:r + COUNT_ROWS, :], j * kb + r + crow)
                accs[ci % COUNT_CHAINS] = accs[ci % COUNT_CHAINS] + jnp.where(hit, 1.0, 0.0)
            return tuple(accs)
        zero = jnp.zeros((COUNT_ROWS, tq), F32)
        accs = lax.fori_loop(0, nkb, body, (zero,) * COUNT_CHAINS)
        return jnp.sum(sum(accs[1:], accs[0]), axis=0, keepdims=True)

    def radix_body(i, carry):
        prefix, cge, cgt = carry
        cand = prefix | lax.shift_left(jnp.int32(1), KEY_BITS - 1 - i)
        t = _key_bits_to_float(cand)
        c = count(lambda s, idx: s >= t)
        take = c >= kf
        return jnp.where(take, cand, prefix), jnp.where(take, c, cge), jnp.where(take, cgt, c)

    prefix, cge, cgt = lax.fori_loop(
        0, KEY_BITS, radix_body, (jnp.zeros((1, tq), jnp.int32), limf, jnp.zeros((1, tq), F32)))
    thr = jnp.where(limf >= kf, _key_bits_to_float(prefix), -FLT_MAX)

    any_tie = jnp.max(jnp.where(cge > kf, 1.0, 0.0), axis=1, keepdims=True)

    def mask_plain():
        def body(j, carry):
            sc_ref[j] = jnp.where(sc_ref[j] >= thr, 0.0, MASK_VALUE)
            return carry
        lax.fori_loop(0, nkb, body, 0)

    def mask_ties():
        need = kf - cgt
        tri = (lax.broadcasted_iota(jnp.int32, (kb, kb), 1)
               <= lax.broadcasted_iota(jnp.int32, (kb, kb), 0)).astype(BF16)

        def body(j, seen):
            s = sc_ref[j]
            eq = s == thr
            rank = jnp.dot(tri, jnp.where(eq, 1.0, 0.0).astype(BF16), preferred_element_type=F32) + seen
            keep = (s > thr) | (eq & (rank <= need))
            sc_ref[j] = jnp.where(keep, 0.0, MASK_VALUE)
            return rank[kb - 1:kb, :]
        lax.fori_loop(0, nkb, body, jnp.zeros((1, tq), F32))

    lax.cond(any_tie[0, 0] > 0.0, mask_ties, mask_plain)

    @pl.when(qblk == 0)
    def _():
        for g in range(N_KV_A):
            def norm_body(j, mx, g=g):
                kk = k_ref[g, j].astype(F32)
                return jnp.maximum(mx, jnp.sum(kk * kk, axis=1, keepdims=True))
            mx = lax.fori_loop(0, nkb_total, norm_body, jnp.zeros((kb, 1), F32))
            km_ref[g] = jnp.broadcast_to(jnp.max(mx, axis=0, keepdims=True), (SUBLANES, LANES))

    aug_row = lax.broadcasted_iota(jnp.int32, (LANES - HD_A, tq), 0)
    for g in range(N_KV_A):
        kmax2 = km_ref[g][0:1, :]
        kmax2 = jnp.concatenate([kmax2] * (tq // LANES), axis=1) if tq >= LANES else kmax2[:, :tq]
        for u in range(GROUP_A):
            hd = g * GROUP_A + u
            q = qat_ref[hd * HD_A:(hd + 1) * HD_A, :]
            qf = q.astype(F32)
            shift = jnp.sqrt(jnp.sum(qf * qf, axis=0, keepdims=True) * kmax2)
            qaug_ref[g, 0:HD_A, u * tq:(u + 1) * tq] = q
            qaug_ref[g, HD_A:LANES, u * tq:(u + 1) * tq] = jnp.where(aug_row == 0, -shift, 0.0).astype(BF16)

    acc_ref[...] = jnp.zeros(acc_ref.shape, F32)

    def attend_block(j):
        bias = sc_ref[j]
        for g in range(N_KV_A):
            lg = jnp.dot(k_ref[g, j], qaug_ref[g], preferred_element_type=F32)
            vt = vt_ref[g, j]
            for u in range(GROUP_A):
                hd = g * GROUP_A + u
                p = jnp.exp2(lg[:, u * tq:(u + 1) * tq] + bias).astype(BF16)
                acc_ref[hd] += jnp.dot(vt, p, preferred_element_type=F32)

    for_each_block_pairwise(attend_block)

    sums_ok = jnp.ones((1, tq), F32)
    for hd in range(N_HEADS_A):
        sums_ok = jnp.where(acc_ref[hd][HD_A:HD_A + 1, :] > SUM_FLOOR, sums_ok, 0.0)
    all_ok = jnp.min(sums_ok, axis=1, keepdims=True)

    @pl.when(all_ok[0, 0] < 1.0)
    def _():
        m_ref[...] = jnp.full(m_ref.shape, MASK_VALUE, F32)
        acc_ref[...] = jnp.zeros(acc_ref.shape, F32)

        def online_body(j, carry):
            bias = sc_ref[j]
            for g in range(N_KV_A):
                lg = jnp.dot(k_ref[g, j], qaug_ref[g], preferred_element_type=F32)
                vt = vt_ref[g, j]
                for u in range(GROUP_A):
                    hd = g * GROUP_A + u
                    l = lg[:, u * tq:(u + 1) * tq] + bias
                    m_old = m_ref[hd:hd + 1, :]
                    m_new = jnp.maximum(m_old, jnp.max(l, axis=0, keepdims=True))
                    p = jnp.exp2(l - m_new).astype(BF16)
                    pv = jnp.dot(vt, p, preferred_element_type=F32)
                    acc_ref[hd] = jnp.exp2(m_old - m_new) * acc_ref[hd] + pv
                    m_ref[hd:hd + 1, :] = m_new
            return carry

        lax.fori_loop(0, nkb, online_body, 0)

    heads = []
    for hd in range(N_HEADS_A):
        a = acc_ref[hd]
        heads.append(a[:HD_A, :] / a[HD_A:HD_A + 1, :])
    out_t = jnp.concatenate(heads, axis=0)
    o_ref[...] = (out_t.T if rows_out else out_t).astype(BF16)


def _dsa_call(qat, qit, wit, k, vt, ki, *, tq, kb, causal, s_valid, topk):
    b, t = qat.shape[0], qat.shape[2]
    nkb_total = k.shape[2]
    rows_out = tq % LANES == 0
    kern = functools.partial(_dsa_kernel, tq=tq, kb=kb, nkb_total=nkb_total, causal=causal,
                             s_valid=s_valid, topk=topk, rows_out=rows_out)
    qcols = lambda i, j: (i, 0, j)
    in_specs = [
        pl.BlockSpec((None, D_ATT_A, tq), qcols),
        pl.BlockSpec((None, D_QIDX, tq), qcols),
        pl.BlockSpec((None, N_IDX_HEADS, tq), qcols),
        pl.BlockSpec((N_KV_A, None, nkb_total, kb, LANES), lambda i, j: (0, i, 0, 0, 0)),
        pl.BlockSpec((None, N_KV_A, nkb_total, V_ROWS, kb), lambda i, j: (i, 0, 0, 0, 0)),
        pl.BlockSpec((None, nkb_total, kb, D_IDX), lambda i, j: (i, 0, 0, 0)),
    ]
    if rows_out:
        out_spec = pl.BlockSpec((None, tq, D_ATT_A), lambda i, j: (i, j, 0))
        out_shape = jax.ShapeDtypeStruct((b, t, D_ATT_A), BF16)
    else:
        out_spec = pl.BlockSpec((None, D_ATT_A, tq), qcols)
        out_shape = jax.ShapeDtypeStruct((b, D_ATT_A, t), BF16)
    return pl.pallas_call(
        kern, grid=(b, t // tq), in_specs=in_specs, out_specs=out_spec, out_shape=out_shape,
        scratch_shapes=[pltpu.VMEM((nkb_total, kb, tq), F32),
                        pltpu.VMEM((N_HEADS_A, tq), F32),
                        pltpu.VMEM((N_HEADS_A, V_ROWS, tq), F32),
                        pltpu.VMEM((N_KV_A, LANES, GROUP_A * tq), BF16),
                        pltpu.VMEM((D_IDX, N_IDX_HEADS * tq), BF16),
                        pltpu.VMEM((N_KV_A, SUBLANES, LANES), F32)],
        compiler_params=_cparams(("parallel", "arbitrary")), name="dsa",
    )(qat, qit, wit, k, vt, ki)


def _memkv_kernel(mem_ref, gmem_ref, w_ref, gkm_ref, k_ref, v_ref):
    h = _rms(mem_ref[...], gmem_ref[...]).astype(BF16)
    kv = jnp.dot(h, w_ref[...], preferred_element_type=F32)
    gkm = gkm_ref[...]
    for hd in range(N_HEADS_M):
        k_ref[:, hd * HD_M:(hd + 1) * HD_M] = _rms(kv[:, hd * HD_M:(hd + 1) * HD_M], gkm)
    v_ref[...] = kv[:, D_MEM_ATT:]


def _memkv_call(mem2d, gmem, w_kv, gkm, tm):
    n, d = mem2d.shape
    row = lambda i: (i, 0)
    return pl.pallas_call(
        _memkv_kernel, grid=(n // tm,),
        in_specs=[pl.BlockSpec((tm, d), row), _resident((1, d)),
                  _resident((d, 2 * D_MEM_ATT)), _resident((1, HD_M))],
        out_specs=[pl.BlockSpec((tm, D_MEM_ATT), row), pl.BlockSpec((tm, D_MEM_ATT), row)],
        out_shape=[jax.ShapeDtypeStruct((n, D_MEM_ATT), F32), jax.ShapeDtypeStruct((n, D_MEM_ATT), F32)],
        compiler_params=_cparams(("parallel",)), name="memkv",
    )(mem2d, gmem, w_kv, gkm)


def _merge_kernel(x_ref, a_ref, ub_ref, prev_ref, hist0_ref, qm_ref, mkt_ref, mv_ref,
                  gmix_ref, wg_ref, woa_ref, wob_ref, wom_ref, wout_ref, wpool_ref, spool_ref,
                  o_ref, ext_ref, *, tm, pos0):
    it = pl.program_id(1)
    x = x_ref[...]

    h = _rms(x, gmix_ref[...]).astype(BF16)
    gates = jax.nn.sigmoid(jnp.dot(h, wg_ref[...], preferred_element_type=F32))
    d = x.shape[1]

    ub = ub_ref[...]
    ext_ref[0:HIST_ROWS, :] = jnp.where(it == 0, hist0_ref[...], prev_ref[...])
    ext_ref[HIST_ROWS:HIST_ROWS + tm, :] = ub
    pos = pos0 + it * tm + lax.broadcasted_iota(jnp.int32, (tm, 1), 0)
    ys = []
    for g, w in enumerate(POOL_WINDOWS):
        c0 = g * POOL_GROUP
        win = ub[:, c0:c0 + POOL_GROUP]
        for k in range(1, w):
            win = win + ext_ref[HIST_ROWS - k:HIST_ROWS - k + tm, c0:c0 + POOL_GROUP]
        cnt = jnp.minimum(w, pos + 1).astype(F32)
        pg = (win / cnt - ub[:, c0:c0 + POOL_GROUP]).astype(BF16)
        ys.append(jnp.dot(pg, wpool_ref[g], preferred_element_type=F32))
    bmix = (jnp.concatenate(ys, axis=1) * spool_ref[...]).astype(BF16)

    qm = qm_ref[...]
    ms = []
    for hd in range(N_HEADS_M):
        lg = jnp.dot(qm[:, hd * HD_M:(hd + 1) * HD_M], mkt_ref[hd], preferred_element_type=F32)
        lg = lg * (HD_M ** -0.5)
        e = jnp.exp(lg - jnp.max(lg, axis=1, keepdims=True))
        pr = (e / jnp.sum(e, axis=1, keepdims=True)).astype(BF16)
        ms.append(jnp.dot(pr, mv_ref[hd], preferred_element_type=F32))
    mmix = jnp.concatenate(ms, axis=1).astype(BF16)

    mixed = (gates[:, 0:d] * jnp.dot(a_ref[...], woa_ref[...], preferred_element_type=F32)
             + gates[:, d:2 * d] * jnp.dot(bmix, wob_ref[...], preferred_element_type=F32)
             + gates[:, 2 * d:3 * d] * jnp.dot(mmix, wom_ref[...], preferred_element_type=F32))
    o_ref[...] = x + jnp.dot(mixed.astype(BF16), wout_ref[...], preferred_element_type=F32)


def _merge_call(x, a, ub, hist0, qm, mkt, mv, gmix, wg, woa, wob, wom, wout, wpool, spool, *, tm, pos0):
    b, t, d = x.shape
    nt = t // tm
    hb = tm // HIST_ROWS
    tok = lambda i, j: (i, j, 0)
    per_b3 = lambda i, j: (i, 0, 0)
    per_b4 = lambda i, j: (i, 0, 0, 0)
    in_specs = [
        pl.BlockSpec((None, tm, d), tok),
        pl.BlockSpec((None, tm, D_ATT_A), tok),
        pl.BlockSpec((None, tm, D_POOL), tok),
        pl.BlockSpec((None, HIST_ROWS, D_POOL), lambda i, j: (i, jnp.maximum(j * hb - 1, 0), 0)),
        pl.BlockSpec((None, HIST_ROWS, D_POOL), per_b3),
        pl.BlockSpec((None, tm, D_MEM_ATT), tok),
        pl.BlockSpec((None, N_HEADS_M, HD_M, mkt.shape[3]), per_b4),
        pl.BlockSpec((None, N_HEADS_M, mv.shape[2], HD_M), per_b4),
        _resident((1, d)),
        _resident(wg.shape),
        _resident(woa.shape), _resident(wob.shape), _resident(wom.shape),
        _resident(wout.shape),
        _resident(wpool.shape),
        _resident((1, D_POOL)),
    ]
    return pl.pallas_call(
        functools.partial(_merge_kernel, tm=tm, pos0=pos0), grid=(b, nt), in_specs=in_specs,
        out_specs=pl.BlockSpec((None, tm, d), tok),
        out_shape=jax.ShapeDtypeStruct((b, t, d), F32),
        scratch_shapes=[pltpu.VMEM((HIST_ROWS + tm, D_POOL), F32)],
        compiler_params=_cparams(("parallel", "arbitrary")), name="merge",
    )(x, a, ub, ub, hist0, qm, mkt, mv, gmix, wg, woa, wob, wom, wout, wpool, spool)


def _ffn_kernel(x_ref, g_ref, wgate_ref, wup_ref, wdown_ref, o_ref):
    x = x_ref[...]
    h = _rms(x, g_ref[...]).astype(BF16)
    gate = jnp.dot(h, wgate_ref[...], preferred_element_type=F32)
    up = jnp.dot(h, wup_ref[...], preferred_element_type=F32)
    act = (jax.nn.silu(gate) * up).astype(BF16)
    o_ref[...] = x + jnp.dot(act, wdown_ref[...], preferred_element_type=F32)


def _ffn_call(x2d, g, wgate, wup, wdown, tm):
    n, d = x2d.shape
    row = lambda i: (i, 0)
    return pl.pallas_call(
        _ffn_kernel, grid=(n // tm,),
        in_specs=[pl.BlockSpec((tm, d), row), _resident((1, d)),
                  _resident(wgate.shape), _resident(wup.shape), _resident(wdown.shape)],
        out_specs=pl.BlockSpec((tm, d), row),
        out_shape=jax.ShapeDtypeStruct((n, d), F32),
        compiler_params=_cparams(("parallel",)), name="ffn",
    )(x2d, g, wgate, wup, wdown)


def _rope_tables(pos, head_dim):
    half = head_dim // 2
    inv = ROPE_THETA ** (-jnp.arange(half, dtype=F32) / half)
    ang = pos.astype(F32)[:, None] * inv[None, :]
    reps = LANES // half
    return jnp.tile(jnp.cos(ang), (1, reps)), jnp.tile(jnp.sin(ang), (1, reps))


def _pack_weights(w_in, g_qa, g_ka, g_kidx, g_qm):
    d = w_in.shape[0]
    widths = (D_ATT_A, D_KV_A, D_KV_A, D_QIDX, D_IDX, N_IDX_HEADS, D_POOL, D_MEM_ATT, N_BRANCH * d)
    cuts = [int(c) for c in np.cumsum(widths)[:-1]]
    wqa, wka, wva, wqi, wki, wwi, wub, wqm, wgates = jnp.split(w_in, cuts, axis=1)
    pad = jnp.zeros((d, LANES - D_IDX - N_IDX_HEADS), w_in.dtype)
    w_cat = jnp.concatenate([wqa, wka, wva, wqi, wki, wwi, pad, wub, wqm], axis=1).astype(BF16)
    gqa_t = jnp.tile(g_qa, N_HEADS_A)[None, :]
    gka_t = jnp.tile(g_ka, N_KV_A)[None, :]
    gki_t = jnp.concatenate([g_kidx, jnp.ones((LANES - D_IDX,), g_kidx.dtype)])[None, :]
    return w_cat, wgates.astype(BF16), gqa_t, gka_t, gki_t, g_qm[None, :]


def _block_diag_ones(n, group, limit=None):
    i = np.arange(n)
    m = (i[:, None] // group) == (i[None, :] // group)
    if limit is not None:
        m = m & (i[:, None] < limit) & (i[None, :] < limit)
    return jnp.asarray(m, BF16)


def _key_blocks(k_all, v_all, ki_all, kb):
    b, s = k_all.shape[0], k_all.shape[1]
    nkb = -(-s // kb)
    pad = nkb * kb - s
    if pad:
        k_all = jnp.pad(k_all, ((0, 0), (0, pad), (0, 0), (0, 0)))
        v_all = jnp.pad(v_all, ((0, 0), (0, pad), (0, 0), (0, 0)))
        ki_all = jnp.pad(ki_all, ((0, 0), (0, pad), (0, 0)))
    ones = jnp.ones(v_all.shape[:-1] + (1,), BF16)
    zeros = jnp.zeros(v_all.shape[:-1] + (LANES - HD_A - 1,), BF16)
    kaug = jnp.concatenate([k_all.astype(BF16), ones, zeros], axis=-1)
    k = kaug.reshape(b, nkb, kb, N_KV_A, LANES).transpose(3, 0, 1, 2, 4)
    vaug = jnp.concatenate([v_all.astype(BF16), ones, zeros[..., :V_ROWS - HD_A - 1]], axis=-1)
    vt = vaug.reshape(b, nkb, kb, N_KV_A, V_ROWS).transpose(0, 3, 1, 4, 2)
    ki = ki_all.astype(BF16).reshape(b, nkb, kb, D_IDX)
    return k, vt, ki


def _tile(n, pref):
    t = pref
    while n % t:
        t //= 2
    return t


def _group_forward(x, pos, pos0, hist0, k_hist, v_hist, ki_hist, mkt, mv, wts, *, causal):
    (gmix, w_cat, wgates, gqa_t, gka_t, gki_t, gqm, bd64, bdki, wpool, spool,
     woa, wob, wom, wout, gffn, wgate, wup, wdown) = wts
    b, t, d = x.shape
    n = b * t
    tm = _tile(t, 512)
    cosa, sina = _rope_tables(pos, HD_A)
    cosi, sini = _rope_tables(pos, D_IDX)
    tq = _tile(t, 256)
    proj_args = (x.reshape(n, d), gmix, w_cat, (cosa, sina, cosi, sini), gqa_t, gka_t, gki_t, gqm, bd64, bdki,
                 tm, t // tm)
    if causal:
        ka, va, kw, ub, qm, qat, qit, wit, kaug, vt, kib = _proj_call(*proj_args, dsa_layout=True)
        kb, s = tm, t
        kblk = kaug.reshape(N_KV_A, b, t // kb, kb, LANES)
        kiblk = kib.reshape(b, t // kb, kb, D_IDX)
    else:
        qa, ka, va, qi, kw, ub, qm = _proj_call(*proj_args, dsa_layout=False)
        qat = qa.reshape(b, t, D_ATT_A).transpose(0, 2, 1)
        qit = qi.reshape(b, t, D_QIDX).transpose(0, 2, 1)
        wit = kw[:, D_IDX:D_IDX + N_IDX_HEADS].reshape(b, t, N_IDX_HEADS).transpose(0, 2, 1)
    ka = ka.reshape(b, t, N_KV_A, HD_A)
    va = va.reshape(b, t, N_KV_A, HD_A)
    ki = kw[:, :D_IDX].reshape(b, t, D_IDX)
    ub = ub.reshape(b, t, D_POOL)
    if not causal:
        k_all = jnp.concatenate([k_hist, ka], axis=1)
        s, kb = k_all.shape[1], 256
        kblk, vt, kiblk = _key_blocks(k_all, jnp.concatenate([v_hist, va], axis=1),
                                      jnp.concatenate([ki_hist, ki], axis=1), kb)
    topk = min(TOPK_MAX, s // 4)
    a = _dsa_call(qat, qit, wit, kblk, vt, kiblk, tq=tq, kb=kb, causal=causal, s_valid=s, topk=topk)
    if tq % LANES:
        a = a.transpose(0, 2, 1)

    tmm = _tile(t, 512)
    x2 = _merge_call(x, a, ub, hist0, qm.reshape(b, t, D_MEM_ATT), mkt, mv, gmix, wgates,
                     woa, wob, wom, wout, wpool, spool, tm=tmm, pos0=pos0)
    y = _ffn_call(x2.reshape(n, d), gffn, wgate, wup, wdown, _tile(n, 512)).reshape(b, t, d)
    return y, ka, va, ki, ub


def kernel(x_prompt, x_sample, mem_prompt, cache_a_k, cache_a_v, cache_idx_k, cache_pool, cache_mem_k,
           cache_mem_v, g_mix, w_in, g_qa, g_ka, g_kidx, g_qm, g_mem, w_mem_kv, g_km, w_pool, s_pool,
           w_oa, w_ob, w_om, w_out, g_ffn, w_gate, w_up, w_down):
    depth = w_in.shape[0]
    t = x_prompt.shape[1]
    ts = x_sample.shape[1]
    past = cache_a_k.shape[2]
    b = x_prompt.shape[0]
    n_mem = mem_prompt.shape[1]
    pos_p = jnp.arange(t, dtype=jnp.int32)
    pos_s = past + jnp.arange(ts, dtype=jnp.int32)
    bd64 = _block_diag_ones(D_ATT_A, HD_A)
    bdki = _block_diag_ones(LANES, LANES, limit=D_IDX)

    xp, xs = x_prompt, x_sample
    outs = [[] for _ in range(10)]
    for l in range(depth):
        w_cat, wgates, gqa_t, gka_t, gki_t, gqm = _pack_weights(w_in[l], g_qa[l], g_ka[l], g_kidx[l], g_qm[l])
        wts = (g_mix[l][None, :], w_cat, wgates, gqa_t, gka_t, gki_t, gqm, bd64, bdki,
               w_pool[l].astype(BF16), s_pool[l][None, :],
               w_oa[l].astype(BF16), w_ob[l].astype(BF16), w_om[l].astype(BF16), w_out[l].astype(BF16),
               g_ffn[l][None, :], w_gate[l].astype(BF16), w_up[l].astype(BF16), w_down[l].astype(BF16))

        mk, mv = _memkv_call(mem_prompt.reshape(b * n_mem, -1), g_mem[l][None, :],
                             w_mem_kv[l].astype(BF16), g_km[l][None, :], _tile(b * n_mem, 256))
        mk = mk.reshape(b, n_mem, N_HEADS_M, HD_M)
        mv = mv.reshape(b, n_mem, N_HEADS_M, HD_M)
        hist0 = jnp.zeros((b, HIST_ROWS, D_POOL), F32)
        xp, ka, va, ki, ub = _group_forward(
            xp, pos_p, 0, hist0, None, None, None,
            mk.astype(BF16).transpose(0, 2, 3, 1), mv.astype(BF16).transpose(0, 2, 1, 3), wts, causal=True)
        for lst, val in zip(outs[:6], (ka, va, ki, ub[:, -POOL_HIST:], mk, mv)):
            lst.append(val)

        bs = xs.shape[0]
        hist0 = jnp.concatenate([jnp.zeros((bs, HIST_ROWS - POOL_HIST, D_POOL), F32), cache_pool[l]], axis=1)
        xs, ka, va, ki, ub = _group_forward(
            xs, pos_s, past, hist0, cache_a_k[l], cache_a_v[l], cache_idx_k[l],
            cache_mem_k[l].astype(BF16).transpose(0, 2, 3, 1), cache_mem_v[l].astype(BF16).transpose(0, 2, 1, 3),
            wts, causal=False)
        pool_s = jnp.concatenate([cache_pool[l], ub], axis=1)[:, -POOL_HIST:]
        for lst, val in zip(outs[6:], (ka, va, ki, pool_s)):
            lst.append(val)

    stacked = [jnp.stack(o) for o in outs]
    return (xp, xs, *stacked)
```

```python
import functools

import jax
import jax.numpy as jnp
import numpy as np
from jax import lax
from jax.experimental import pallas as pl
from jax.experimental.pallas import tpu as pltpu

F32 = jnp.float32
BF16 = jnp.bfloat16

CHUNK = 64
EPS = 1e-6
ROPE_THETA = 10000.0
N_HEADS_A = 8
N_KV_A = 2
GROUP_A = N_HEADS_A // N_KV_A
HD_A = 64
D_ATT_A = N_HEADS_A * HD_A
D_KV_A = N_KV_A * HD_A
N_IDX_HEADS = 8
D_IDX = 32
D_QIDX = N_IDX_HEADS * D_IDX
IDX_SCALE = (N_IDX_HEADS * D_IDX) ** -0.5
TOPK_MAX = 256
POOL_WINDOWS = (2, 4, 8, 16)
POOL_GROUP = 128
D_POOL = len(POOL_WINDOWS) * POOL_GROUP
POOL_HIST = 15
N_HEADS_M = 4
HD_M = 128
D_MEM_ATT = N_HEADS_M * HD_M
N_BRANCH = 3

LANES = 128
SUBLANES = 8
V7X_VMEM_BYTES = 64 * 1024 * 1024
VMEM_LIMIT = 56 * 1024 * 1024

HIST_ROWS = 2 * SUBLANES
MASK_VALUE = -1e30
FLT_MAX = float(np.finfo(np.float32).max)
KEY_BITS = 32
Q_SCALE = HD_A ** -0.5 * float(np.log2(np.e))
SUM_FLOOR = 1e-30
COUNT_ROWS = 4 * SUBLANES
COUNT_CHAINS = 2
V_ROWS = HD_A + 2 * SUBLANES

C_QA = 0
C_KA = C_QA + D_ATT_A
C_VA = C_KA + D_KV_A
C_QI = C_VA + D_KV_A
C_KW = C_QI + D_QIDX
C_UB = C_KW + LANES
C_QM = C_UB + D_POOL
C_END = C_QM + D_MEM_ATT

R_KA = 0
R_VA = R_KA + D_KV_A
R_KW = R_VA + D_KV_A
R_UB = R_KW + LANES
R_QM = R_UB + D_POOL
R_END = R_QM + D_MEM_ATT
T_QA = 0
T_QI = T_QA + D_ATT_A
T_VA = T_QI + D_QIDX
T_WI = T_VA + D_KV_A
T_END = T_WI + 2 * SUBLANES


def _cparams(sem):
    return pltpu.CompilerParams(dimension_semantics=sem, vmem_limit_bytes=VMEM_LIMIT)


def _resident(shape):
    zeros = (0,) * len(shape)
    return pl.BlockSpec(shape, lambda *_: zeros, pipeline_mode=pl.Buffered(1))


def _rms(x, g):
    ms = jnp.mean(x * x, axis=-1, keepdims=True)
    return x * lax.rsqrt(ms + EPS) * g


def _group_sumsq(x, bd):
    sq = x * x
    hi = sq.astype(BF16)
    lo = (sq - hi.astype(F32)).astype(BF16)
    return (jnp.dot(hi, bd, preferred_element_type=F32)
            + jnp.dot(lo, bd, preferred_element_type=F32))


def _rope_lanes(x, cos, sin, half):
    lane = lax.broadcasted_iota(jnp.int32, x.shape, 1)
    first = (lane % (2 * half)) < half
    left = pltpu.roll(x, LANES - half, 1)
    right = pltpu.roll(x, half, 1)
    rot = jnp.where(first, -left, right)
    return x * cos + rot * sin


def _proj_kernel(x_ref, gmix_ref, w_ref, cosa_ref, sina_ref, cosi_ref, sini_ref,
                 gqa_ref, gka_ref, gki_ref, gqm_ref, bd64_ref, bdki_ref, *refs, dsa_layout):
    if dsa_layout:
        (wt_ref, cosat_ref, sinat_ref, cosit_ref, sinit_ref, gqab_ref,
         ka_ref, va_ref, kw_ref, ub_ref, qm_ref, qat_ref, qit_ref, wit_ref, kaug_ref, vt_ref, kib_ref) = refs
        c_ka, c_va, c_kw, c_ub, c_qm = R_KA, R_VA, R_KW, R_UB, R_QM
    else:
        qa_ref, ka_ref, va_ref, qi_ref, kw_ref, ub_ref, qm_ref = refs
        c_ka, c_va, c_kw, c_ub, c_qm = C_KA, C_VA, C_KW, C_UB, C_QM
    x = x_ref[...]
    tm = x.shape[0]
    h = _rms(x, gmix_ref[...]).astype(BF16)
    p = jnp.dot(h, w_ref[...], preferred_element_type=F32)
    cosa, sina = cosa_ref[...], sina_ref[...]
    cosi, sini = cosi_ref[...], sini_ref[...]
    bd64 = bd64_ref[...]

    if dsa_layout:
        pt = lax.dot_general(wt_ref[...], h, (((1,), (1,)), ((), ())), preferred_element_type=F32)
        cosat, sinat = cosat_ref[...], sinat_ref[...]
        gq = jnp.concatenate([gqab_ref[...]] * (tm // LANES), axis=1)
        ha = HD_A // 2
        for hd in range(N_HEADS_A):
            xh = pt[T_QA + hd * HD_A:T_QA + (hd + 1) * HD_A, :]
            xh = xh * lax.rsqrt(jnp.sum(xh * xh, axis=0, keepdims=True) * (1.0 / HD_A) + EPS) * gq
            x1, x2 = xh[:ha, :], xh[ha:, :]
            qat_ref[hd * HD_A:hd * HD_A + ha, :] = ((x1 * cosat - x2 * sinat) * Q_SCALE).astype(BF16)
            qat_ref[hd * HD_A + ha:(hd + 1) * HD_A, :] = ((x2 * cosat + x1 * sinat) * Q_SCALE).astype(BF16)
        cosit, sinit = cosit_ref[...], sinit_ref[...]
        hi = D_IDX // 2
        for hd in range(N_IDX_HEADS):
            x1 = pt[T_QI + hd * D_IDX:T_QI + hd * D_IDX + hi, :]
            x2 = pt[T_QI + hd * D_IDX + hi:T_QI + (hd + 1) * D_IDX, :]
            qit_ref[hd * D_IDX:hd * D_IDX + hi, :] = (x1 * cosit - x2 * sinit).astype(BF16)
            qit_ref[hd * D_IDX + hi:(hd + 1) * D_IDX, :] = (x2 * cosit + x1 * sinit).astype(BF16)
        wit_ref[...] = pt[T_WI:T_WI + N_IDX_HEADS, :] * IDX_SCALE
    else:
        qa = p[:, C_QA:C_QA + D_ATT_A]
        ss = _group_sumsq(qa, bd64)
        qa = qa * lax.rsqrt(ss * (1.0 / HD_A) + EPS) * gqa_ref[...]
        for c in range(D_ATT_A // LANES):
            blk = _rope_lanes(qa[:, c * LANES:(c + 1) * LANES], cosa, sina, HD_A // 2) * Q_SCALE
            qa_ref[:, c * LANES:(c + 1) * LANES] = blk.astype(BF16)
        for c in range(D_QIDX // LANES):
            blk = _rope_lanes(p[:, C_QI + c * LANES:C_QI + (c + 1) * LANES], cosi, sini, D_IDX // 2)
            qi_ref[:, c * LANES:(c + 1) * LANES] = blk.astype(BF16)

    ka = p[:, c_ka:c_ka + D_KV_A]
    ss = _group_sumsq(ka, bd64[:D_KV_A, :D_KV_A])
    ka = ka * lax.rsqrt(ss * (1.0 / HD_A) + EPS) * gka_ref[...]
    ka = _rope_lanes(ka, cosa, sina, HD_A // 2)
    va = p[:, c_va:c_va + D_KV_A]
    for g in range(N_KV_A):
        ka_ref[:, g, :] = (ka if g == 0 else pltpu.roll(ka, LANES - g * HD_A, 1))[:, :HD_A]
        va_ref[:, g, :] = (va if g == 0 else pltpu.roll(va, LANES - g * HD_A, 1))[:, :HD_A]
    if dsa_layout:
        lane = lax.broadcasted_iota(jnp.int32, ka.shape, 1)
        tail = jnp.where(lane == HD_A, 1.0, 0.0)
        kaug_ref[0] = jnp.where(lane < HD_A, ka, tail).astype(BF16)
        kaug_ref[1] = jnp.where(lane < HD_A, pltpu.roll(ka, HD_A, 1), tail).astype(BF16)
        srow = lax.broadcasted_iota(jnp.int32, (V_ROWS - HD_A, tm), 0)
        ones_rows = jnp.where(srow == 0, 1.0, 0.0).astype(BF16)
        for g in range(N_KV_A):
            vt_ref[g, 0:HD_A, :] = pt[T_VA + g * HD_A:T_VA + (g + 1) * HD_A, :].astype(BF16)
            vt_ref[g, HD_A:V_ROWS, :] = ones_rows

    kw = p[:, c_kw:c_kw + LANES]
    ss = _group_sumsq(kw, bdki_ref[...])
    kin = kw * lax.rsqrt(ss * (1.0 / D_IDX) + EPS) * gki_ref[...]
    kin = _rope_lanes(kin, cosi, sini, D_IDX // 2)
    lane = lax.broadcasted_iota(jnp.int32, kw.shape, 1)
    kw = jnp.where(lane < D_IDX, kin, kw * IDX_SCALE)
    kw_ref[...] = kw
    if dsa_layout:
        kib_ref[...] = kw[:, :D_IDX].astype(BF16)

    ub_ref[...] = p[:, c_ub:c_ub + D_POOL]

    gqm = gqm_ref[...]
    for hd in range(N_HEADS_M):
        blk = p[:, c_qm + hd * HD_M:c_qm + (hd + 1) * HD_M]
        qm_ref[:, hd * HD_M:(hd + 1) * HD_M] = _rms(blk, gqm).astype(BF16)


def _proj_call(x2d, gmix, w_rows, tabs, gqa_t, gka_t, gki_t, gqm, bd64, bdki, tm, n_tab_blocks,
               dsa_layout, tposed_side=()):
    n = x2d.shape[0]
    d = x2d.shape[1]
    nb = n // (tm * n_tab_blocks)
    grid = (n // tm,)
    row = lambda i: (i, 0)
    tab = lambda i: (i % n_tab_blocks, 0)
    tabt = lambda i: (0, i % n_tab_blocks)
    in_specs = [
        pl.BlockSpec((tm, d), row),
        _resident((1, d)),
        _resident(w_rows.shape),
        pl.BlockSpec((tm, LANES), tab), pl.BlockSpec((tm, LANES), tab),
        pl.BlockSpec((tm, LANES), tab), pl.BlockSpec((tm, LANES), tab),
        _resident((1, D_ATT_A)), _resident((1, D_KV_A)),
        _resident((1, LANES)), _resident((1, HD_M)),
        _resident((D_ATT_A, D_ATT_A)), _resident((LANES, LANES)),
    ]
    if dsa_layout:
        in_specs += [_resident((T_END, d)),
                     pl.BlockSpec((HD_A // 2, tm), tabt), pl.BlockSpec((HD_A // 2, tm), tabt),
                     pl.BlockSpec((D_IDX // 2, tm), tabt), pl.BlockSpec((D_IDX // 2, tm), tabt),
                     _resident((HD_A, LANES))]
    rows = lambda width, dtype: (jax.ShapeDtypeStruct((n, width), dtype), pl.BlockSpec((tm, width), row))
    heads = (jax.ShapeDtypeStruct((n, N_KV_A, HD_A), F32), pl.BlockSpec((tm, N_KV_A, HD_A), lambda i: (i, 0, 0)))
    per_token = [heads, heads, rows(LANES, F32), rows(D_POOL, F32),
                 rows(D_MEM_ATT, BF16)]
    if dsa_layout:
        t = tm * n_tab_blocks
        cols = lambda i: (i // n_tab_blocks, 0, i % n_tab_blocks)
        tposed = lambda r, dtype: (jax.ShapeDtypeStruct((nb, r, t), dtype), pl.BlockSpec((None, r, tm), cols))
        outs = per_token + [
            tposed(D_ATT_A, BF16), tposed(D_QIDX, BF16), tposed(N_IDX_HEADS, F32),
            (jax.ShapeDtypeStruct((N_KV_A, n, LANES), BF16), pl.BlockSpec((N_KV_A, tm, LANES), lambda i: (0, i, 0))),
            (jax.ShapeDtypeStruct((nb, N_KV_A, n_tab_blocks, V_ROWS, tm), BF16),
             pl.BlockSpec((None, N_KV_A, None, V_ROWS, tm),
                          lambda i: (i // n_tab_blocks, 0, i % n_tab_blocks, 0, 0))),
            rows(D_IDX, BF16),
        ]
    else:
        outs = [rows(D_ATT_A, BF16)] + per_token[:2] + [rows(D_QIDX, BF16)] + per_token[2:]
    return pl.pallas_call(
        functools.partial(_proj_kernel, dsa_layout=dsa_layout), grid=grid, in_specs=in_specs,
        out_specs=[o[1] for o in outs], out_shape=[o[0] for o in outs],
        compiler_params=_cparams(("parallel",)), name="proj",
    )(x2d, gmix, w_rows, *tabs, gqa_t, gka_t, gki_t, gqm, bd64, bdki, *tposed_side)


def _key_bits_to_float(u):
    k = u ^ jnp.int32(-2147483648)
    bits = jnp.where(k >= 0, k, k ^ jnp.int32(0x7FFFFFFF))
    return lax.bitcast_convert_type(bits, F32)


def _dsa_kernel(qat_ref, qit_ref, wit_ref, k_ref, vt_ref, ki_ref, o_ref,
                sc_ref, m_ref, acc_ref, qaug_ref, qis_ref, km_ref,
                *, tq, kb, nkb_total, causal, s_valid, topk, rows_out):
    qblk = pl.program_id(1)
    if causal:
        nkb = lax.div(qblk + kb // tq, jnp.int32(kb // tq))
    else:
        nkb = nkb_total
    qpos = qblk * tq + lax.broadcasted_iota(jnp.int32, (1, tq), 1)
    if causal:
        lim = (lax.shift_right_logical(qpos, 6) + 1) * CHUNK
    else:
        lim = jnp.full((1, tq), s_valid, jnp.int32)
    limf = lim.astype(F32)
    krow = lax.broadcasted_iota(jnp.int32, (kb, tq), 0)
    kf = float(topk)

    def for_each_block_pairwise(fn):
        n = jnp.int32(nkb)

        def step(jj, carry):
            fn(2 * jj)
            fn(2 * jj + 1)
            return carry
        lax.fori_loop(0, lax.div(n, 2), step, 0)

        @pl.when(lax.rem(n, 2) == 1)
        def _():
            fn(n - 1)

    for h in range(N_IDX_HEADS):
        qis_ref[:, h * tq:(h + 1) * tq] = qit_ref[h * D_IDX:(h + 1) * D_IDX, :]
    qit = qis_ref[...]
    wit = wit_ref[...]

    def score_block(j):
        z = jnp.dot(ki_ref[j], qit, preferred_element_type=F32)
        s = jnp.zeros((kb, tq), F32)
        for h in range(N_IDX_HEADS):
            s = s + wit[h:h + 1, :] * jnp.maximum(z[:, h * tq:(h + 1) * tq], 0.0)
        sc_ref[j] = jnp.where(j * kb + krow < lim, s, -jnp.inf)

    for_each_block_pairwise(score_block)

    crow = lax.broadcasted_iota(jnp.int32, (COUNT_ROWS, tq), 0)

    def count(pred):
        def body(j, accs):
            accs = list(accs)
            for ci, r in enumerate(range(0, kb, COUNT_ROWS)):
                hit = pred(sc_ref[j, r:r + COUNT_ROWS, :], j * kb + r + crow)
                accs[ci % COUNT_CHAINS] = accs[ci % COUNT_CHAINS] + jnp.where(hit, 1.0, 0.0)
            return tuple(accs)
        zero = jnp.zeros((COUNT_ROWS, tq), F32)
        accs = lax.fori_loop(0, nkb, body, (zero,) * COUNT_CHAINS)
        return jnp.sum(sum(accs[1:], accs[0]), axis=0, keepdims=True)

    def radix_body(i, carry):
        prefix, cge, cgt = carry
        cand = prefix | lax.shift_left(jnp.int32(1), KEY_BITS - 1 - i)
        t = _key_bits_to_float(cand)
        c = count(lambda s, idx: s >= t)
        take = c >= kf
        return jnp.where(take, cand, prefix), jnp.where(take, c, cge), jnp.where(take, cgt, c)

    prefix, cge, cgt = lax.fori_loop(
        0, KEY_BITS, radix_body, (jnp.zeros((1, tq), jnp.int32), limf, jnp.zeros((1, tq), F32)))
    thr = jnp.where(limf >= kf, _key_bits_to_float(prefix), -FLT_MAX)

    any_tie = jnp.max(jnp.where(cge > kf, 1.0, 0.0), axis=1, keepdims=True)

    def mask_plain():
        def body(j, carry):
            sc_ref[j] = jnp.where(sc_ref[j] >= thr, 0.0, MASK_VALUE)
            return carry
        lax.fori_loop(0, nkb, body, 0)

    def mask_ties():
        need = kf - cgt
        tri = (lax.broadcasted_iota(jnp.int32, (kb, kb), 1)
               <= lax.broadcasted_iota(jnp.int32, (kb, kb), 0)).astype(BF16)

        def body(j, seen):
            s = sc_ref[j]
            eq = s == thr
            rank = jnp.dot(tri, jnp.where(eq, 1.0, 0.0).astype(BF16), preferred_element_type=F32) + seen
            keep = (s > thr) | (eq & (rank <= need))
            sc_ref[j] = jnp.where(keep, 0.0, MASK_VALUE)
            return rank[kb - 1:kb, :]
        lax.fori_loop(0, nkb, body, jnp.zeros((1, tq), F32))

    lax.cond(any_tie[0, 0] > 0.0, mask_ties, mask_plain)

    @pl.when(qblk == 0)
    def _():
        for g in range(N_KV_A):
            def norm_body(j, mx, g=g):
                kk = k_ref[g, j].astype(F32)
                return jnp.maximum(mx, jnp.sum(kk * kk, axis=1, keepdims=True))
            mx = lax.fori_loop(0, nkb_total, norm_body, jnp.zeros((kb, 1), F32))
            km_ref[g] = jnp.broadcast_to(jnp.max(mx, axis=0, keepdims=True), (SUBLANES, LANES))

    aug_row = lax.broadcasted_iota(jnp.int32, (LANES - HD_A, tq), 0)
    for g in range(N_KV_A):
        kmax2 = km_ref[g][0:1, :]
        kmax2 = jnp.concatenate([kmax2] * (tq // LANES), axis=1) if tq >= LANES else kmax2[:, :tq]
        for u in range(GROUP_A):
            hd = g * GROUP_A + u
            q = qat_ref[hd * HD_A:(hd + 1) * HD_A, :]
            qf = q.astype(F32)
            shift = jnp.sqrt(jnp.sum(qf * qf, axis=0, keepdims=True) * kmax2)
            qaug_ref[g, 0:HD_A, u * tq:(u + 1) * tq] = q
            qaug_ref[g, HD_A:LANES, u * tq:(u + 1) * tq] = jnp.where(aug_row == 0, -shift, 0.0).astype(BF16)

    acc_ref[...] = jnp.zeros(acc_ref.shape, F32)

    def attend_block(j):
        bias = sc_ref[j]
        for g in range(N_KV_A):
            lg = jnp.dot(k_ref[g, j], qaug_ref[g], preferred_element_type=F32)
            vt = vt_ref[g, j]
            for u in range(GROUP_A):
                hd = g * GROUP_A + u
                p = jnp.exp2(lg[:, u * tq:(u + 1) * tq] + bias).astype(BF16)
                acc_ref[hd] += jnp.dot(vt, p, preferred_element_type=F32)

    for_each_block_pairwise(attend_block)

    sums_ok = jnp.ones((1, tq), F32)
    for hd in range(N_HEADS_A):
        sums_ok = jnp.where(acc_ref[hd][HD_A:HD_A + 1, :] > SUM_FLOOR, sums_ok, 0.0)
    all_ok = jnp.min(sums_ok, axis=1, keepdims=True)

    @pl.when(all_ok[0, 0] < 1.0)
    def _():
        m_ref[...] = jnp.full(m_ref.shape, MASK_VALUE, F32)
        acc_ref[...] = jnp.zeros(acc_ref.shape, F32)

        def online_body(j, carry):
            bias = sc_ref[j]
            for g in range(N_KV_A):
                lg = jnp.dot(k_ref[g, j], qaug_ref[g], preferred_element_type=F32)
                vt = vt_ref[g, j]
                for u in range(GROUP_A):
                    hd = g * GROUP_A + u
                    l = lg[:, u * tq:(u + 1) * tq] + bias
                    m_old = m_ref[hd:hd + 1, :]
                    m_new = jnp.maximum(m_old, jnp.max(l, axis=0, keepdims=True))
                    p = jnp.exp2(l - m_new).astype(BF16)
                    pv = jnp.dot(vt, p, preferred_element_type=F32)
                    acc_ref[hd] = jnp.exp2(m_old - m_new) * acc_ref[hd] + pv
                    m_ref[hd:hd + 1, :] = m_new
            return carry

        lax.fori_loop(0, nkb, online_body, 0)

    heads = []
    for hd in range(N_HEADS_A):
        a = acc_ref[hd]
        heads.append(a[:HD_A, :] / a[HD_A:HD_A + 1, :])
    out_t = jnp.concatenate(heads, axis=0)
    o_ref[...] = (out_t.T if rows_out else out_t).astype(BF16)


def _dsa_call(qat, qit, wit, k, vt, ki, *, tq, kb, causal, s_valid, topk):
    b, t = qat.shape[0], qat.shape[2]
    nkb_total = k.shape[2]
    rows_out = tq % LANES == 0
    kern = functools.partial(_dsa_kernel, tq=tq, kb=kb, nkb_total=nkb_total, causal=causal,
                             s_valid=s_valid, topk=topk, rows_out=rows_out)
    qcols = lambda i, j: (i, 0, j)
    in_specs = [
        pl.BlockSpec((None, D_ATT_A, tq), qcols),
        pl.BlockSpec((None, D_QIDX, tq), qcols),
        pl.BlockSpec((None, N_IDX_HEADS, tq), qcols),
        pl.BlockSpec((N_KV_A, None, nkb_total, kb, LANES), lambda i, j: (0, i, 0, 0, 0)),
        pl.BlockSpec((None, N_KV_A, nkb_total, V_ROWS, kb), lambda i, j: (i, 0, 0, 0, 0)),
        pl.BlockSpec((None, nkb_total, kb, D_IDX), lambda i, j: (i, 0, 0, 0)),
    ]
    if rows_out:
        out_spec = pl.BlockSpec((None, tq, D_ATT_A), lambda i, j: (i, j, 0))
        out_shape = jax.ShapeDtypeStruct((b, t, D_ATT_A), BF16)
    else:
        out_spec = pl.BlockSpec((None, D_ATT_A, tq), qcols)
        out_shape = jax.ShapeDtypeStruct((b, D_ATT_A, t), BF16)
    return pl.pallas_call(
        kern, grid=(b, t // tq), in_specs=in_specs, out_specs=out_spec, out_shape=out_shape,
        scratch_shapes=[pltpu.VMEM((nkb_total, kb, tq), F32),
                        pltpu.VMEM((N_HEADS_A, tq), F32),
                        pltpu.VMEM((N_HEADS_A, V_ROWS, tq), F32),
                        pltpu.VMEM((N_KV_A, LANES, GROUP_A * tq), BF16),
                        pltpu.VMEM((D_IDX, N_IDX_HEADS * tq), BF16),
                        pltpu.VMEM((N_KV_A, SUBLANES, LANES), F32)],
        compiler_params=_cparams(("parallel", "arbitrary")), name="dsa",
    )(qat, qit, wit, k, vt, ki)


def _memkv_kernel(mem_ref, gmem_ref, w_ref, gkm_ref, k_ref, v_ref):
    h = _rms(mem_ref[...], gmem_ref[...]).astype(BF16)
    kv = jnp.dot(h, w_ref[...], preferred_element_type=F32)
    gkm = gkm_ref[...]
    for hd in range(N_HEADS_M):
        k_ref[:, hd * HD_M:(hd + 1) * HD_M] = _rms(kv[:, hd * HD_M:(hd + 1) * HD_M], gkm)
    v_ref[...] = kv[:, D_MEM_ATT:]


def _memkv_call(mem2d, gmem, w_kv, gkm, tm):
    n, d = mem2d.shape
    row = lambda i: (i, 0)
    return pl.pallas_call(
        _memkv_kernel, grid=(n // tm,),
        in_specs=[pl.BlockSpec((tm, d), row), _resident((1, d)),
                  _resident((d, 2 * D_MEM_ATT)), _resident((1, HD_M))],
        out_specs=[pl.BlockSpec((tm, D_MEM_ATT), row), pl.BlockSpec((tm, D_MEM_ATT), row)],
        out_shape=[jax.ShapeDtypeStruct((n, D_MEM_ATT), F32), jax.ShapeDtypeStruct((n, D_MEM_ATT), F32)],
        compiler_params=_cparams(("parallel",)), name="memkv",
    )(mem2d, gmem, w_kv, gkm)


def _merge_kernel(x_ref, a_ref, ub_ref, prev_ref, hist0_ref, qm_ref, mkt_ref, mv_ref,
                  gmix_ref, wg_ref, woa_ref, wob_ref, wom_ref, wout_ref, wpool_ref, spool_ref,
                  o_ref, ext_ref, *, tm, pos0):
    it = pl.program_id(1)
    x = x_ref[...]

    h = _rms(x, gmix_ref[...]).astype(BF16)
    gates = jax.nn.sigmoid(jnp.dot(h, wg_ref[...], preferred_element_type=F32))
    d = x.shape[1]

    ub = ub_ref[...]
    ext_ref[0:HIST_ROWS, :] = jnp.where(it == 0, hist0_ref[...], prev_ref[...])
    ext_ref[HIST_ROWS:HIST_ROWS + tm, :] = ub
    pos = pos0 + it * tm + lax.broadcasted_iota(jnp.int32, (tm, 1), 0)
    ys = []
    for g, w in enumerate(POOL_WINDOWS):
        c0 = g * POOL_GROUP
        win = ub[:, c0:c0 + POOL_GROUP]
        for k in range(1, w):
            win = win + ext_ref[HIST_ROWS - k:HIST_ROWS - k + tm, c0:c0 + POOL_GROUP]
        cnt = jnp.minimum(w, pos + 1).astype(F32)
        pg = (win / cnt - ub[:, c0:c0 + POOL_GROUP]).astype(BF16)
        ys.append(jnp.dot(pg, wpool_ref[g], preferred_element_type=F32))
    bmix = (jnp.concatenate(ys, axis=1) * spool_ref[...]).astype(BF16)

    qm = qm_ref[...]
    ms = []
    for hd in range(N_HEADS_M):
        lg = jnp.dot(qm[:, hd * HD_M:(hd + 1) * HD_M], mkt_ref[hd], preferred_element_type=F32)
        lg = lg * (HD_M ** -0.5)
        e = jnp.exp(lg - jnp.max(lg, axis=1, keepdims=True))
        pr = (e / jnp.sum(e, axis=1, keepdims=True)).astype(BF16)
        ms.append(jnp.dot(pr, mv_ref[hd], preferred_element_type=F32))
    mmix = jnp.concatenate(ms, axis=1).astype(BF16)

    mixed = (gates[:, 0:d] * jnp.dot(a_ref[...], woa_ref[...], preferred_element_type=F32)
             + gates[:, d:2 * d] * jnp.dot(bmix, wob_ref[...], preferred_element_type=F32)
             + gates[:, 2 * d:3 * d] * jnp.dot(mmix, wom_ref[...], preferred_element_type=F32))
    o_ref[...] = x + jnp.dot(mixed.astype(BF16), wout_ref[...], preferred_element_type=F32)


def _merge_call(x, a, ub, hist0, qm, mkt, mv, gmix, wg, woa, wob, wom, wout, wpool, spool, *, tm, pos0):
    b, t, d = x.shape
    nt = t // tm
    hb = tm // HIST_ROWS
    tok = lambda i, j: (i, j, 0)
    per_b3 = lambda i, j: (i, 0, 0)
    per_b4 = lambda i, j: (i, 0, 0, 0)
    in_specs = [
        pl.BlockSpec((None, tm, d), tok),
        pl.BlockSpec((None, tm, D_ATT_A), tok),
        pl.BlockSpec((None, tm, D_POOL), tok),
        pl.BlockSpec((None, HIST_ROWS, D_POOL), lambda i, j: (i, jnp.maximum(j * hb - 1, 0), 0)),
        pl.BlockSpec((None, HIST_ROWS, D_POOL), per_b3),
        pl.BlockSpec((None, tm, D_MEM_ATT), tok),
        pl.BlockSpec((None, N_HEADS_M, HD_M, mkt.shape[3]), per_b4),
        pl.BlockSpec((None, N_HEADS_M, mv.shape[2], HD_M), per_b4),
        _resident((1, d)),
        _resident(wg.shape),
        _resident(woa.shape), _resident(wob.shape), _resident(wom.shape),
        _resident(wout.shape),
        _resident(wpool.shape),
        _resident((1, D_POOL)),
    ]
    return pl.pallas_call(
        functools.partial(_merge_kernel, tm=tm, pos0=pos0), grid=(b, nt), in_specs=in_specs,
        out_specs=pl.BlockSpec((None, tm, d), tok),
        out_shape=jax.ShapeDtypeStruct((b, t, d), F32),
        scratch_shapes=[pltpu.VMEM((HIST_ROWS + tm, D_POOL), F32)],
        compiler_params=_cparams(("parallel", "arbitrary")), name="merge",
    )(x, a, ub, ub, hist0, qm, mkt, mv, gmix, wg, woa, wob, wom, wout, wpool, spool)


def _ffn_kernel(x_ref, g_ref, wgate_ref, wup_ref, wdown_ref, o_ref):
    x = x_ref[...]
    h = _rms(x, g_ref[...]).astype(BF16)
    gate = jnp.dot(h, wgate_ref[...], preferred_element_type=F32)
    up = jnp.dot(h, wup_ref[...], preferred_element_type=F32)
    act = (jax.nn.silu(gate) * up).astype(BF16)
    o_ref[...] = x + jnp.dot(act, wdown_ref[...], preferred_element_type=F32)


def _ffn_call(x2d, g, wgate, wup, wdown, tm):
    n, d = x2d.shape
    row = lambda i: (i, 0)
    return pl.pallas_call(
        _ffn_kernel, grid=(n // tm,),
        in_specs=[pl.BlockSpec((tm, d), row), _resident((1, d)),
                  _resident(wgate.shape), _resident(wup.shape), _resident(wdown.shape)],
        out_specs=pl.BlockSpec((tm, d), row),
        out_shape=jax.ShapeDtypeStruct((n, d), F32),
        compiler_params=_cparams(("parallel",)), name="ffn",
    )(x2d, g, wgate, wup, wdown)


def _rope_tables(pos, head_dim):
    half = head_dim // 2
    inv = ROPE_THETA ** (-jnp.arange(half, dtype=F32) / half)
    ang = pos.astype(F32)[:, None] * inv[None, :]
    reps = LANES // half
    return jnp.tile(jnp.cos(ang), (1, reps)), jnp.tile(jnp.sin(ang), (1, reps))


def _rope_tables_t(pos, head_dim):
    half = head_dim // 2
    inv = ROPE_THETA ** (-jnp.arange(half, dtype=F32) / half)
    ang = inv[:, None] * pos.astype(F32)[None, :]
    return jnp.cos(ang), jnp.sin(ang)


def _pack_weights(w_in, g_qa, g_ka, g_kidx, g_qm):
    d = w_in.shape[0]
    widths = (D_ATT_A, D_KV_A, D_KV_A, D_QIDX, D_IDX, N_IDX_HEADS, D_POOL, D_MEM_ATT, N_BRANCH * d)
    cuts = [int(c) for c in np.cumsum(widths)[:-1]]
    wqa, wka, wva, wqi, wki, wwi, wub, wqm, wgates = jnp.split(w_in, cuts, axis=1)
    pad = jnp.zeros((d, LANES - D_IDX - N_IDX_HEADS), w_in.dtype)
    w_cat = jnp.concatenate([wqa, wka, wva, wqi, wki, wwi, pad, wub, wqm], axis=1).astype(BF16)
    w_rows = jnp.concatenate([wka, wva, wki, wwi, pad, wub, wqm], axis=1).astype(BF16)
    wpad = jnp.zeros((d, T_END - T_WI - N_IDX_HEADS), w_in.dtype)
    w_t = jnp.concatenate([wqa, wqi, wva, wwi, wpad], axis=1).T.astype(BF16)
    gqa_t = jnp.tile(g_qa, N_HEADS_A)[None, :]
    gka_t = jnp.tile(g_ka, N_KV_A)[None, :]
    gki_t = jnp.concatenate([g_kidx, jnp.ones((LANES - D_IDX,), g_kidx.dtype)])[None, :]
    gqa_b = jnp.broadcast_to(g_qa[:, None], (HD_A, LANES))
    return w_cat, w_rows, w_t, wgates.astype(BF16), gqa_t, gka_t, gki_t, g_qm[None, :], gqa_b


def _block_diag_ones(n, group, limit=None):
    i = np.arange(n)
    m = (i[:, None] // group) == (i[None, :] // group)
    if limit is not None:
        m = m & (i[:, None] < limit) & (i[None, :] < limit)
    return jnp.asarray(m, BF16)


def _key_blocks(k_all, v_all, ki_all, kb):
    b, s = k_all.shape[0], k_all.shape[1]
    nkb = -(-s // kb)
    pad = nkb * kb - s
    if pad:
        k_all = jnp.pad(k_all, ((0, 0), (0, pad), (0, 0), (0, 0)))
        v_all = jnp.pad(v_all, ((0, 0), (0, pad), (0, 0), (0, 0)))
        ki_all = jnp.pad(ki_all, ((0, 0), (0, pad), (0, 0)))
    ones = jnp.ones(v_all.shape[:-1] + (1,), BF16)
    zeros = jnp.zeros(v_all.shape[:-1] + (LANES - HD_A - 1,), BF16)
    kaug = jnp.concatenate([k_all.astype(BF16), ones, zeros], axis=-1)
    k = kaug.reshape(b, nkb, kb, N_KV_A, LANES).transpose(3, 0, 1, 2, 4)
    vaug = jnp.concatenate([v_all.astype(BF16), ones, zeros[..., :V_ROWS - HD_A - 1]], axis=-1)
    vt = vaug.reshape(b, nkb, kb, N_KV_A, V_ROWS).transpose(0, 3, 1, 4, 2)
    ki = ki_all.astype(BF16).reshape(b, nkb, kb, D_IDX)
    return k, vt, ki


def _tile(n, pref):
    t = pref
    while n % t:
        t //= 2
    return t


def _group_forward(x, pos, pos0, hist0, k_hist, v_hist, ki_hist, mkt, mv, wts, *, causal):
    (gmix, w_cat, w_rows, w_t, gqa_b, wgates, gqa_t, gka_t, gki_t, gqm, bd64, bdki, wpool, spool,
     woa, wob, wom, wout, gffn, wgate, wup, wdown) = wts
    b, t, d = x.shape
    n = b * t
    tm = _tile(t, 512)
    cosa, sina = _rope_tables(pos, HD_A)
    cosi, sini = _rope_tables(pos, D_IDX)
    tq = _tile(t, 256)
    proj_args = ((cosa, sina, cosi, sini), gqa_t, gka_t, gki_t, gqm, bd64, bdki, tm, t // tm)
    if causal:
        tside = (w_t,) + _rope_tables_t(pos, HD_A) + _rope_tables_t(pos, D_IDX) + (gqa_b,)
        ka, va, kw, ub, qm, qat, qit, wit, kaug, vt, kib = _proj_call(
            x.reshape(n, d), gmix, w_rows, *proj_args, dsa_layout=True, tposed_side=tside)
        kb, s = tm, t
        kblk = kaug.reshape(N_KV_A, b, t // kb, kb, LANES)
        kiblk = kib.reshape(b, t // kb, kb, D_IDX)
    else:
        qa, ka, va, qi, kw, ub, qm = _proj_call(x.reshape(n, d), gmix, w_cat, *proj_args, dsa_layout=False)
        qat = qa.reshape(b, t, D_ATT_A).transpose(0, 2, 1)
        qit = qi.reshape(b, t, D_QIDX).transpose(0, 2, 1)
        wit = kw[:, D_IDX:D_IDX + N_IDX_HEADS].reshape(b, t, N_IDX_HEADS).transpose(0, 2, 1)
    ka = ka.reshape(b, t, N_KV_A, HD_A)
    va = va.reshape(b, t, N_KV_A, HD_A)
    ki = kw[:, :D_IDX].reshape(b, t, D_IDX)
    ub = ub.reshape(b, t, D_POOL)
    if not causal:
        k_all = jnp.concatenate([k_hist, ka], axis=1)
        s, kb = k_all.shape[1], 256
        kblk, vt, kiblk = _key_blocks(k_all, jnp.concatenate([v_hist, va], axis=1),
                                      jnp.concatenate([ki_hist, ki], axis=1), kb)
    topk = min(TOPK_MAX, s // 4)
    a = _dsa_call(qat, qit, wit, kblk, vt, kiblk, tq=tq, kb=kb, causal=causal, s_valid=s, topk=topk)
    if tq % LANES:
        a = a.transpose(0, 2, 1)

    tmm = _tile(t, 512)
    x2 = _merge_call(x, a, ub, hist0, qm.reshape(b, t, D_MEM_ATT), mkt, mv, gmix, wgates,
                     woa, wob, wom, wout, wpool, spool, tm=tmm, pos0=pos0)
    y = _ffn_call(x2.reshape(n, d), gffn, wgate, wup, wdown, _tile(n, 512)).reshape(b, t, d)
    return y, ka, va, ki, ub


def kernel(x_prompt, x_sample, mem_prompt, cache_a_k, cache_a_v, cache_idx_k, cache_pool, cache_mem_k,
           cache_mem_v, g_mix, w_in, g_qa, g_ka, g_kidx, g_qm, g_mem, w_mem_kv, g_km, w_pool, s_pool,
           w_oa, w_ob, w_om, w_out, g_ffn, w_gate, w_up, w_down):
    depth = w_in.shape[0]
    t = x_prompt.shape[1]
    ts = x_sample.shape[1]
    past = cache_a_k.shape[2]
    b = x_prompt.shape[0]
    n_mem = mem_prompt.shape[1]
    pos_p = jnp.arange(t, dtype=jnp.int32)
    pos_s = past + jnp.arange(ts, dtype=jnp.int32)
    bd64 = _block_diag_ones(D_ATT_A, HD_A)
    bdki = _block_diag_ones(LANES, LANES, limit=D_IDX)

    xp, xs = x_prompt, x_sample
    outs = [[] for _ in range(10)]
    for l in range(depth):
        w_cat, w_rows, w_t, wgates, gqa_t, gka_t, gki_t, gqm, gqa_b = _pack_weights(
            w_in[l], g_qa[l], g_ka[l], g_kidx[l], g_qm[l])
        wts = (g_mix[l][None, :], w_cat, w_rows, w_t, gqa_b, wgates, gqa_t, gka_t, gki_t, gqm, bd64, bdki,
               w_pool[l].astype(BF16), s_pool[l][None, :],
               w_oa[l].astype(BF16), w_ob[l].astype(BF16), w_om[l].astype(BF16), w_out[l].astype(BF16),
               g_ffn[l][None, :], w_gate[l].astype(BF16), w_up[l].astype(BF16), w_down[l].astype(BF16))

        mk, mv = _memkv_call(mem_prompt.reshape(b * n_mem, -1), g_mem[l][None, :],
                             w_mem_kv[l].astype(BF16), g_km[l][None, :], _tile(b * n_mem, 256))
        mk = mk.reshape(b, n_mem, N_HEADS_M, HD_M)
        mv = mv.reshape(b, n_mem, N_HEADS_M, HD_M)
        hist0 = jnp.zeros((b, HIST_ROWS, D_POOL), F32)
        xp, ka, va, ki, ub = _group_forward(
            xp, pos_p, 0, hist0, None, None, None,
            mk.astype(BF16).transpose(0, 2, 3, 1), mv.astype(BF16).transpose(0, 2, 1, 3), wts, causal=True)
        for lst, val in zip(outs[:6], (ka, va, ki, ub[:, -POOL_HIST:], mk, mv)):
            lst.append(val)

        bs = xs.shape[0]
        hist0 = jnp.concatenate([jnp.zeros((bs, HIST_ROWS - POOL_HIST, D_POOL), F32), cache_pool[l]], axis=1)
        xs, ka, va, ki, ub = _group_forward(
            xs, pos_s, past, hist0, cache_a_k[l], cache_a_v[l], cache_idx_k[l],
            cache_mem_k[l].astype(BF16).transpose(0, 2, 3, 1), cache_mem_v[l].astype(BF16).transpose(0, 2, 1, 3),
            wts, causal=False)
        pool_s = jnp.concatenate([cache_pool[l], ub], axis=1)[:, -POOL_HIST:]
        for lst, val in zip(outs[6:], (ka, va, ki, pool_s)):
            lst.append(val)

    stacked = [jnp.stack(o) for o in outs]
    return (xp, xs, *stacked)
```

```python
import functools

import jax
import jax.numpy as jnp
import numpy as np
from jax import lax
from jax.experimental import pallas as pl
from jax.experimental.pallas import tpu as pltpu

F32 = jnp.float32
BF16 = jnp.bfloat16

CHUNK = 64
EPS = 1e-6
ROPE_THETA = 10000.0
N_HEADS_A = 8
N_KV_A = 2
GROUP_A = N_HEADS_A // N_KV_A
HD_A = 64
D_ATT_A = N_HEADS_A * HD_A
D_KV_A = N_KV_A * HD_A
N_IDX_HEADS = 8
D_IDX = 32
D_QIDX = N_IDX_HEADS * D_IDX
IDX_SCALE = (N_IDX_HEADS * D_IDX) ** -0.5
TOPK_MAX = 256
POOL_WINDOWS = (2, 4, 8, 16)
POOL_GROUP = 128
D_POOL = len(POOL_WINDOWS) * POOL_GROUP
POOL_HIST = 15
N_HEADS_M = 4
HD_M = 128
D_MEM_ATT = N_HEADS_M * HD_M
N_BRANCH = 3

LANES = 128
SUBLANES = 8
V7X_VMEM_BYTES = 64 * 1024 * 1024
VMEM_LIMIT = 56 * 1024 * 1024

HIST_ROWS = 2 * SUBLANES
MASK_VALUE = -1e30
FLT_MAX = float(np.finfo(np.float32).max)
KEY_BITS = 32
Q_SCALE = HD_A ** -0.5 * float(np.log2(np.e))
SUM_FLOOR = 1e-30
COUNT_ROWS = 4 * SUBLANES
COUNT_CHAINS = 2
COUNT_BLOCKS_PER_STEP = 4
V_ROWS = HD_A + 2 * SUBLANES

C_QA = 0
C_KA = C_QA + D_ATT_A
C_VA = C_KA + D_KV_A
C_QI = C_VA + D_KV_A
C_KW = C_QI + D_QIDX
C_UB = C_KW + LANES
C_QM = C_UB + D_POOL
C_END = C_QM + D_MEM_ATT

R_KA = 0
R_VA = R_KA + D_KV_A
R_KW = R_VA + D_KV_A
R_UB = R_KW + LANES
R_QM = R_UB + D_POOL
R_END = R_QM + D_MEM_ATT
T_QA = 0
T_QI = T_QA + D_ATT_A
T_VA = T_QI + D_QIDX
T_WI = T_VA + D_KV_A
T_END = T_WI + 2 * SUBLANES


def _cparams(sem):
    return pltpu.CompilerParams(dimension_semantics=sem, vmem_limit_bytes=VMEM_LIMIT)


def _resident(shape):
    zeros = (0,) * len(shape)
    return pl.BlockSpec(shape, lambda *_: zeros, pipeline_mode=pl.Buffered(1))


def _rms(x, g):
    ms = jnp.mean(x * x, axis=-1, keepdims=True)
    return x * lax.rsqrt(ms + EPS) * g


def _group_sumsq(x, bd):
    sq = x * x
    hi = sq.astype(BF16)
    lo = (sq - hi.astype(F32)).astype(BF16)
    return (jnp.dot(hi, bd, preferred_element_type=F32)
            + jnp.dot(lo, bd, preferred_element_type=F32))


def _rope_lanes(x, cos, sin, half):
    lane = lax.broadcasted_iota(jnp.int32, x.shape, 1)
    first = (lane % (2 * half)) < half
    left = pltpu.roll(x, LANES - half, 1)
    right = pltpu.roll(x, half, 1)
    rot = jnp.where(first, -left, right)
    return x * cos + rot * sin


def _proj_kernel(x_ref, gmix_ref, w_ref, cosa_ref, sina_ref, cosi_ref, sini_ref,
                 gqa_ref, gka_ref, gki_ref, gqm_ref, bd64_ref, bdki_ref, *refs, dsa_layout):
    if dsa_layout:
        (wt_ref, cosat_ref, sinat_ref, cosit_ref, sinit_ref, gqab_ref,
         ka_ref, va_ref, kw_ref, ub_ref, qm_ref, qat_ref, qit_ref, wit_ref, kaug_ref, vt_ref, kib_ref) = refs
        c_ka, c_va, c_kw, c_ub, c_qm = R_KA, R_VA, R_KW, R_UB, R_QM
    else:
        qa_ref, ka_ref, va_ref, qi_ref, kw_ref, ub_ref, qm_ref = refs
        c_ka, c_va, c_kw, c_ub, c_qm = C_KA, C_VA, C_KW, C_UB, C_QM
    x = x_ref[...]
    tm = x.shape[0]
    h = _rms(x, gmix_ref[...]).astype(BF16)
    p = jnp.dot(h, w_ref[...], preferred_element_type=F32)
    cosa, sina = cosa_ref[...], sina_ref[...]
    cosi, sini = cosi_ref[...], sini_ref[...]
    bd64 = bd64_ref[...]

    if dsa_layout:
        pt = lax.dot_general(wt_ref[...], h, (((1,), (1,)), ((), ())), preferred_element_type=F32)
        cosat, sinat = cosat_ref[...], sinat_ref[...]
        gq = jnp.concatenate([gqab_ref[...]] * (tm // LANES), axis=1)
        ha = HD_A // 2
        for hd in range(N_HEADS_A):
            xh = pt[T_QA + hd * HD_A:T_QA + (hd + 1) * HD_A, :]
            xh = xh * lax.rsqrt(jnp.sum(xh * xh, axis=0, keepdims=True) * (1.0 / HD_A) + EPS) * gq
            x1, x2 = xh[:ha, :], xh[ha:, :]
            qat_ref[hd * HD_A:hd * HD_A + ha, :] = ((x1 * cosat - x2 * sinat) * Q_SCALE).astype(BF16)
            qat_ref[hd * HD_A + ha:(hd + 1) * HD_A, :] = ((x2 * cosat + x1 * sinat) * Q_SCALE).astype(BF16)
        cosit, sinit = cosit_ref[...], sinit_ref[...]
        hi = D_IDX // 2
        for hd in range(N_IDX_HEADS):
            x1 = pt[T_QI + hd * D_IDX:T_QI + hd * D_IDX + hi, :]
            x2 = pt[T_QI + hd * D_IDX + hi:T_QI + (hd + 1) * D_IDX, :]
            qit_ref[hd * D_IDX:hd * D_IDX + hi, :] = (x1 * cosit - x2 * sinit).astype(BF16)
            qit_ref[hd * D_IDX + hi:(hd + 1) * D_IDX, :] = (x2 * cosit + x1 * sinit).astype(BF16)
        wit_ref[...] = pt[T_WI:T_WI + N_IDX_HEADS, :] * IDX_SCALE
    else:
        qa = p[:, C_QA:C_QA + D_ATT_A]
        ss = _group_sumsq(qa, bd64)
        qa = qa * lax.rsqrt(ss * (1.0 / HD_A) + EPS) * gqa_ref[...]
        for c in range(D_ATT_A // LANES):
            blk = _rope_lanes(qa[:, c * LANES:(c + 1) * LANES], cosa, sina, HD_A // 2) * Q_SCALE
            qa_ref[:, c * LANES:(c + 1) * LANES] = blk.astype(BF16)
        for c in range(D_QIDX // LANES):
            blk = _rope_lanes(p[:, C_QI + c * LANES:C_QI + (c + 1) * LANES], cosi, sini, D_IDX // 2)
            qi_ref[:, c * LANES:(c + 1) * LANES] = blk.astype(BF16)

    ka = p[:, c_ka:c_ka + D_KV_A]
    ss = _group_sumsq(ka, bd64[:D_KV_A, :D_KV_A])
    ka = ka * lax.rsqrt(ss * (1.0 / HD_A) + EPS) * gka_ref[...]
    ka = _rope_lanes(ka, cosa, sina, HD_A // 2)
    va = p[:, c_va:c_va + D_KV_A]
    for g in range(N_KV_A):
        ka_ref[:, g, :] = (ka if g == 0 else pltpu.roll(ka, LANES - g * HD_A, 1))[:, :HD_A]
        va_ref[:, g, :] = (va if g == 0 else pltpu.roll(va, LANES - g * HD_A, 1))[:, :HD_A]
    if dsa_layout:
        lane = lax.broadcasted_iota(jnp.int32, ka.shape, 1)
        tail = jnp.where(lane == HD_A, 1.0, 0.0)
        kaug_ref[0] = jnp.where(lane < HD_A, ka, tail).astype(BF16)
        kaug_ref[1] = jnp.where(lane < HD_A, pltpu.roll(ka, HD_A, 1), tail).astype(BF16)
        srow = lax.broadcasted_iota(jnp.int32, (V_ROWS - HD_A, tm), 0)
        ones_rows = jnp.where(srow == 0, 1.0, 0.0).astype(BF16)
        for g in range(N_KV_A):
            vt_ref[g, 0:HD_A, :] = pt[T_VA + g * HD_A:T_VA + (g + 1) * HD_A, :].astype(BF16)
            vt_ref[g, HD_A:V_ROWS, :] = ones_rows

    kw = p[:, c_kw:c_kw + LANES]
    ss = _group_sumsq(kw, bdki_ref[...])
    kin = kw * lax.rsqrt(ss * (1.0 / D_IDX) + EPS) * gki_ref[...]
    kin = _rope_lanes(kin, cosi, sini, D_IDX // 2)
    lane = lax.broadcasted_iota(jnp.int32, kw.shape, 1)
    kw = jnp.where(lane < D_IDX, kin, kw * IDX_SCALE)
    kw_ref[...] = kw
    if dsa_layout:
        kib_ref[...] = kw[:, :D_IDX].astype(BF16)

    ub_ref[...] = p[:, c_ub:c_ub + D_POOL]

    gqm = gqm_ref[...]
    for hd in range(N_HEADS_M):
        blk = p[:, c_qm + hd * HD_M:c_qm + (hd + 1) * HD_M]
        qm_ref[:, hd * HD_M:(hd + 1) * HD_M] = _rms(blk, gqm).astype(BF16)


def _proj_call(x2d, gmix, w_rows, tabs, gqa_t, gka_t, gki_t, gqm, bd64, bdki, tm, n_tab_blocks,
               dsa_layout, tposed_side=()):
    n = x2d.shape[0]
    d = x2d.shape[1]
    nb = n // (tm * n_tab_blocks)
    grid = (n // tm,)
    row = lambda i: (i, 0)
    tab = lambda i: (i % n_tab_blocks, 0)
    tabt = lambda i: (0, i % n_tab_blocks)
    in_specs = [
        pl.BlockSpec((tm, d), row),
        _resident((1, d)),
        _resident(w_rows.shape),
        pl.BlockSpec((tm, LANES), tab), pl.BlockSpec((tm, LANES), tab),
        pl.BlockSpec((tm, LANES), tab), pl.BlockSpec((tm, LANES), tab),
        _resident((1, D_ATT_A)), _resident((1, D_KV_A)),
        _resident((1, LANES)), _resident((1, HD_M)),
        _resident((D_ATT_A, D_ATT_A)), _resident((LANES, LANES)),
    ]
    if dsa_layout:
        in_specs += [_resident((T_END, d)),
                     pl.BlockSpec((HD_A // 2, tm), tabt), pl.BlockSpec((HD_A // 2, tm), tabt),
                     pl.BlockSpec((D_IDX // 2, tm), tabt), pl.BlockSpec((D_IDX // 2, tm), tabt),
                     _resident((HD_A, LANES))]
    rows = lambda width, dtype: (jax.ShapeDtypeStruct((n, width), dtype), pl.BlockSpec((tm, width), row))
    heads = (jax.ShapeDtypeStruct((n, N_KV_A, HD_A), F32), pl.BlockSpec((tm, N_KV_A, HD_A), lambda i: (i, 0, 0)))
    per_token = [heads, heads, rows(LANES, F32), rows(D_POOL, F32),
                 rows(D_MEM_ATT, BF16)]
    if dsa_layout:
        t = tm * n_tab_blocks
        cols = lambda i: (i // n_tab_blocks, 0, i % n_tab_blocks)
        tposed = lambda r, dtype: (jax.ShapeDtypeStruct((nb, r, t), dtype), pl.BlockSpec((None, r, tm), cols))
        outs = per_token + [
            tposed(D_ATT_A, BF16), tposed(D_QIDX, BF16), tposed(N_IDX_HEADS, F32),
            (jax.ShapeDtypeStruct((N_KV_A, n, LANES), BF16), pl.BlockSpec((N_KV_A, tm, LANES), lambda i: (0, i, 0))),
            (jax.ShapeDtypeStruct((nb, N_KV_A, n_tab_blocks, V_ROWS, tm), BF16),
             pl.BlockSpec((None, N_KV_A, None, V_ROWS, tm),
                          lambda i: (i // n_tab_blocks, 0, i % n_tab_blocks, 0, 0))),
            rows(D_IDX, BF16),
        ]
    else:
        outs = [rows(D_ATT_A, BF16)] + per_token[:2] + [rows(D_QIDX, BF16)] + per_token[2:]
    return pl.pallas_call(
        functools.partial(_proj_kernel, dsa_layout=dsa_layout), grid=grid, in_specs=in_specs,
        out_specs=[o[1] for o in outs], out_shape=[o[0] for o in outs],
        compiler_params=_cparams(("parallel",)), name="proj",
    )(x2d, gmix, w_rows, *tabs, gqa_t, gka_t, gki_t, gqm, bd64, bdki, *tposed_side)


def _key_bits_to_float(u):
    k = u ^ jnp.int32(-2147483648)
    bits = jnp.where(k >= 0, k, k ^ jnp.int32(0x7FFFFFFF))
    return lax.bitcast_convert_type(bits, F32)


def _dsa_kernel(qat_ref, qit_ref, wit_ref, k_ref, vt_ref, ki_ref, o_ref,
                sc_ref, m_ref, acc_ref, qaug_ref, qis_ref, km_ref,
                *, tq, kb, nkb_total, causal, s_valid, topk, rows_out):
    qblk = pl.program_id(1)
    if causal:
        nkb = lax.div(qblk + kb // tq, jnp.int32(kb // tq))
    else:
        nkb = nkb_total
    qpos = qblk * tq + lax.broadcasted_iota(jnp.int32, (1, tq), 1)
    if causal:
        lim = (lax.shift_right_logical(qpos, 6) + 1) * CHUNK
    else:
        lim = jnp.full((1, tq), s_valid, jnp.int32)
    limf = lim.astype(F32)
    krow = lax.broadcasted_iota(jnp.int32, (kb, tq), 0)
    kf = float(topk)

    def for_each_block_pairwise(fn):
        n = jnp.int32(nkb)

        def step(jj, carry):
            fn(2 * jj)
            fn(2 * jj + 1)
            return carry
        lax.fori_loop(0, lax.div(n, 2), step, 0)

        @pl.when(lax.rem(n, 2) == 1)
        def _():
            fn(n - 1)

    for h in range(N_IDX_HEADS):
        qis_ref[:, h * tq:(h + 1) * tq] = qit_ref[h * D_IDX:(h + 1) * D_IDX, :]
    qit = qis_ref[...]
    wit = wit_ref[...]

    def score_block(j):
        z = jnp.dot(ki_ref[j], qit, preferred_element_type=F32)
        s = jnp.zeros((kb, tq), F32)
        for h in range(N_IDX_HEADS):
            s = s + wit[h:h + 1, :] * jnp.maximum(z[:, h * tq:(h + 1) * tq], 0.0)
        sc_ref[j] = jnp.where(j * kb + krow < lim, s, -jnp.inf)

    for_each_block_pairwise(score_block)

    crow = lax.broadcasted_iota(jnp.int32, (COUNT_ROWS, tq), 0)

    def count(pred):
        def tally(j, accs):
            accs = list(accs)
            for ci, r in enumerate(range(0, kb, COUNT_ROWS)):
                hit = pred(sc_ref[j, r:r + COUNT_ROWS, :], j * kb + r + crow)
                accs[ci % COUNT_CHAINS] = accs[ci % COUNT_CHAINS] + jnp.where(hit, 1.0, 0.0)
            return tuple(accs)

        def tally_group(jj, accs):
            for w in range(COUNT_BLOCKS_PER_STEP):
                accs = tally(COUNT_BLOCKS_PER_STEP * jj + w, accs)
            return accs

        n = jnp.int32(nkb)
        groups = lax.div(n, COUNT_BLOCKS_PER_STEP)
        zero = jnp.zeros((COUNT_ROWS, tq), F32)
        accs = lax.fori_loop(0, groups, tally_group, (zero,) * COUNT_CHAINS)
        accs = lax.fori_loop(groups * COUNT_BLOCKS_PER_STEP, n, tally, accs)
        return jnp.sum(sum(accs[1:], accs[0]), axis=0, keepdims=True)

    def radix_body(i, carry):
        prefix, cge, cgt = carry
        cand = prefix | lax.shift_left(jnp.int32(1), KEY_BITS - 1 - i)
        t = _key_bits_to_float(cand)
        c = count(lambda s, idx: s >= t)
        take = c >= kf
        return jnp.where(take, cand, prefix), jnp.where(take, c, cge), jnp.where(take, cgt, c)

    prefix, cge, cgt = lax.fori_loop(
        0, KEY_BITS, radix_body, (jnp.zeros((1, tq), jnp.int32), limf, jnp.zeros((1, tq), F32)))
    thr = jnp.where(limf >= kf, _key_bits_to_float(prefix), -FLT_MAX)

    any_tie = jnp.max(jnp.where(cge > kf, 1.0, 0.0), axis=1, keepdims=True)

    def mask_plain():
        def body(j, carry):
            sc_ref[j] = jnp.where(sc_ref[j] >= thr, 0.0, MASK_VALUE)
            return carry
        lax.fori_loop(0, nkb, body, 0)

    def mask_ties():
        need = kf - cgt
        tri = (lax.broadcasted_iota(jnp.int32, (kb, kb), 1)
               <= lax.broadcasted_iota(jnp.int32, (kb, kb), 0)).astype(BF16)

        def body(j, seen):
            s = sc_ref[j]
            eq = s == thr
            rank = jnp.dot(tri, jnp.where(eq, 1.0, 0.0).astype(BF16), preferred_element_type=F32) + seen
            keep = (s > thr) | (eq & (rank <= need))
            sc_ref[j] = jnp.where(keep, 0.0, MASK_VALUE)
            return rank[kb - 1:kb, :]

        lax.fori_loop(0, nkb, body, jnp.zeros((1, tq), F32))

    lax.cond(any_tie[0, 0] > 0.0, mask_ties, mask_plain)

    @pl.when(qblk == 0)
    def _():
        for g in range(N_KV_A):
            def norm_body(j, mx, g=g):
                kk = k_ref[g, j].astype(F32)
                return jnp.maximum(mx, jnp.sum(kk * kk, axis=1, keepdims=True))
            mx = lax.fori_loop(0, nkb_total, norm_body, jnp.zeros((kb, 1), F32))
            km_ref[g] = jnp.broadcast_to(jnp.max(mx, axis=0, keepdims=True), (SUBLANES, LANES))

    aug_row = lax.broadcasted_iota(jnp.int32, (LANES - HD_A, tq), 0)
    for g in range(N_KV_A):
        kmax2 = km_ref[g][0:1, :]
        kmax2 = jnp.concatenate([kmax2] * (tq // LANES), axis=1) if tq >= LANES else kmax2[:, :tq]
        for u in range(GROUP_A):
            hd = g * GROUP_A + u
            q = qat_ref[hd * HD_A:(hd + 1) * HD_A, :]
            qf = q.astype(F32)
            shift = jnp.sqrt(jnp.sum(qf * qf, axis=0, keepdims=True) * kmax2)
            qaug_ref[g, 0:HD_A, u * tq:(u + 1) * tq] = q
            qaug_ref[g, HD_A:LANES, u * tq:(u + 1) * tq] = jnp.where(aug_row == 0, -shift, 0.0).astype(BF16)

    acc_ref[...] = jnp.zeros(acc_ref.shape, F32)

    def attend_block(j):
        bias = sc_ref[j]
        for g in range(N_KV_A):
            lg = jnp.dot(k_ref[g, j], qaug_ref[g], preferred_element_type=F32)
            vt = vt_ref[g, j]
            for u in range(GROUP_A):
                hd = g * GROUP_A + u
                p = jnp.exp2(lg[:, u * tq:(u + 1) * tq] + bias).astype(BF16)
                acc_ref[hd] += jnp.dot(vt, p, preferred_element_type=F32)

    for_each_block_pairwise(attend_block)

    sums_ok = jnp.ones((1, tq), F32)
    for hd in range(N_HEADS_A):
        sums_ok = jnp.where(acc_ref[hd][HD_A:HD_A + 1, :] > SUM_FLOOR, sums_ok, 0.0)
    all_ok = jnp.min(sums_ok, axis=1, keepdims=True)

    @pl.when(all_ok[0, 0] < 1.0)
    def _():
        m_ref[...] = jnp.full(m_ref.shape, MASK_VALUE, F32)
        acc_ref[...] = jnp.zeros(acc_ref.shape, F32)

        def online_body(j, carry):
            bias = sc_ref[j]
            for g in range(N_KV_A):
                lg = jnp.dot(k_ref[g, j], qaug_ref[g], preferred_element_type=F32)
                vt = vt_ref[g, j]
                for u in range(GROUP_A):
                    hd = g * GROUP_A + u
                    l = lg[:, u * tq:(u + 1) * tq] + bias
                    m_old = m_ref[hd:hd + 1, :]
                    m_new = jnp.maximum(m_old, jnp.max(l, axis=0, keepdims=True))
                    p = jnp.exp2(l - m_new).astype(BF16)
                    pv = jnp.dot(vt, p, preferred_element_type=F32)
                    acc_ref[hd] = jnp.exp2(m_old - m_new) * acc_ref[hd] + pv
                    m_ref[hd:hd + 1, :] = m_new
            return carry

        lax.fori_loop(0, nkb, online_body, 0)

    heads = []
    for hd in range(N_HEADS_A):
        a = acc_ref[hd]
        heads.append(a[:HD_A, :] / a[HD_A:HD_A + 1, :])
    out_t = jnp.concatenate(heads, axis=0)
    o_ref[...] = (out_t.T if rows_out else out_t).astype(BF16)


def _dsa_call(qat, qit, wit, k, vt, ki, *, tq, kb, causal, s_valid, topk):
    b, t = qat.shape[0], qat.shape[2]
    nkb_total = k.shape[2]
    rows_out = tq % LANES == 0
    kern = functools.partial(_dsa_kernel, tq=tq, kb=kb, nkb_total=nkb_total, causal=causal,
                             s_valid=s_valid, topk=topk, rows_out=rows_out)
    qcols = lambda i, j: (i, 0, j)
    in_specs = [
        pl.BlockSpec((None, D_ATT_A, tq), qcols),
        pl.BlockSpec((None, D_QIDX, tq), qcols),
        pl.BlockSpec((None, N_IDX_HEADS, tq), qcols),
        pl.BlockSpec((N_KV_A, None, nkb_total, kb, LANES), lambda i, j: (0, i, 0, 0, 0)),
        pl.BlockSpec((None, N_KV_A, nkb_total, V_ROWS, kb), lambda i, j: (i, 0, 0, 0, 0)),
        pl.BlockSpec((None, nkb_total, kb, D_IDX), lambda i, j: (i, 0, 0, 0)),
    ]
    if rows_out:
        out_spec = pl.BlockSpec((None, tq, D_ATT_A), lambda i, j: (i, j, 0))
        out_shape = jax.ShapeDtypeStruct((b, t, D_ATT_A), BF16)
    else:
        out_spec = pl.BlockSpec((None, D_ATT_A, tq), qcols)
        out_shape = jax.ShapeDtypeStruct((b, D_ATT_A, t), BF16)
    return pl.pallas_call(
        kern, grid=(b, t // tq), in_specs=in_specs, out_specs=out_spec, out_shape=out_shape,
        scratch_shapes=[pltpu.VMEM((nkb_total, kb, tq), F32),
                        pltpu.VMEM((N_HEADS_A, tq), F32),
                        pltpu.VMEM((N_HEADS_A, V_ROWS, tq), F32),
                        pltpu.VMEM((N_KV_A, LANES, GROUP_A * tq), BF16),
                        pltpu.VMEM((D_IDX, N_IDX_HEADS * tq), BF16),
                        pltpu.VMEM((N_KV_A, SUBLANES, LANES), F32)],
        compiler_params=_cparams(("parallel", "arbitrary")), name="dsa",
    )(qat, qit, wit, k, vt, ki)


def _memkv_kernel(mem_ref, gmem_ref, w_ref, gkm_ref, k_ref, v_ref):
    h = _rms(mem_ref[...], gmem_ref[...]).astype(BF16)
    kv = jnp.dot(h, w_ref[...], preferred_element_type=F32)
    gkm = gkm_ref[...]
    for hd in range(N_HEADS_M):
        k_ref[:, hd * HD_M:(hd + 1) * HD_M] = _rms(kv[:, hd * HD_M:(hd + 1) * HD_M], gkm)
    v_ref[...] = kv[:, D_MEM_ATT:]


def _memkv_call(mem2d, gmem, w_kv, gkm, tm):
    n, d = mem2d.shape
    row = lambda i: (i, 0)
    return pl.pallas_call(
        _memkv_kernel, grid=(n // tm,),
        in_specs=[pl.BlockSpec((tm, d), row), _resident((1, d)),
                  _resident((d, 2 * D_MEM_ATT)), _resident((1, HD_M))],
        out_specs=[pl.BlockSpec((tm, D_MEM_ATT), row), pl.BlockSpec((tm, D_MEM_ATT), row)],
        out_shape=[jax.ShapeDtypeStruct((n, D_MEM_ATT), F32), jax.ShapeDtypeStruct((n, D_MEM_ATT), F32)],
        compiler_params=_cparams(("parallel",)), name="memkv",
    )(mem2d, gmem, w_kv, gkm)


def _merge_kernel(x_ref, a_ref, ub_ref, prev_ref, hist0_ref, qm_ref, mkt_ref, mv_ref,
                  gmix_ref, wg_ref, woa_ref, wob_ref, wom_ref, wout_ref, wpool_ref, spool_ref,
                  o_ref, ext_ref, *, tm, pos0):
    it = pl.program_id(1)
    x = x_ref[...]

    h = _rms(x, gmix_ref[...]).astype(BF16)
    gates = jax.nn.sigmoid(jnp.dot(h, wg_ref[...], preferred_element_type=F32))
    d = x.shape[1]

    ub = ub_ref[...]
    ext_ref[0:HIST_ROWS, :] = jnp.where(it == 0, hist0_ref[...], prev_ref[...])
    ext_ref[HIST_ROWS:HIST_ROWS + tm, :] = ub
    pos = pos0 + it * tm + lax.broadcasted_iota(jnp.int32, (tm, 1), 0)
    ys = []
    for g, w in enumerate(POOL_WINDOWS):
        c0 = g * POOL_GROUP
        win = ub[:, c0:c0 + POOL_GROUP]
        for k in range(1, w):
            win = win + ext_ref[HIST_ROWS - k:HIST_ROWS - k + tm, c0:c0 + POOL_GROUP]
        cnt = jnp.minimum(w, pos + 1).astype(F32)
        pg = (win / cnt - ub[:, c0:c0 + POOL_GROUP]).astype(BF16)
        ys.append(jnp.dot(pg, wpool_ref[g], preferred_element_type=F32))
    bmix = (jnp.concatenate(ys, axis=1) * spool_ref[...]).astype(BF16)

    qm = qm_ref[...]
    ms = []
    for hd in range(N_HEADS_M):
        lg = jnp.dot(qm[:, hd * HD_M:(hd + 1) * HD_M], mkt_ref[hd], preferred_element_type=F32)
        lg = lg * (HD_M ** -0.5)
        e = jnp.exp(lg - jnp.max(lg, axis=1, keepdims=True))
        pr = (e / jnp.sum(e, axis=1, keepdims=True)).astype(BF16)
        ms.append(jnp.dot(pr, mv_ref[hd], preferred_element_type=F32))
    mmix = jnp.concatenate(ms, axis=1).astype(BF16)

    mixed = (gates[:, 0:d] * jnp.dot(a_ref[...], woa_ref[...], preferred_element_type=F32)
             + gates[:, d:2 * d] * jnp.dot(bmix, wob_ref[...], preferred_element_type=F32)
             + gates[:, 2 * d:3 * d] * jnp.dot(mmix, wom_ref[...], preferred_element_type=F32))
    o_ref[...] = x + jnp.dot(mixed.astype(BF16), wout_ref[...], preferred_element_type=F32)


def _merge_call(x, a, ub, hist0, qm, mkt, mv, gmix, wg, woa, wob, wom, wout, wpool, spool, *, tm, pos0):
    b, t, d = x.shape
    nt = t // tm
    hb = tm // HIST_ROWS
    tok = lambda i, j: (i, j, 0)
    per_b3 = lambda i, j: (i, 0, 0)
    per_b4 = lambda i, j: (i, 0, 0, 0)
    in_specs = [
        pl.BlockSpec((None, tm, d), tok),
        pl.BlockSpec((None, tm, D_ATT_A), tok),
        pl.BlockSpec((None, tm, D_POOL), tok),
        pl.BlockSpec((None, HIST_ROWS, D_POOL), lambda i, j: (i, jnp.maximum(j * hb - 1, 0), 0)),
        pl.BlockSpec((None, HIST_ROWS, D_POOL), per_b3),
        pl.BlockSpec((None, tm, D_MEM_ATT), tok),
        pl.BlockSpec((None, N_HEADS_M, HD_M, mkt.shape[3]), per_b4),
        pl.BlockSpec((None, N_HEADS_M, mv.shape[2], HD_M), per_b4),
        _resident((1, d)),
        _resident(wg.shape),
        _resident(woa.shape), _resident(wob.shape), _resident(wom.shape),
        _resident(wout.shape),
        _resident(wpool.shape),
        _resident((1, D_POOL)),
    ]
    return pl.pallas_call(
        functools.partial(_merge_kernel, tm=tm, pos0=pos0), grid=(b, nt), in_specs=in_specs,
        out_specs=pl.BlockSpec((None, tm, d), tok),
        out_shape=jax.ShapeDtypeStruct((b, t, d), F32),
        scratch_shapes=[pltpu.VMEM((HIST_ROWS + tm, D_POOL), F32)],
        compiler_params=_cparams(("parallel", "arbitrary")), name="merge",
    )(x, a, ub, ub, hist0, qm, mkt, mv, gmix, wg, woa, wob, wom, wout, wpool, spool)


def _ffn_kernel(x_ref, g_ref, wgate_ref, wup_ref, wdown_ref, o_ref):
    x = x_ref[...]
    h = _rms(x, g_ref[...]).astype(BF16)
    gate = jnp.dot(h, wgate_ref[...], preferred_element_type=F32)
    up = jnp.dot(h, wup_ref[...], preferred_element_type=F32)
    act = (jax.nn.silu(gate) * up).astype(BF16)
    o_ref[...] = x + jnp.dot(act, wdown_ref[...], preferred_element_type=F32)


def _ffn_call(x2d, g, wgate, wup, wdown, tm):
    n, d = x2d.shape
    row = lambda i: (i, 0)
    return pl.pallas_call(
        _ffn_kernel, grid=(n // tm,),
        in_specs=[pl.BlockSpec((tm, d), row), _resident((1, d)),
                  _resident(wgate.shape), _resident(wup.shape), _resident(wdown.shape)],
        out_specs=pl.BlockSpec((tm, d), row),
        out_shape=jax.ShapeDtypeStruct((n, d), F32),
        compiler_params=_cparams(("parallel",)), name="ffn",
    )(x2d, g, wgate, wup, wdown)


def _rope_tables(pos, head_dim):
    half = head_dim // 2
    inv = ROPE_THETA ** (-jnp.arange(half, dtype=F32) / half)
    ang = pos.astype(F32)[:, None] * inv[None, :]
    reps = LANES // half
    return jnp.tile(jnp.cos(ang), (1, reps)), jnp.tile(jnp.sin(ang), (1, reps))


def _rope_tables_t(pos, head_dim):
    half = head_dim // 2
    inv = ROPE_THETA ** (-jnp.arange(half, dtype=F32) / half)
    ang = inv[:, None] * pos.astype(F32)[None, :]
    return jnp.cos(ang), jnp.sin(ang)


def _pack_weights(w_in, g_qa, g_ka, g_kidx, g_qm):
    d = w_in.shape[0]
    widths = (D_ATT_A, D_KV_A, D_KV_A, D_QIDX, D_IDX, N_IDX_HEADS, D_POOL, D_MEM_ATT, N_BRANCH * d)
    cuts = [int(c) for c in np.cumsum(widths)[:-1]]
    wqa, wka, wva, wqi, wki, wwi, wub, wqm, wgates = jnp.split(w_in, cuts, axis=1)
    pad = jnp.zeros((d, LANES - D_IDX - N_IDX_HEADS), w_in.dtype)
    w_cat = jnp.concatenate([wqa, wka, wva, wqi, wki, wwi, pad, wub, wqm], axis=1).astype(BF16)
    w_rows = jnp.concatenate([wka, wva, wki, wwi, pad, wub, wqm], axis=1).astype(BF16)
    wpad = jnp.zeros((d, T_END - T_WI - N_IDX_HEADS), w_in.dtype)
    w_t = jnp.concatenate([wqa, wqi, wva, wwi, wpad], axis=1).T.astype(BF16)
    gqa_t = jnp.tile(g_qa, N_HEADS_A)[None, :]
    gka_t = jnp.tile(g_ka, N_KV_A)[None, :]
    gki_t = jnp.concatenate([g_kidx, jnp.ones((LANES - D_IDX,), g_kidx.dtype)])[None, :]
    gqa_b = jnp.broadcast_to(g_qa[:, None], (HD_A, LANES))
    return w_cat, w_rows, w_t, wgates.astype(BF16), gqa_t, gka_t, gki_t, g_qm[None, :], gqa_b


def _block_diag_ones(n, group, limit=None):
    i = np.arange(n)
    m = (i[:, None] // group) == (i[None, :] // group)
    if limit is not None:
        m = m & (i[:, None] < limit) & (i[None, :] < limit)
    return jnp.asarray(m, BF16)


def _key_blocks(k_all, v_all, ki_all, kb):
    b, s = k_all.shape[0], k_all.shape[1]
    nkb = -(-s // kb)
    pad = nkb * kb - s
    if pad:
        k_all = jnp.pad(k_all, ((0, 0), (0, pad), (0, 0), (0, 0)))
        v_all = jnp.pad(v_all, ((0, 0), (0, pad), (0, 0), (0, 0)))
        ki_all = jnp.pad(ki_all, ((0, 0), (0, pad), (0, 0)))
    ones = jnp.ones(v_all.shape[:-1] + (1,), BF16)
    zeros = jnp.zeros(v_all.shape[:-1] + (LANES - HD_A - 1,), BF16)
    kaug = jnp.concatenate([k_all.astype(BF16), ones, zeros], axis=-1)
    k = kaug.reshape(b, nkb, kb, N_KV_A, LANES).transpose(3, 0, 1, 2, 4)
    vaug = jnp.concatenate([v_all.astype(BF16), ones, zeros[..., :V_ROWS - HD_A - 1]], axis=-1)
    vt = vaug.reshape(b, nkb, kb, N_KV_A, V_ROWS).transpose(0, 3, 1, 4, 2)
    ki = ki_all.astype(BF16).reshape(b, nkb, kb, D_IDX)
    return k, vt, ki


def _tile(n, pref):
    t = pref
    while n % t:
        t //= 2
    return t


def _group_forward(x, pos, pos0, hist0, k_hist, v_hist, ki_hist, mkt, mv, wts, *, causal):
    (gmix, w_cat, w_rows, w_t, gqa_b, wgates, gqa_t, gka_t, gki_t, gqm, bd64, bdki, wpool, spool,
     woa, wob, wom, wout, gffn, wgate, wup, wdown) = wts
    b, t, d = x.shape
    n = b * t
    tm = _tile(t, 512)
    cosa, sina = _rope_tables(pos, HD_A)
    cosi, sini = _rope_tables(pos, D_IDX)
    tq = _tile(t, 256)
    proj_args = ((cosa, sina, cosi, sini), gqa_t, gka_t, gki_t, gqm, bd64, bdki, tm, t // tm)
    if causal:
        tside = (w_t,) + _rope_tables_t(pos, HD_A) + _rope_tables_t(pos, D_IDX) + (gqa_b,)
        ka, va, kw, ub, qm, qat, qit, wit, kaug, vt, kib = _proj_call(
            x.reshape(n, d), gmix, w_rows, *proj_args, dsa_layout=True, tposed_side=tside)
        kb, s = tm, t
        kblk = kaug.reshape(N_KV_A, b, t // kb, kb, LANES)
        kiblk = kib.reshape(b, t // kb, kb, D_IDX)
    else:
        qa, ka, va, qi, kw, ub, qm = _proj_call(x.reshape(n, d), gmix, w_cat, *proj_args, dsa_layout=False)
        qat = qa.reshape(b, t, D_ATT_A).transpose(0, 2, 1)
        qit = qi.reshape(b, t, D_QIDX).transpose(0, 2, 1)
        wit = kw[:, D_IDX:D_IDX + N_IDX_HEADS].reshape(b, t, N_IDX_HEADS).transpose(0, 2, 1)
    ka = ka.reshape(b, t, N_KV_A, HD_A)
    va = va.reshape(b, t, N_KV_A, HD_A)
    ki = kw[:, :D_IDX].reshape(b, t, D_IDX)
    ub = ub.reshape(b, t, D_POOL)
    if not causal:
        k_all = jnp.concatenate([k_hist, ka], axis=1)
        s, kb = k_all.shape[1], 256
        kblk, vt, kiblk = _key_blocks(k_all, jnp.concatenate([v_hist, va], axis=1),
                                      jnp.concatenate([ki_hist, ki], axis=1), kb)
    topk = min(TOPK_MAX, s // 4)
    a = _dsa_call(qat, qit, wit, kblk, vt, kiblk, tq=tq, kb=kb, causal=causal, s_valid=s, topk=topk)
    if tq % LANES:
        a = a.transpose(0, 2, 1)

    tmm = _tile(t, 512)
    x2 = _merge_call(x, a, ub, hist0, qm.reshape(b, t, D_MEM_ATT), mkt, mv, gmix, wgates,
                     woa, wob, wom, wout, wpool, spool, tm=tmm, pos0=pos0)
    y = _ffn_call(x2.reshape(n, d), gffn, wgate, wup, wdown, _tile(n, 512)).reshape(b, t, d)
    return y, ka, va, ki, ub


def kernel(x_prompt, x_sample, mem_prompt, cache_a_k, cache_a_v, cache_idx_k, cache_pool, cache_mem_k,
           cache_mem_v, g_mix, w_in, g_qa, g_ka, g_kidx, g_qm, g_mem, w_mem_kv, g_km, w_pool, s_pool,
           w_oa, w_ob, w_om, w_out, g_ffn, w_gate, w_up, w_down):
    depth = w_in.shape[0]
    t = x_prompt.shape[1]
    ts = x_sample.shape[1]
    past = cache_a_k.shape[2]
    b = x_prompt.shape[0]
    n_mem = mem_prompt.shape[1]
    pos_p = jnp.arange(t, dtype=jnp.int32)
    pos_s = past + jnp.arange(ts, dtype=jnp.int32)
    bd64 = _block_diag_ones(D_ATT_A, HD_A)
    bdki = _block_diag_ones(LANES, LANES, limit=D_IDX)

    xp, xs = x_prompt, x_sample
    outs = [[] for _ in range(10)]
    for l in range(depth):
        w_cat, w_rows, w_t, wgates, gqa_t, gka_t, gki_t, gqm, gqa_b = _pack_weights(
            w_in[l], g_qa[l], g_ka[l], g_kidx[l], g_qm[l])
        wts = (g_mix[l][None, :], w_cat, w_rows, w_t, gqa_b, wgates, gqa_t, gka_t, gki_t, gqm, bd64, bdki,
               w_pool[l].astype(BF16), s_pool[l][None, :],
               w_oa[l].astype(BF16), w_ob[l].astype(BF16), w_om[l].astype(BF16), w_out[l].astype(BF16),
               g_ffn[l][None, :], w_gate[l].astype(BF16), w_up[l].astype(BF16), w_down[l].astype(BF16))

        mk, mv = _memkv_call(mem_prompt.reshape(b * n_mem, -1), g_mem[l][None, :],
                             w_mem_kv[l].astype(BF16), g_km[l][None, :], _tile(b * n_mem, 256))
        mk = mk.reshape(b, n_mem, N_HEADS_M, HD_M)
        mv = mv.reshape(b, n_mem, N_HEADS_M, HD_M)
        hist0 = jnp.zeros((b, HIST_ROWS, D_POOL), F32)
        xp, ka, va, ki, ub = _group_forward(
            xp, pos_p, 0, hist0, None, None, None,
            mk.astype(BF16).transpose(0, 2, 3, 1), mv.astype(BF16).transpose(0, 2, 1, 3), wts, causal=True)
        for lst, val in zip(outs[:6], (ka, va, ki, ub[:, -POOL_HIST:], mk, mv)):
            lst.append(val)

        bs = xs.shape[0]
        hist0 = jnp.concatenate([jnp.zeros((bs, HIST_ROWS - POOL_HIST, D_POOL), F32), cache_pool[l]], axis=1)
        xs, ka, va, ki, ub = _group_forward(
            xs, pos_s, past, hist0, cache_a_k[l], cache_a_v[l], cache_idx_k[l],
            cache_mem_k[l].astype(BF16).transpose(0, 2, 3, 1), cache_mem_v[l].astype(BF16).transpose(0, 2, 1, 3),
            wts, causal=False)
        pool_s = jnp.concatenate([cache_pool[l], ub], axis=1)[:, -POOL_HIST:]
        for lst, val in zip(outs[6:], (ka, va, ki, pool_s)):
            lst.append(val)

    stacked = [jnp.stack(o) for o in outs]
    return (xp, xs, *stacked)
```

```python
import functools

import jax
import jax.numpy as jnp
import numpy as np
from jax import lax
from jax.experimental import pallas as pl
from jax.experimental.pallas import tpu as pltpu

F32 = jnp.float32
BF16 = jnp.bfloat16

CHUNK = 64
EPS = 1e-6
ROPE_THETA = 10000.0
N_HEADS_A = 8
N_KV_A = 2
GROUP_A = N_HEADS_A // N_KV_A
HD_A = 64
D_ATT_A = N_HEADS_A * HD_A
D_KV_A = N_KV_A * HD_A
N_IDX_HEADS = 8
D_IDX = 32
D_QIDX = N_IDX_HEADS * D_IDX
IDX_SCALE = (N_IDX_HEADS * D_IDX) ** -0.5
TOPK_MAX = 256
POOL_WINDOWS = (2, 4, 8, 16)
POOL_GROUP = 128
D_POOL = len(POOL_WINDOWS) * POOL_GROUP
POOL_HIST = 15
N_HEADS_M = 4
HD_M = 128
D_MEM_ATT = N_HEADS_M * HD_M
N_BRANCH = 3

LANES = 128
SUBLANES = 8
V7X_VMEM_BYTES = 64 * 1024 * 1024
VMEM_LIMIT = 56 * 1024 * 1024

HIST_ROWS = 2 * SUBLANES
MASK_VALUE = -1e30
FLT_MAX = float(np.finfo(np.float32).max)
KEY_BITS = 32
Q_SCALE = HD_A ** -0.5 * float(np.log2(np.e))
SUM_FLOOR = 1e-30
COUNT_ROWS = 4 * SUBLANES
COUNT_CHAINS = 2
COUNT_BLOCKS_PER_STEP = 4
MXU_BLOCKS_PER_STEP = 4
V_ROWS = HD_A + 2 * SUBLANES

C_QA = 0
C_KA = C_QA + D_ATT_A
C_VA = C_KA + D_KV_A
C_QI = C_VA + D_KV_A
C_KW = C_QI + D_QIDX
C_UB = C_KW + LANES
C_QM = C_UB + D_POOL
C_END = C_QM + D_MEM_ATT

R_KA = 0
R_VA = R_KA + D_KV_A
R_KW = R_VA + D_KV_A
R_UB = R_KW + LANES
R_QM = R_UB + D_POOL
R_END = R_QM + D_MEM_ATT
T_QA = 0
T_QI = T_QA + D_ATT_A
T_VA = T_QI + D_QIDX
T_WI = T_VA + D_KV_A
T_END = T_WI + 2 * SUBLANES


def _cparams(sem):
    return pltpu.CompilerParams(dimension_semantics=sem, vmem_limit_bytes=VMEM_LIMIT)


def _resident(shape):
    zeros = (0,) * len(shape)
    return pl.BlockSpec(shape, lambda *_: zeros, pipeline_mode=pl.Buffered(1))


def _rms(x, g):
    ms = jnp.mean(x * x, axis=-1, keepdims=True)
    return x * lax.rsqrt(ms + EPS) * g


def _group_sumsq(x, bd):
    sq = x * x
    hi = sq.astype(BF16)
    lo = (sq - hi.astype(F32)).astype(BF16)
    return (jnp.dot(hi, bd, preferred_element_type=F32)
            + jnp.dot(lo, bd, preferred_element_type=F32))


def _rope_lanes(x, cos, sin, half):
    lane = lax.broadcasted_iota(jnp.int32, x.shape, 1)
    first = (lane % (2 * half)) < half
    left = pltpu.roll(x, LANES - half, 1)
    right = pltpu.roll(x, half, 1)
    rot = jnp.where(first, -left, right)
    return x * cos + rot * sin


def _proj_kernel(x_ref, gmix_ref, w_ref, cosa_ref, sina_ref, cosi_ref, sini_ref,
                 gqa_ref, gka_ref, gki_ref, gqm_ref, bd64_ref, bdki_ref, *refs, dsa_layout):
    if dsa_layout:
        (wt_ref, cosat_ref, sinat_ref, cosit_ref, sinit_ref, gqab_ref,
         ka_ref, va_ref, kw_ref, ub_ref, qm_ref, qat_ref, qit_ref, wit_ref, kaug_ref, vt_ref, kib_ref) = refs
        c_ka, c_va, c_kw, c_ub, c_qm = R_KA, R_VA, R_KW, R_UB, R_QM
    else:
        qa_ref, ka_ref, va_ref, qi_ref, kw_ref, ub_ref, qm_ref = refs
        c_ka, c_va, c_kw, c_ub, c_qm = C_KA, C_VA, C_KW, C_UB, C_QM
    x = x_ref[...]
    tm = x.shape[0]
    h = _rms(x, gmix_ref[...]).astype(BF16)
    p = jnp.dot(h, w_ref[...], preferred_element_type=F32)
    cosa, sina = cosa_ref[...], sina_ref[...]
    cosi, sini = cosi_ref[...], sini_ref[...]
    bd64 = bd64_ref[...]

    if dsa_layout:
        pt = lax.dot_general(wt_ref[...], h, (((1,), (1,)), ((), ())), preferred_element_type=F32)
        cosat, sinat = cosat_ref[...], sinat_ref[...]
        gq = jnp.concatenate([gqab_ref[...]] * (tm // LANES), axis=1)
        ha = HD_A // 2
        for hd in range(N_HEADS_A):
            xh = pt[T_QA + hd * HD_A:T_QA + (hd + 1) * HD_A, :]
            xh = xh * lax.rsqrt(jnp.sum(xh * xh, axis=0, keepdims=True) * (1.0 / HD_A) + EPS) * gq
            x1, x2 = xh[:ha, :], xh[ha:, :]
            qat_ref[hd * HD_A:hd * HD_A + ha, :] = ((x1 * cosat - x2 * sinat) * Q_SCALE).astype(BF16)
            qat_ref[hd * HD_A + ha:(hd + 1) * HD_A, :] = ((x2 * cosat + x1 * sinat) * Q_SCALE).astype(BF16)
        cosit, sinit = cosit_ref[...], sinit_ref[...]
        hi = D_IDX // 2
        for hd in range(N_IDX_HEADS):
            x1 = pt[T_QI + hd * D_IDX:T_QI + hd * D_IDX + hi, :]
            x2 = pt[T_QI + hd * D_IDX + hi:T_QI + (hd + 1) * D_IDX, :]
            qit_ref[hd * D_IDX:hd * D_IDX + hi, :] = (x1 * cosit - x2 * sinit).astype(BF16)
            qit_ref[hd * D_IDX + hi:(hd + 1) * D_IDX, :] = (x2 * cosit + x1 * sinit).astype(BF16)
        wit_ref[...] = pt[T_WI:T_WI + N_IDX_HEADS, :] * IDX_SCALE
    else:
        qa = p[:, C_QA:C_QA + D_ATT_A]
        ss = _group_sumsq(qa, bd64)
        qa = qa * lax.rsqrt(ss * (1.0 / HD_A) + EPS) * gqa_ref[...]
        for c in range(D_ATT_A // LANES):
            blk = _rope_lanes(qa[:, c * LANES:(c + 1) * LANES], cosa, sina, HD_A // 2) * Q_SCALE
            qa_ref[:, c * LANES:(c + 1) * LANES] = blk.astype(BF16)
        for c in range(D_QIDX // LANES):
            blk = _rope_lanes(p[:, C_QI + c * LANES:C_QI + (c + 1) * LANES], cosi, sini, D_IDX // 2)
            qi_ref[:, c * LANES:(c + 1) * LANES] = blk.astype(BF16)

    ka = p[:, c_ka:c_ka + D_KV_A]
    ss = _group_sumsq(ka, bd64[:D_KV_A, :D_KV_A])
    ka = ka * lax.rsqrt(ss * (1.0 / HD_A) + EPS) * gka_ref[...]
    ka = _rope_lanes(ka, cosa, sina, HD_A // 2)
    va = p[:, c_va:c_va + D_KV_A]
    for g in range(N_KV_A):
        ka_ref[:, g, :] = (ka if g == 0 else pltpu.roll(ka, LANES - g * HD_A, 1))[:, :HD_A]
        va_ref[:, g, :] = (va if g == 0 else pltpu.roll(va, LANES - g * HD_A, 1))[:, :HD_A]
    if dsa_layout:
        lane = lax.broadcasted_iota(jnp.int32, ka.shape, 1)
        tail = jnp.where(lane == HD_A, 1.0, 0.0)
        kaug_ref[0] = jnp.where(lane < HD_A, ka, tail).astype(BF16)
        kaug_ref[1] = jnp.where(lane < HD_A, pltpu.roll(ka, HD_A, 1), tail).astype(BF16)
        srow = lax.broadcasted_iota(jnp.int32, (V_ROWS - HD_A, tm), 0)
        ones_rows = jnp.where(srow == 0, 1.0, 0.0).astype(BF16)
        for g in range(N_KV_A):
            vt_ref[g, 0:HD_A, :] = pt[T_VA + g * HD_A:T_VA + (g + 1) * HD_A, :].astype(BF16)
            vt_ref[g, HD_A:V_ROWS, :] = ones_rows

    kw = p[:, c_kw:c_kw + LANES]
    ss = _group_sumsq(kw, bdki_ref[...])
    kin = kw * lax.rsqrt(ss * (1.0 / D_IDX) + EPS) * gki_ref[...]
    kin = _rope_lanes(kin, cosi, sini, D_IDX // 2)
    lane = lax.broadcasted_iota(jnp.int32, kw.shape, 1)
    kw = jnp.where(lane < D_IDX, kin, kw * IDX_SCALE)
    kw_ref[...] = kw
    if dsa_layout:
        kib_ref[...] = kw[:, :D_IDX].astype(BF16)

    ub_ref[...] = p[:, c_ub:c_ub + D_POOL]

    gqm = gqm_ref[...]
    for hd in range(N_HEADS_M):
        blk = p[:, c_qm + hd * HD_M:c_qm + (hd + 1) * HD_M]
        qm_ref[:, hd * HD_M:(hd + 1) * HD_M] = _rms(blk, gqm).astype(BF16)


def _proj_call(x2d, gmix, w_rows, tabs, gqa_t, gka_t, gki_t, gqm, bd64, bdki, tm, n_tab_blocks,
               dsa_layout, tposed_side=()):
    n = x2d.shape[0]
    d = x2d.shape[1]
    nb = n // (tm * n_tab_blocks)
    grid = (n // tm,)
    row = lambda i: (i, 0)
    tab = lambda i: (i % n_tab_blocks, 0)
    tabt = lambda i: (0, i % n_tab_blocks)
    in_specs = [
        pl.BlockSpec((tm, d), row),
        _resident((1, d)),
        _resident(w_rows.shape),
        pl.BlockSpec((tm, LANES), tab), pl.BlockSpec((tm, LANES), tab),
        pl.BlockSpec((tm, LANES), tab), pl.BlockSpec((tm, LANES), tab),
        _resident((1, D_ATT_A)), _resident((1, D_KV_A)),
        _resident((1, LANES)), _resident((1, HD_M)),
        _resident((D_ATT_A, D_ATT_A)), _resident((LANES, LANES)),
    ]
    if dsa_layout:
        in_specs += [_resident((T_END, d)),
                     pl.BlockSpec((HD_A // 2, tm), tabt), pl.BlockSpec((HD_A // 2, tm), tabt),
                     pl.BlockSpec((D_IDX // 2, tm), tabt), pl.BlockSpec((D_IDX // 2, tm), tabt),
                     _resident((HD_A, LANES))]
    rows = lambda width, dtype: (jax.ShapeDtypeStruct((n, width), dtype), pl.BlockSpec((tm, width), row))
    heads = (jax.ShapeDtypeStruct((n, N_KV_A, HD_A), F32), pl.BlockSpec((tm, N_KV_A, HD_A), lambda i: (i, 0, 0)))
    per_token = [heads, heads, rows(LANES, F32), rows(D_POOL, F32),
                 rows(D_MEM_ATT, BF16)]
    if dsa_layout:
        t = tm * n_tab_blocks
        cols = lambda i: (i // n_tab_blocks, 0, i % n_tab_blocks)
        tposed = lambda r, dtype: (jax.ShapeDtypeStruct((nb, r, t), dtype), pl.BlockSpec((None, r, tm), cols))
        outs = per_token + [
            tposed(D_ATT_A, BF16), tposed(D_QIDX, BF16), tposed(N_IDX_HEADS, F32),
            (jax.ShapeDtypeStruct((N_KV_A, n, LANES), BF16), pl.BlockSpec((N_KV_A, tm, LANES), lambda i: (0, i, 0))),
            (jax.ShapeDtypeStruct((nb, N_KV_A, n_tab_blocks, V_ROWS, tm), BF16),
             pl.BlockSpec((None, N_KV_A, None, V_ROWS, tm),
                          lambda i: (i // n_tab_blocks, 0, i % n_tab_blocks, 0, 0))),
            rows(D_IDX, BF16),
        ]
    else:
        outs = [rows(D_ATT_A, BF16)] + per_token[:2] + [rows(D_QIDX, BF16)] + per_token[2:]
    return pl.pallas_call(
        functools.partial(_proj_kernel, dsa_layout=dsa_layout), grid=grid, in_specs=in_specs,
        out_specs=[o[1] for o in outs], out_shape=[o[0] for o in outs],
        compiler_params=_cparams(("parallel",)), name="proj",
    )(x2d, gmix, w_rows, *tabs, gqa_t, gka_t, gki_t, gqm, bd64, bdki, *tposed_side)


def _key_bits_to_float(u):
    k = u ^ jnp.int32(-2147483648)
    bits = jnp.where(k >= 0, k, k ^ jnp.int32(0x7FFFFFFF))
    return lax.bitcast_convert_type(bits, F32)


def _dsa_kernel(qat_ref, qit_ref, wit_ref, k_ref, vt_ref, ki_ref, o_ref,
                sc_ref, m_ref, acc_ref, qaug_ref, qis_ref, km_ref,
                *, tq, kb, nkb_total, causal, s_valid, topk, rows_out):
    qblk = pl.program_id(1)
    if causal:
        nkb = lax.div(qblk + kb // tq, jnp.int32(kb // tq))
    else:
        nkb = nkb_total
    qpos = qblk * tq + lax.broadcasted_iota(jnp.int32, (1, tq), 1)
    if causal:
        lim = (lax.shift_right_logical(qpos, 6) + 1) * CHUNK
    else:
        lim = jnp.full((1, tq), s_valid, jnp.int32)
    limf = lim.astype(F32)
    krow = lax.broadcasted_iota(jnp.int32, (kb, tq), 0)
    kf = float(topk)

    def for_each_block_pairwise(fn):
        n = jnp.int32(nkb)
        groups = lax.div(n, MXU_BLOCKS_PER_STEP)

        def step(jj, carry):
            for w in range(MXU_BLOCKS_PER_STEP):
                fn(MXU_BLOCKS_PER_STEP * jj + w)
            return carry
        lax.fori_loop(0, groups, step, 0)
        done = groups * MXU_BLOCKS_PER_STEP

        @pl.when(n - done >= 2)
        def _():
            fn(done)
            fn(done + 1)

        @pl.when(lax.rem(n, 2) == 1)
        def _():
            fn(n - 1)

    for h in range(N_IDX_HEADS):
        qis_ref[:, h * tq:(h + 1) * tq] = qit_ref[h * D_IDX:(h + 1) * D_IDX, :]
    qit = qis_ref[...]
    wit = wit_ref[...]

    def score_block(j):
        z = jnp.dot(ki_ref[j], qit, preferred_element_type=F32)
        s = jnp.zeros((kb, tq), F32)
        for h in range(N_IDX_HEADS):
            s = s + wit[h:h + 1, :] * jnp.maximum(z[:, h * tq:(h + 1) * tq], 0.0)
        sc_ref[j] = jnp.where(j * kb + krow < lim, s, -jnp.inf)

    for_each_block_pairwise(score_block)

    crow = lax.broadcasted_iota(jnp.int32, (COUNT_ROWS, tq), 0)

    def count(pred):
        def tally(j, accs):
            accs = list(accs)
            for ci, r in enumerate(range(0, kb, COUNT_ROWS)):
                hit = pred(sc_ref[j, r:r + COUNT_ROWS, :], j * kb + r + crow)
                accs[ci % COUNT_CHAINS] = accs[ci % COUNT_CHAINS] + jnp.where(hit, 1.0, 0.0)
            return tuple(accs)

        def tally_group(jj, accs):
            for w in range(COUNT_BLOCKS_PER_STEP):
                accs = tally(COUNT_BLOCKS_PER_STEP * jj + w, accs)
            return accs

        n = jnp.int32(nkb)
        groups = lax.div(n, COUNT_BLOCKS_PER_STEP)
        zero = jnp.zeros((COUNT_ROWS, tq), F32)
        accs = lax.fori_loop(0, groups, tally_group, (zero,) * COUNT_CHAINS)
        accs = lax.fori_loop(groups * COUNT_BLOCKS_PER_STEP, n, tally, accs)
        return jnp.sum(sum(accs[1:], accs[0]), axis=0, keepdims=True)

    def radix_body(i, carry):
        prefix, cge, cgt = carry
        cand = prefix | lax.shift_left(jnp.int32(1), KEY_BITS - 1 - i)
        t = _key_bits_to_float(cand)
        c = count(lambda s, idx: s >= t)
        take = c >= kf
        return jnp.where(take, cand, prefix), jnp.where(take, c, cge), jnp.where(take, cgt, c)

    prefix, cge, cgt = lax.fori_loop(
        0, KEY_BITS, radix_body, (jnp.zeros((1, tq), jnp.int32), limf, jnp.zeros((1, tq), F32)))
    thr = jnp.where(limf >= kf, _key_bits_to_float(prefix), -FLT_MAX)

    any_tie = jnp.max(jnp.where(cge > kf, 1.0, 0.0), axis=1, keepdims=True)

    def mask_plain():
        def body(j, carry):
            sc_ref[j] = jnp.where(sc_ref[j] >= thr, 0.0, MASK_VALUE)
            return carry
        lax.fori_loop(0, nkb, body, 0)

    def mask_ties():
        need = kf - cgt
        tri = (lax.broadcasted_iota(jnp.int32, (kb, kb), 1)
               <= lax.broadcasted_iota(jnp.int32, (kb, kb), 0)).astype(BF16)

        def body(j, seen):
            s = sc_ref[j]
            eq = s == thr
            rank = jnp.dot(tri, jnp.where(eq, 1.0, 0.0).astype(BF16), preferred_element_type=F32) + seen
            keep = (s > thr) | (eq & (rank <= need))
            sc_ref[j] = jnp.where(keep, 0.0, MASK_VALUE)
            return rank[kb - 1:kb, :]

        lax.fori_loop(0, nkb, body, jnp.zeros((1, tq), F32))

    lax.cond(any_tie[0, 0] > 0.0, mask_ties, mask_plain)

    @pl.when(qblk == 0)
    def _():
        for g in range(N_KV_A):
            def norm_body(j, mx, g=g):
                kk = k_ref[g, j].astype(F32)
                return jnp.maximum(mx, jnp.sum(kk * kk, axis=1, keepdims=True))
            mx = lax.fori_loop(0, nkb_total, norm_body, jnp.zeros((kb, 1), F32))
            km_ref[g] = jnp.broadcast_to(jnp.max(mx, axis=0, keepdims=True), (SUBLANES, LANES))

    aug_row = lax.broadcasted_iota(jnp.int32, (LANES - HD_A, tq), 0)
    for g in range(N_KV_A):
        kmax2 = km_ref[g][0:1, :]
        kmax2 = jnp.concatenate([kmax2] * (tq // LANES), axis=1) if tq >= LANES else kmax2[:, :tq]
        for u in range(GROUP_A):
            hd = g * GROUP_A + u
            q = qat_ref[hd * HD_A:(hd + 1) * HD_A, :]
            qf = q.astype(F32)
            shift = jnp.sqrt(jnp.sum(qf * qf, axis=0, keepdims=True) * kmax2)
            qaug_ref[g, 0:HD_A, u * tq:(u + 1) * tq] = q
            qaug_ref[g, HD_A:LANES, u * tq:(u + 1) * tq] = jnp.where(aug_row == 0, -shift, 0.0).astype(BF16)

    acc_ref[...] = jnp.zeros(acc_ref.shape, F32)

    def attend_block(j):
        bias = sc_ref[j]
        for g in range(N_KV_A):
            lg = jnp.dot(k_ref[g, j], qaug_ref[g], preferred_element_type=F32)
            vt = vt_ref[g, j]
            for u in range(GROUP_A):
                hd = g * GROUP_A + u
                p = jnp.exp2(lg[:, u * tq:(u + 1) * tq] + bias).astype(BF16)
                acc_ref[hd] += jnp.dot(vt, p, preferred_element_type=F32)

    for_each_block_pairwise(attend_block)

    sums_ok = jnp.ones((1, tq), F32)
    for hd in range(N_HEADS_A):
        sums_ok = jnp.where(acc_ref[hd][HD_A:HD_A + 1, :] > SUM_FLOOR, sums_ok, 0.0)
    all_ok = jnp.min(sums_ok, axis=1, keepdims=True)

    @pl.when(all_ok[0, 0] < 1.0)
    def _():
        m_ref[...] = jnp.full(m_ref.shape, MASK_VALUE, F32)
        acc_ref[...] = jnp.zeros(acc_ref.shape, F32)

        def online_body(j, carry):
            bias = sc_ref[j]
            for g in range(N_KV_A):
                lg = jnp.dot(k_ref[g, j], qaug_ref[g], preferred_element_type=F32)
                vt = vt_ref[g, j]
                for u in range(GROUP_A):
                    hd = g * GROUP_A + u
                    l = lg[:, u * tq:(u + 1) * tq] + bias
                    m_old = m_ref[hd:hd + 1, :]
                    m_new = jnp.maximum(m_old, jnp.max(l, axis=0, keepdims=True))
                    p = jnp.exp2(l - m_new).astype(BF16)
                    pv = jnp.dot(vt, p, preferred_element_type=F32)
                    acc_ref[hd] = jnp.exp2(m_old - m_new) * acc_ref[hd] + pv
                    m_ref[hd:hd + 1, :] = m_new
            return carry

        lax.fori_loop(0, nkb, online_body, 0)

    heads = []
    for hd in range(N_HEADS_A):
        a = acc_ref[hd]
        heads.append(a[:HD_A, :] / a[HD_A:HD_A + 1, :])
    out_t = jnp.concatenate(heads, axis=0)
    o_ref[...] = (out_t.T if rows_out else out_t).astype(BF16)


def _dsa_call(qat, qit, wit, k, vt, ki, *, tq, kb, causal, s_valid, topk):
    b, t = qat.shape[0], qat.shape[2]
    nkb_total = k.shape[2]
    rows_out = tq % LANES == 0
    kern = functools.partial(_dsa_kernel, tq=tq, kb=kb, nkb_total=nkb_total, causal=causal,
                             s_valid=s_valid, topk=topk, rows_out=rows_out)
    qcols = lambda i, j: (i, 0, j)
    in_specs = [
        pl.BlockSpec((None, D_ATT_A, tq), qcols),
        pl.BlockSpec((None, D_QIDX, tq), qcols),
        pl.BlockSpec((None, N_IDX_HEADS, tq), qcols),
        pl.BlockSpec((N_KV_A, None, nkb_total, kb, LANES), lambda i, j: (0, i, 0, 0, 0)),
        pl.BlockSpec((None, N_KV_A, nkb_total, V_ROWS, kb), lambda i, j: (i, 0, 0, 0, 0)),
        pl.BlockSpec((None, nkb_total, kb, D_IDX), lambda i, j: (i, 0, 0, 0)),
    ]
    if rows_out:
        out_spec = pl.BlockSpec((None, tq, D_ATT_A), lambda i, j: (i, j, 0))
        out_shape = jax.ShapeDtypeStruct((b, t, D_ATT_A), BF16)
    else:
        out_spec = pl.BlockSpec((None, D_ATT_A, tq), qcols)
        out_shape = jax.ShapeDtypeStruct((b, D_ATT_A, t), BF16)
    return pl.pallas_call(
        kern, grid=(b, t // tq), in_specs=in_specs, out_specs=out_spec, out_shape=out_shape,
        scratch_shapes=[pltpu.VMEM((nkb_total, kb, tq), F32),
                        pltpu.VMEM((N_HEADS_A, tq), F32),
                        pltpu.VMEM((N_HEADS_A, V_ROWS, tq), F32),
                        pltpu.VMEM((N_KV_A, LANES, GROUP_A * tq), BF16),
                        pltpu.VMEM((D_IDX, N_IDX_HEADS * tq), BF16),
                        pltpu.VMEM((N_KV_A, SUBLANES, LANES), F32)],
        compiler_params=_cparams(("parallel", "arbitrary")), name="dsa",
    )(qat, qit, wit, k, vt, ki)


def _memkv_kernel(mem_ref, gmem_ref, w_ref, gkm_ref, k_ref, v_ref):
    h = _rms(mem_ref[...], gmem_ref[...]).astype(BF16)
    kv = jnp.dot(h, w_ref[...], preferred_element_type=F32)
    gkm = gkm_ref[...]
    for hd in range(N_HEADS_M):
        k_ref[:, hd * HD_M:(hd + 1) * HD_M] = _rms(kv[:, hd * HD_M:(hd + 1) * HD_M], gkm)
    v_ref[...] = kv[:, D_MEM_ATT:]


def _memkv_call(mem2d, gmem, w_kv, gkm, tm):
    n, d = mem2d.shape
    row = lambda i: (i, 0)
    return pl.pallas_call(
        _memkv_kernel, grid=(n // tm,),
        in_specs=[pl.BlockSpec((tm, d), row), _resident((1, d)),
                  _resident((d, 2 * D_MEM_ATT)), _resident((1, HD_M))],
        out_specs=[pl.BlockSpec((tm, D_MEM_ATT), row), pl.BlockSpec((tm, D_MEM_ATT), row)],
        out_shape=[jax.ShapeDtypeStruct((n, D_MEM_ATT), F32), jax.ShapeDtypeStruct((n, D_MEM_ATT), F32)],
        compiler_params=_cparams(("parallel",)), name="memkv",
    )(mem2d, gmem, w_kv, gkm)


def _merge_kernel(x_ref, a_ref, ub_ref, prev_ref, hist0_ref, qm_ref, mkt_ref, mv_ref,
                  gmix_ref, wg_ref, woa_ref, wob_ref, wom_ref, wout_ref, wpool_ref, spool_ref,
                  o_ref, ext_ref, *, tm, pos0):
    it = pl.program_id(1)
    x = x_ref[...]

    h = _rms(x, gmix_ref[...]).astype(BF16)
    gates = jax.nn.sigmoid(jnp.dot(h, wg_ref[...], preferred_element_type=F32))
    d = x.shape[1]

    ub = ub_ref[...]
    ext_ref[0:HIST_ROWS, :] = jnp.where(it == 0, hist0_ref[...], prev_ref[...])
    ext_ref[HIST_ROWS:HIST_ROWS + tm, :] = ub
    pos = pos0 + it * tm + lax.broadcasted_iota(jnp.int32, (tm, 1), 0)
    ys = []
    for g, w in enumerate(POOL_WINDOWS):
        c0 = g * POOL_GROUP
        win = ub[:, c0:c0 + POOL_GROUP]
        for k in range(1, w):
            win = win + ext_ref[HIST_ROWS - k:HIST_ROWS - k + tm, c0:c0 + POOL_GROUP]
        cnt = jnp.minimum(w, pos + 1).astype(F32)
        pg = (win / cnt - ub[:, c0:c0 + POOL_GROUP]).astype(BF16)
        ys.append(jnp.dot(pg, wpool_ref[g], preferred_element_type=F32))
    bmix = (jnp.concatenate(ys, axis=1) * spool_ref[...]).astype(BF16)

    qm = qm_ref[...]
    ms = []
    for hd in range(N_HEADS_M):
        lg = jnp.dot(qm[:, hd * HD_M:(hd + 1) * HD_M], mkt_ref[hd], preferred_element_type=F32)
        lg = lg * (HD_M ** -0.5)
        e = jnp.exp(lg - jnp.max(lg, axis=1, keepdims=True))
        pr = (e / jnp.sum(e, axis=1, keepdims=True)).astype(BF16)
        ms.append(jnp.dot(pr, mv_ref[hd], preferred_element_type=F32))
    mmix = jnp.concatenate(ms, axis=1).astype(BF16)

    mixed = (gates[:, 0:d] * jnp.dot(a_ref[...], woa_ref[...], preferred_element_type=F32)
             + gates[:, d:2 * d] * jnp.dot(bmix, wob_ref[...], preferred_element_type=F32)
             + gates[:, 2 * d:3 * d] * jnp.dot(mmix, wom_ref[...], preferred_element_type=F32))
    o_ref[...] = x + jnp.dot(mixed.astype(BF16), wout_ref[...], preferred_element_type=F32)


def _merge_call(x, a, ub, hist0, qm, mkt, mv, gmix, wg, woa, wob, wom, wout, wpool, spool, *, tm, pos0):
    b, t, d = x.shape
    nt = t // tm
    hb = tm // HIST_ROWS
    tok = lambda i, j: (i, j, 0)
    per_b3 = lambda i, j: (i, 0, 0)
    per_b4 = lambda i, j: (i, 0, 0, 0)
    in_specs = [
        pl.BlockSpec((None, tm, d), tok),
        pl.BlockSpec((None, tm, D_ATT_A), tok),
        pl.BlockSpec((None, tm, D_POOL), tok),
        pl.BlockSpec((None, HIST_ROWS, D_POOL), lambda i, j: (i, jnp.maximum(j * hb - 1, 0), 0)),
        pl.BlockSpec((None, HIST_ROWS, D_POOL), per_b3),
        pl.BlockSpec((None, tm, D_MEM_ATT), tok),
        pl.BlockSpec((None, N_HEADS_M, HD_M, mkt.shape[3]), per_b4),
        pl.BlockSpec((None, N_HEADS_M, mv.shape[2], HD_M), per_b4),
        _resident((1, d)),
        _resident(wg.shape),
        _resident(woa.shape), _resident(wob.shape), _resident(wom.shape),
        _resident(wout.shape),
        _resident(wpool.shape),
        _resident((1, D_POOL)),
    ]
    return pl.pallas_call(
        functools.partial(_merge_kernel, tm=tm, pos0=pos0), grid=(b, nt), in_specs=in_specs,
        out_specs=pl.BlockSpec((None, tm, d), tok),
        out_shape=jax.ShapeDtypeStruct((b, t, d), F32),
        scratch_shapes=[pltpu.VMEM((HIST_ROWS + tm, D_POOL), F32)],
        compiler_params=_cparams(("parallel", "arbitrary")), name="merge",
    )(x, a, ub, ub, hist0, qm, mkt, mv, gmix, wg, woa, wob, wom, wout, wpool, spool)


def _ffn_kernel(x_ref, g_ref, wgate_ref, wup_ref, wdown_ref, o_ref):
    x = x_ref[...]
    h = _rms(x, g_ref[...]).astype(BF16)
    gate = jnp.dot(h, wgate_ref[...], preferred_element_type=F32)
    up = jnp.dot(h, wup_ref[...], preferred_element_type=F32)
    act = (jax.nn.silu(gate) * up).astype(BF16)
    o_ref[...] = x + jnp.dot(act, wdown_ref[...], preferred_element_type=F32)


def _ffn_call(x2d, g, wgate, wup, wdown, tm):
    n, d = x2d.shape
    row = lambda i: (i, 0)
    return pl.pallas_call(
        _ffn_kernel, grid=(n // tm,),
        in_specs=[pl.BlockSpec((tm, d), row), _resident((1, d)),
                  _resident(wgate.shape), _resident(wup.shape), _resident(wdown.shape)],
        out_specs=pl.BlockSpec((tm, d), row),
        out_shape=jax.ShapeDtypeStruct((n, d), F32),
        compiler_params=_cparams(("parallel",)), name="ffn",
    )(x2d, g, wgate, wup, wdown)


def _rope_tables(pos, head_dim):
    half = head_dim // 2
    inv = ROPE_THETA ** (-jnp.arange(half, dtype=F32) / half)
    ang = pos.astype(F32)[:, None] * inv[None, :]
    reps = LANES // half
    return jnp.tile(jnp.cos(ang), (1, reps)), jnp.tile(jnp.sin(ang), (1, reps))


def _rope_tables_t(pos, head_dim):
    half = head_dim // 2
    inv = ROPE_THETA ** (-jnp.arange(half, dtype=F32) / half)
    ang = inv[:, None] * pos.astype(F32)[None, :]
    return jnp.cos(ang), jnp.sin(ang)


def _pack_weights(w_in, g_qa, g_ka, g_kidx, g_qm):
    d = w_in.shape[0]
    widths = (D_ATT_A, D_KV_A, D_KV_A, D_QIDX, D_IDX, N_IDX_HEADS, D_POOL, D_MEM_ATT, N_BRANCH * d)
    cuts = [int(c) for c in np.cumsum(widths)[:-1]]
    wqa, wka, wva, wqi, wki, wwi, wub, wqm, wgates = jnp.split(w_in, cuts, axis=1)
    pad = jnp.zeros((d, LANES - D_IDX - N_IDX_HEADS), w_in.dtype)
    w_cat = jnp.concatenate([wqa, wka, wva, wqi, wki, wwi, pad, wub, wqm], axis=1).astype(BF16)
    w_rows = jnp.concatenate([wka, wva, wki, wwi, pad, wub, wqm], axis=1).astype(BF16)
    wpad = jnp.zeros((d, T_END - T_WI - N_IDX_HEADS), w_in.dtype)
    w_t = jnp.concatenate([wqa, wqi, wva, wwi, wpad], axis=1).T.astype(BF16)
    gqa_t = jnp.tile(g_qa, N_HEADS_A)[None, :]
    gka_t = jnp.tile(g_ka, N_KV_A)[None, :]
    gki_t = jnp.concatenate([g_kidx, jnp.ones((LANES - D_IDX,), g_kidx.dtype)])[None, :]
    gqa_b = jnp.broadcast_to(g_qa[:, None], (HD_A, LANES))
    return w_cat, w_rows, w_t, wgates.astype(BF16), gqa_t, gka_t, gki_t, g_qm[None, :], gqa_b


def _block_diag_ones(n, group, limit=None):
    i = np.arange(n)
    m = (i[:, None] // group) == (i[None, :] // group)
    if limit is not None:
        m = m & (i[:, None] < limit) & (i[None, :] < limit)
    return jnp.asarray(m, BF16)


def _key_blocks(k_all, v_all, ki_all, kb):
    b, s = k_all.shape[0], k_all.shape[1]
    nkb = -(-s // kb)
    pad = nkb * kb - s
    if pad:
        k_all = jnp.pad(k_all, ((0, 0), (0, pad), (0, 0), (0, 0)))
        v_all = jnp.pad(v_all, ((0, 0), (0, pad), (0, 0), (0, 0)))
        ki_all = jnp.pad(ki_all, ((0, 0), (0, pad), (0, 0)))
    ones = jnp.ones(v_all.shape[:-1] + (1,), BF16)
    zeros = jnp.zeros(v_all.shape[:-1] + (LANES - HD_A - 1,), BF16)
    kaug = jnp.concatenate([k_all.astype(BF16), ones, zeros], axis=-1)
    k = kaug.reshape(b, nkb, kb, N_KV_A, LANES).transpose(3, 0, 1, 2, 4)
    vaug = jnp.concatenate([v_all.astype(BF16), ones, zeros[..., :V_ROWS - HD_A - 1]], axis=-1)
    vt = vaug.reshape(b, nkb, kb, N_KV_A, V_ROWS).transpose(0, 3, 1, 4, 2)
    ki = ki_all.astype(BF16).reshape(b, nkb, kb, D_IDX)
    return k, vt, ki


def _tile(n, pref):
    t = pref
    while n % t:
        t //= 2
    return t


def _group_forward(x, pos, pos0, hist0, k_hist, v_hist, ki_hist, mkt, mv, wts, *, causal):
    (gmix, w_cat, w_rows, w_t, gqa_b, wgates, gqa_t, gka_t, gki_t, gqm, bd64, bdki, wpool, spool,
     woa, wob, wom, wout, gffn, wgate, wup, wdown) = wts
    b, t, d = x.shape
    n = b * t
    tm = _tile(t, 512)
    cosa, sina = _rope_tables(pos, HD_A)
    cosi, sini = _rope_tables(pos, D_IDX)
    tq = _tile(t, 256)
    proj_args = ((cosa, sina, cosi, sini), gqa_t, gka_t, gki_t, gqm, bd64, bdki, tm, t // tm)
    if causal:
        tside = (w_t,) + _rope_tables_t(pos, HD_A) + _rope_tables_t(pos, D_IDX) + (gqa_b,)
        ka, va, kw, ub, qm, qat, qit, wit, kaug, vt, kib = _proj_call(
            x.reshape(n, d), gmix, w_rows, *proj_args, dsa_layout=True, tposed_side=tside)
        kb, s = tm, t
        kblk = kaug.reshape(N_KV_A, b, t // kb, kb, LANES)
        kiblk = kib.reshape(b, t // kb, kb, D_IDX)
    else:
        qa, ka, va, qi, kw, ub, qm = _proj_call(x.reshape(n, d), gmix, w_cat, *proj_args, dsa_layout=False)
        qat = qa.reshape(b, t, D_ATT_A).transpose(0, 2, 1)
        qit = qi.reshape(b, t, D_QIDX).transpose(0, 2, 1)
        wit = kw[:, D_IDX:D_IDX + N_IDX_HEADS].reshape(b, t, N_IDX_HEADS).transpose(0, 2, 1)
    ka = ka.reshape(b, t, N_KV_A, HD_A)
    va = va.reshape(b, t, N_KV_A, HD_A)
    ki = kw[:, :D_IDX].reshape(b, t, D_IDX)
    ub = ub.reshape(b, t, D_POOL)
    if not causal:
        k_all = jnp.concatenate([k_hist, ka], axis=1)
        s, kb = k_all.shape[1], 256
        kblk, vt, kiblk = _key_blocks(k_all, jnp.concatenate([v_hist, va], axis=1),
                                      jnp.concatenate([ki_hist, ki], axis=1), kb)
    topk = min(TOPK_MAX, s // 4)
    a = _dsa_call(qat, qit, wit, kblk, vt, kiblk, tq=tq, kb=kb, causal=causal, s_valid=s, topk=topk)
    if tq % LANES:
        a = a.transpose(0, 2, 1)

    tmm = _tile(t, 512)
    x2 = _merge_call(x, a, ub, hist0, qm.reshape(b, t, D_MEM_ATT), mkt, mv, gmix, wgates,
                     woa, wob, wom, wout, wpool, spool, tm=tmm, pos0=pos0)
    y = _ffn_call(x2.reshape(n, d), gffn, wgate, wup, wdown, _tile(n, 512)).reshape(b, t, d)
    return y, ka, va, ki, ub


def kernel(x_prompt, x_sample, mem_prompt, cache_a_k, cache_a_v, cache_idx_k, cache_pool, cache_mem_k,
           cache_mem_v, g_mix, w_in, g_qa, g_ka, g_kidx, g_qm, g_mem, w_mem_kv, g_km, w_pool, s_pool,
           w_oa, w_ob, w_om, w_out, g_ffn, w_gate, w_up, w_down):
    depth = w_in.shape[0]
    t = x_prompt.shape[1]
    ts = x_sample.shape[1]
    past = cache_a_k.shape[2]
    b = x_prompt.shape[0]
    n_mem = mem_prompt.shape[1]
    pos_p = jnp.arange(t, dtype=jnp.int32)
    pos_s = past + jnp.arange(ts, dtype=jnp.int32)
    bd64 = _block_diag_ones(D_ATT_A, HD_A)
    bdki = _block_diag_ones(LANES, LANES, limit=D_IDX)

    xp, xs = x_prompt, x_sample
    outs = [[] for _ in range(10)]
    for l in range(depth):
        w_cat, w_rows, w_t, wgates, gqa_t, gka_t, gki_t, gqm, gqa_b = _pack_weights(
            w_in[l], g_qa[l], g_ka[l], g_kidx[l], g_qm[l])
        wts = (g_mix[l][None, :], w_cat, w_rows, w_t, gqa_b, wgates, gqa_t, gka_t, gki_t, gqm, bd64, bdki,
               w_pool[l].astype(BF16), s_pool[l][None, :],
               w_oa[l].astype(BF16), w_ob[l].astype(BF16), w_om[l].astype(BF16), w_out[l].astype(BF16),
               g_ffn[l][None, :], w_gate[l].astype(BF16), w_up[l].astype(BF16), w_down[l].astype(BF16))

        mk, mv = _memkv_call(mem_prompt.reshape(b * n_mem, -1), g_mem[l][None, :],
                             w_mem_kv[l].astype(BF16), g_km[l][None, :], _tile(b * n_mem, 256))
        mk = mk.reshape(b, n_mem, N_HEADS_M, HD_M)
        mv = mv.reshape(b, n_mem, N_HEADS_M, HD_M)
        hist0 = jnp.zeros((b, HIST_ROWS, D_POOL), F32)
        xp, ka, va, ki, ub = _group_forward(
            xp, pos_p, 0, hist0, None, None, None,
            mk.astype(BF16).transpose(0, 2, 3, 1), mv.astype(BF16).transpose(0, 2, 1, 3), wts, causal=True)
        for lst, val in zip(outs[:6], (ka, va, ki, ub[:, -POOL_HIST:], mk, mv)):
            lst.append(val)

        bs = xs.shape[0]
        hist0 = jnp.concatenate([jnp.zeros((bs, HIST_ROWS - POOL_HIST, D_POOL), F32), cache_pool[l]], axis=1)
        xs, ka, va, ki, ub = _group_forward(
            xs, pos_s, past, hist0, cache_a_k[l], cache_a_v[l], cache_idx_k[l],
            cache_mem_k[l].astype(BF16).transpose(0, 2, 3, 1), cache_mem_v[l].astype(BF16).transpose(0, 2, 1, 3),
            wts, causal=False)
        pool_s = jnp.concatenate([cache_pool[l], ub], axis=1)[:, -POOL_HIST:]
        for lst, val in zip(outs[6:], (ka, va, ki, pool_s)):
            lst.append(val)

    stacked = [jnp.stack(o) for o in outs]
    return (xp, xs, *stacked)
```

```python
import functools

import jax
import jax.numpy as jnp
import numpy as np
from jax import lax
from jax.experimental import pallas as pl
from jax.experimental.pallas import tpu as pltpu

F32 = jnp.float32
BF16 = jnp.bfloat16

CHUNK = 64
EPS = 1e-6
ROPE_THETA = 10000.0
N_HEADS_A = 8
N_KV_A = 2
GROUP_A = N_HEADS_A // N_KV_A
HD_A = 64
D_ATT_A = N_HEADS_A * HD_A
D_KV_A = N_KV_A * HD_A
N_IDX_HEADS = 8
D_IDX = 32
D_QIDX = N_IDX_HEADS * D_IDX
IDX_SCALE = (N_IDX_HEADS * D_IDX) ** -0.5
TOPK_MAX = 256
POOL_WINDOWS = (2, 4, 8, 16)
POOL_GROUP = 128
D_POOL = len(POOL_WINDOWS) * POOL_GROUP
POOL_HIST = 15
N_HEADS_M = 4
HD_M = 128
D_MEM_ATT = N_HEADS_M * HD_M
N_BRANCH = 3

LANES = 128
SUBLANES = 8
V7X_VMEM_BYTES = 64 * 1024 * 1024
VMEM_LIMIT = 56 * 1024 * 1024

HIST_ROWS = 2 * SUBLANES
MASK_VALUE = -1e30
FLT_MAX = float(np.finfo(np.float32).max)
KEY_BITS = 32
Q_SCALE = HD_A ** -0.5 * float(np.log2(np.e))
SUM_FLOOR = 1e-30
COUNT_ROWS = 4 * SUBLANES
COUNT_CHAINS = 2
COUNT_BLOCKS_PER_STEP = 4
MXU_BLOCKS_PER_STEP = 8
V_ROWS = HD_A + 2 * SUBLANES

C_QA = 0
C_KA = C_QA + D_ATT_A
C_VA = C_KA + D_KV_A
C_QI = C_VA + D_KV_A
C_KW = C_QI + D_QIDX
C_UB = C_KW + LANES
C_QM = C_UB + D_POOL
C_END = C_QM + D_MEM_ATT

R_KA = 0
R_VA = R_KA + D_KV_A
R_KW = R_VA + D_KV_A
R_UB = R_KW + LANES
R_QM = R_UB + D_POOL
R_END = R_QM + D_MEM_ATT
T_QA = 0
T_QI = T_QA + D_ATT_A
T_VA = T_QI + D_QIDX
T_WI = T_VA + D_KV_A
T_END = T_WI + 2 * SUBLANES


def _cparams(sem):
    return pltpu.CompilerParams(dimension_semantics=sem, vmem_limit_bytes=VMEM_LIMIT)


def _resident(shape):
    zeros = (0,) * len(shape)
    return pl.BlockSpec(shape, lambda *_: zeros, pipeline_mode=pl.Buffered(1))


def _rms(x, g):
    ms = jnp.mean(x * x, axis=-1, keepdims=True)
    return x * lax.rsqrt(ms + EPS) * g


def _group_sumsq(x, bd):
    sq = x * x
    hi = sq.astype(BF16)
    lo = (sq - hi.astype(F32)).astype(BF16)
    return (jnp.dot(hi, bd, preferred_element_type=F32)
            + jnp.dot(lo, bd, preferred_element_type=F32))


def _rope_lanes(x, cos, sin, half):
    lane = lax.broadcasted_iota(jnp.int32, x.shape, 1)
    first = (lane % (2 * half)) < half
    left = pltpu.roll(x, LANES - half, 1)
    right = pltpu.roll(x, half, 1)
    rot = jnp.where(first, -left, right)
    return x * cos + rot * sin


def _proj_kernel(x_ref, gmix_ref, w_ref, cosa_ref, sina_ref, cosi_ref, sini_ref,
                 gqa_ref, gka_ref, gki_ref, gqm_ref, bd64_ref, bdki_ref, *refs, dsa_layout):
    if dsa_layout:
        (wt_ref, cosat_ref, sinat_ref, cosit_ref, sinit_ref, gqab_ref,
         ka_ref, va_ref, kw_ref, ub_ref, qm_ref, qat_ref, qit_ref, wit_ref, kaug_ref, vt_ref, kib_ref) = refs
        c_ka, c_va, c_kw, c_ub, c_qm = R_KA, R_VA, R_KW, R_UB, R_QM
    else:
        qa_ref, ka_ref, va_ref, qi_ref, kw_ref, ub_ref, qm_ref = refs
        c_ka, c_va, c_kw, c_ub, c_qm = C_KA, C_VA, C_KW, C_UB, C_QM
    x = x_ref[...]
    tm = x.shape[0]
    h = _rms(x, gmix_ref[...]).astype(BF16)
    p = jnp.dot(h, w_ref[...], preferred_element_type=F32)
    cosa, sina = cosa_ref[...], sina_ref[...]
    cosi, sini = cosi_ref[...], sini_ref[...]
    bd64 = bd64_ref[...]

    if dsa_layout:
        pt = lax.dot_general(wt_ref[...], h, (((1,), (1,)), ((), ())), preferred_element_type=F32)
        cosat, sinat = cosat_ref[...], sinat_ref[...]
        gq = jnp.concatenate([gqab_ref[...]] * (tm // LANES), axis=1)
        ha = HD_A // 2
        for hd in range(N_HEADS_A):
            xh = pt[T_QA + hd * HD_A:T_QA + (hd + 1) * HD_A, :]
            xh = xh * lax.rsqrt(jnp.sum(xh * xh, axis=0, keepdims=True) * (1.0 / HD_A) + EPS) * gq
            x1, x2 = xh[:ha, :], xh[ha:, :]
            qat_ref[hd * HD_A:hd * HD_A + ha, :] = ((x1 * cosat - x2 * sinat) * Q_SCALE).astype(BF16)
            qat_ref[hd * HD_A + ha:(hd + 1) * HD_A, :] = ((x2 * cosat + x1 * sinat) * Q_SCALE).astype(BF16)
        cosit, sinit = cosit_ref[...], sinit_ref[...]
        hi = D_IDX // 2
        for hd in range(N_IDX_HEADS):
            x1 = pt[T_QI + hd * D_IDX:T_QI + hd * D_IDX + hi, :]
            x2 = pt[T_QI + hd * D_IDX + hi:T_QI + (hd + 1) * D_IDX, :]
            qit_ref[hd * D_IDX:hd * D_IDX + hi, :] = (x1 * cosit - x2 * sinit).astype(BF16)
            qit_ref[hd * D_IDX + hi:(hd + 1) * D_IDX, :] = (x2 * cosit + x1 * sinit).astype(BF16)
        wit_ref[...] = pt[T_WI:T_WI + N_IDX_HEADS, :] * IDX_SCALE
    else:
        qa = p[:, C_QA:C_QA + D_ATT_A]
        ss = _group_sumsq(qa, bd64)
        qa = qa * lax.rsqrt(ss * (1.0 / HD_A) + EPS) * gqa_ref[...]
        for c in range(D_ATT_A // LANES):
            blk = _rope_lanes(qa[:, c * LANES:(c + 1) * LANES], cosa, sina, HD_A // 2) * Q_SCALE
            qa_ref[:, c * LANES:(c + 1) * LANES] = blk.astype(BF16)
        for c in range(D_QIDX // LANES):
            blk = _rope_lanes(p[:, C_QI + c * LANES:C_QI + (c + 1) * LANES], cosi, sini, D_IDX // 2)
            qi_ref[:, c * LANES:(c + 1) * LANES] = blk.astype(BF16)

    ka = p[:, c_ka:c_ka + D_KV_A]
    ss = _group_sumsq(ka, bd64[:D_KV_A, :D_KV_A])
    ka = ka * lax.rsqrt(ss * (1.0 / HD_A) + EPS) * gka_ref[...]
    ka = _rope_lanes(ka, cosa, sina, HD_A // 2)
    va = p[:, c_va:c_va + D_KV_A]
    for g in range(N_KV_A):
        ka_ref[:, g, :] = (ka if g == 0 else pltpu.roll(ka, LANES - g * HD_A, 1))[:, :HD_A]
        va_ref[:, g, :] = (va if g == 0 else pltpu.roll(va, LANES - g * HD_A, 1))[:, :HD_A]
    if dsa_layout:
        lane = lax.broadcasted_iota(jnp.int32, ka.shape, 1)
        tail = jnp.where(lane == HD_A, 1.0, 0.0)
        kaug_ref[0] = jnp.where(lane < HD_A, ka, tail).astype(BF16)
        kaug_ref[1] = jnp.where(lane < HD_A, pltpu.roll(ka, HD_A, 1), tail).astype(BF16)
        srow = lax.broadcasted_iota(jnp.int32, (V_ROWS - HD_A, tm), 0)
        ones_rows = jnp.where(srow == 0, 1.0, 0.0).astype(BF16)
        for g in range(N_KV_A):
            vt_ref[g, 0:HD_A, :] = pt[T_VA + g * HD_A:T_VA + (g + 1) * HD_A, :].astype(BF16)
            vt_ref[g, HD_A:V_ROWS, :] = ones_rows

    kw = p[:, c_kw:c_kw + LANES]
    ss = _group_sumsq(kw, bdki_ref[...])
    kin = kw * lax.rsqrt(ss * (1.0 / D_IDX) + EPS) * gki_ref[...]
    kin = _rope_lanes(kin, cosi, sini, D_IDX // 2)
    lane = lax.broadcasted_iota(jnp.int32, kw.shape, 1)
    kw = jnp.where(lane < D_IDX, kin, kw * IDX_SCALE)
    kw_ref[...] = kw
    if dsa_layout:
        kib_ref[...] = kw[:, :D_IDX].astype(BF16)

    ub_ref[...] = p[:, c_ub:c_ub + D_POOL]

    gqm = gqm_ref[...]
    for hd in range(N_HEADS_M):
        blk = p[:, c_qm + hd * HD_M:c_qm + (hd + 1) * HD_M]
        qm_ref[:, hd * HD_M:(hd + 1) * HD_M] = _rms(blk, gqm).astype(BF16)


def _proj_call(x2d, gmix, w_rows, tabs, gqa_t, gka_t, gki_t, gqm, bd64, bdki, tm, n_tab_blocks,
               dsa_layout, tposed_side=()):
    n = x2d.shape[0]
    d = x2d.shape[1]
    nb = n // (tm * n_tab_blocks)
    grid = (n // tm,)
    row = lambda i: (i, 0)
    tab = lambda i: (i % n_tab_blocks, 0)
    tabt = lambda i: (0, i % n_tab_blocks)
    in_specs = [
        pl.BlockSpec((tm, d), row),
        _resident((1, d)),
        _resident(w_rows.shape),
        pl.BlockSpec((tm, LANES), tab), pl.BlockSpec((tm, LANES), tab),
        pl.BlockSpec((tm, LANES), tab), pl.BlockSpec((tm, LANES), tab),
        _resident((1, D_ATT_A)), _resident((1, D_KV_A)),
        _resident((1, LANES)), _resident((1, HD_M)),
        _resident((D_ATT_A, D_ATT_A)), _resident((LANES, LANES)),
    ]
    if dsa_layout:
        in_specs += [_resident((T_END, d)),
                     pl.BlockSpec((HD_A // 2, tm), tabt), pl.BlockSpec((HD_A // 2, tm), tabt),
                     pl.BlockSpec((D_IDX // 2, tm), tabt), pl.BlockSpec((D_IDX // 2, tm), tabt),
                     _resident((HD_A, LANES))]
    rows = lambda width, dtype: (jax.ShapeDtypeStruct((n, width), dtype), pl.BlockSpec((tm, width), row))
    heads = (jax.ShapeDtypeStruct((n, N_KV_A, HD_A), F32), pl.BlockSpec((tm, N_KV_A, HD_A), lambda i: (i, 0, 0)))
    per_token = [heads, heads, rows(LANES, F32), rows(D_POOL, F32),
                 rows(D_MEM_ATT, BF16)]
    if dsa_layout:
        t = tm * n_tab_blocks
        cols = lambda i: (i // n_tab_blocks, 0, i % n_tab_blocks)
        tposed = lambda r, dtype: (jax.ShapeDtypeStruct((nb, r, t), dtype), pl.BlockSpec((None, r, tm), cols))
        outs = per_token + [
            tposed(D_ATT_A, BF16), tposed(D_QIDX, BF16), tposed(N_IDX_HEADS, F32),
            (jax.ShapeDtypeStruct((N_KV_A, n, LANES), BF16), pl.BlockSpec((N_KV_A, tm, LANES), lambda i: (0, i, 0))),
            (jax.ShapeDtypeStruct((nb, N_KV_A, n_tab_blocks, V_ROWS, tm), BF16),
             pl.BlockSpec((None, N_KV_A, None, V_ROWS, tm),
                          lambda i: (i // n_tab_blocks, 0, i % n_tab_blocks, 0, 0))),
            rows(D_IDX, BF16),
        ]
    else:
        outs = [rows(D_ATT_A, BF16)] + per_token[:2] + [rows(D_QIDX, BF16)] + per_token[2:]
    return pl.pallas_call(
        functools.partial(_proj_kernel, dsa_layout=dsa_layout), grid=grid, in_specs=in_specs,
        out_specs=[o[1] for o in outs], out_shape=[o[0] for o in outs],
        compiler_params=_cparams(("parallel",)), name="proj",
    )(x2d, gmix, w_rows, *tabs, gqa_t, gka_t, gki_t, gqm, bd64, bdki, *tposed_side)


def _key_bits_to_float(u):
    k = u ^ jnp.int32(-2147483648)
    bits = jnp.where(k >= 0, k, k ^ jnp.int32(0x7FFFFFFF))
    return lax.bitcast_convert_type(bits, F32)


def _dsa_kernel(qat_ref, qit_ref, wit_ref, k_ref, vt_ref, ki_ref, o_ref,
                sc_ref, m_ref, acc_ref, qaug_ref, qis_ref, km_ref,
                *, tq, kb, nkb_total, causal, s_valid, topk, rows_out):
    qblk = pl.program_id(1)
    if causal:
        nkb = lax.div(qblk + kb // tq, jnp.int32(kb // tq))
    else:
        nkb = nkb_total
    qpos = qblk * tq + lax.broadcasted_iota(jnp.int32, (1, tq), 1)
    if causal:
        lim = (lax.shift_right_logical(qpos, 6) + 1) * CHUNK
    else:
        lim = jnp.full((1, tq), s_valid, jnp.int32)
    limf = lim.astype(F32)
    krow = lax.broadcasted_iota(jnp.int32, (kb, tq), 0)
    kf = float(topk)

    def for_each_block_pairwise(fn):
        n = jnp.int32(nkb)
        groups = lax.div(n, MXU_BLOCKS_PER_STEP)

        def step(jj, carry):
            for w in range(MXU_BLOCKS_PER_STEP):
                fn(MXU_BLOCKS_PER_STEP * jj + w)
            return carry
        lax.fori_loop(0, groups, step, 0)
        half = MXU_BLOCKS_PER_STEP // 2
        rest = n - groups * MXU_BLOCKS_PER_STEP

        @pl.when(rest >= half)
        def _():
            for w in range(half):
                fn(n - rest + w)
        done = n - lax.rem(rest, half)

        @pl.when(lax.rem(rest, half) >= 2)
        def _():
            fn(done)
            fn(done + 1)

        @pl.when(lax.rem(n, 2) == 1)
        def _():
            fn(n - 1)

    for h in range(N_IDX_HEADS):
        qis_ref[:, h * tq:(h + 1) * tq] = qit_ref[h * D_IDX:(h + 1) * D_IDX, :]
    qit = qis_ref[...]
    wit = wit_ref[...]

    def score_block(j):
        z = jnp.dot(ki_ref[j], qit, preferred_element_type=F32)
        s = jnp.zeros((kb, tq), F32)
        for h in range(N_IDX_HEADS):
            s = s + wit[h:h + 1, :] * jnp.maximum(z[:, h * tq:(h + 1) * tq], 0.0)
        sc_ref[j] = jnp.where(j * kb + krow < lim, s, -jnp.inf)

    for_each_block_pairwise(score_block)

    crow = lax.broadcasted_iota(jnp.int32, (COUNT_ROWS, tq), 0)

    def count(pred):
        def tally(j, accs):
            accs = list(accs)
            for ci, r in enumerate(range(0, kb, COUNT_ROWS)):
                hit = pred(sc_ref[j, r:r + COUNT_ROWS, :], j * kb + r + crow)
                accs[ci % COUNT_CHAINS] = accs[ci % COUNT_CHAINS] + jnp.where(hit, 1.0, 0.0)
            return tuple(accs)

        def tally_group(jj, accs):
            for w in range(COUNT_BLOCKS_PER_STEP):
                accs = tally(COUNT_BLOCKS_PER_STEP * jj + w, accs)
            return accs

        n = jnp.int32(nkb)
        groups = lax.div(n, COUNT_BLOCKS_PER_STEP)
        zero = jnp.zeros((COUNT_ROWS, tq), F32)
        accs = lax.fori_loop(0, groups, tally_group, (zero,) * COUNT_CHAINS)
        accs = lax.fori_loop(groups * COUNT_BLOCKS_PER_STEP, n, tally, accs)
        return jnp.sum(sum(accs[1:], accs[0]), axis=0, keepdims=True)

    def radix_body(i, carry):
        prefix, cge, cgt = carry
        cand = prefix | lax.shift_left(jnp.int32(1), KEY_BITS - 1 - i)
        t = _key_bits_to_float(cand)
        c = count(lambda s, idx: s >= t)
        take = c >= kf
        return jnp.where(take, cand, prefix), jnp.where(take, c, cge), jnp.where(take, cgt, c)

    prefix, cge, cgt = lax.fori_loop(
        0, KEY_BITS, radix_body, (jnp.zeros((1, tq), jnp.int32), limf, jnp.zeros((1, tq), F32)))
    thr = jnp.where(limf >= kf, _key_bits_to_float(prefix), -FLT_MAX)

    any_tie = jnp.max(jnp.where(cge > kf, 1.0, 0.0), axis=1, keepdims=True)

    def mask_plain():
        def body(j, carry):
            sc_ref[j] = jnp.where(sc_ref[j] >= thr, 0.0, MASK_VALUE)
            return carry
        lax.fori_loop(0, nkb, body, 0)

    def mask_ties():
        need = kf - cgt
        tri = (lax.broadcasted_iota(jnp.int32, (kb, kb), 1)
               <= lax.broadcasted_iota(jnp.int32, (kb, kb), 0)).astype(BF16)

        def body(j, seen):
            s = sc_ref[j]
            eq = s == thr
            rank = jnp.dot(tri, jnp.where(eq, 1.0, 0.0).astype(BF16), preferred_element_type=F32) + seen
            keep = (s > thr) | (eq & (rank <= need))
            sc_ref[j] = jnp.where(keep, 0.0, MASK_VALUE)
            return rank[kb - 1:kb, :]

        lax.fori_loop(0, nkb, body, jnp.zeros((1, tq), F32))

    lax.cond(any_tie[0, 0] > 0.0, mask_ties, mask_plain)

    @pl.when(qblk == 0)
    def _():
        for g in range(N_KV_A):
            def norm_body(j, mx, g=g):
                kk = k_ref[g, j].astype(F32)
                return jnp.maximum(mx, jnp.sum(kk * kk, axis=1, keepdims=True))
            mx = lax.fori_loop(0, nkb_total, norm_body, jnp.zeros((kb, 1), F32))
            km_ref[g] = jnp.broadcast_to(jnp.max(mx, axis=0, keepdims=True), (SUBLANES, LANES))

    aug_row = lax.broadcasted_iota(jnp.int32, (LANES - HD_A, tq), 0)
    for g in range(N_KV_A):
        kmax2 = km_ref[g][0:1, :]
        kmax2 = jnp.concatenate([kmax2] * (tq // LANES), axis=1) if tq >= LANES else kmax2[:, :tq]
        for u in range(GROUP_A):
            hd = g * GROUP_A + u
            q = qat_ref[hd * HD_A:(hd + 1) * HD_A, :]
            qf = q.astype(F32)
            shift = jnp.sqrt(jnp.sum(qf * qf, axis=0, keepdims=True) * kmax2)
            qaug_ref[g, 0:HD_A, u * tq:(u + 1) * tq] = q
            qaug_ref[g, HD_A:LANES, u * tq:(u + 1) * tq] = jnp.where(aug_row == 0, -shift, 0.0).astype(BF16)

    acc_ref[...] = jnp.zeros(acc_ref.shape, F32)

    def attend_block(j):
        bias = sc_ref[j]
        for g in range(N_KV_A):
            lg = jnp.dot(k_ref[g, j], qaug_ref[g], preferred_element_type=F32)
            vt = vt_ref[g, j]
            for u in range(GROUP_A):
                hd = g * GROUP_A + u
                p = jnp.exp2(lg[:, u * tq:(u + 1) * tq] + bias).astype(BF16)
                acc_ref[hd] += jnp.dot(vt, p, preferred_element_type=F32)

    for_each_block_pairwise(attend_block)

    sums_ok = jnp.ones((1, tq), F32)
    for hd in range(N_HEADS_A):
        sums_ok = jnp.where(acc_ref[hd][HD_A:HD_A + 1, :] > SUM_FLOOR, sums_ok, 0.0)
    all_ok = jnp.min(sums_ok, axis=1, keepdims=True)

    @pl.when(all_ok[0, 0] < 1.0)
    def _():
        m_ref[...] = jnp.full(m_ref.shape, MASK_VALUE, F32)
        acc_ref[...] = jnp.zeros(acc_ref.shape, F32)

        def online_body(j, carry):
            bias = sc_ref[j]
            for g in range(N_KV_A):
                lg = jnp.dot(k_ref[g, j], qaug_ref[g], preferred_element_type=F32)
                vt = vt_ref[g, j]
                for u in range(GROUP_A):
                    hd = g * GROUP_A + u
                    l = lg[:, u * tq:(u + 1) * tq] + bias
                    m_old = m_ref[hd:hd + 1, :]
                    m_new = jnp.maximum(m_old, jnp.max(l, axis=0, keepdims=True))
                    p = jnp.exp2(l - m_new).astype(BF16)
                    pv = jnp.dot(vt, p, preferred_element_type=F32)
                    acc_ref[hd] = jnp.exp2(m_old - m_new) * acc_ref[hd] + pv
                    m_ref[hd:hd + 1, :] = m_new
            return carry

        lax.fori_loop(0, nkb, online_body, 0)

    heads = []
    for hd in range(N_HEADS_A):
        a = acc_ref[hd]
        heads.append(a[:HD_A, :] / a[HD_A:HD_A + 1, :])
    out_t = jnp.concatenate(heads, axis=0)
    o_ref[...] = (out_t.T if rows_out else out_t).astype(BF16)


def _dsa_call(qat, qit, wit, k, vt, ki, *, tq, kb, causal, s_valid, topk):
    b, t = qat.shape[0], qat.shape[2]
    nkb_total = k.shape[2]
    rows_out = tq % LANES == 0
    kern = functools.partial(_dsa_kernel, tq=tq, kb=kb, nkb_total=nkb_total, causal=causal,
                             s_valid=s_valid, topk=topk, rows_out=rows_out)
    qcols = lambda i, j: (i, 0, j)
    in_specs = [
        pl.BlockSpec((None, D_ATT_A, tq), qcols),
        pl.BlockSpec((None, D_QIDX, tq), qcols),
        pl.BlockSpec((None, N_IDX_HEADS, tq), qcols),
        pl.BlockSpec((N_KV_A, None, nkb_total, kb, LANES), lambda i, j: (0, i, 0, 0, 0)),
        pl.BlockSpec((None, N_KV_A, nkb_total, V_ROWS, kb), lambda i, j: (i, 0, 0, 0, 0)),
        pl.BlockSpec((None, nkb_total, kb, D_IDX), lambda i, j: (i, 0, 0, 0)),
    ]
    if rows_out:
        out_spec = pl.BlockSpec((None, tq, D_ATT_A), lambda i, j: (i, j, 0))
        out_shape = jax.ShapeDtypeStruct((b, t, D_ATT_A), BF16)
    else:
        out_spec = pl.BlockSpec((None, D_ATT_A, tq), qcols)
        out_shape = jax.ShapeDtypeStruct((b, D_ATT_A, t), BF16)
    return pl.pallas_call(
        kern, grid=(b, t // tq), in_specs=in_specs, out_specs=out_spec, out_shape=out_shape,
        scratch_shapes=[pltpu.VMEM((nkb_total, kb, tq), F32),
                        pltpu.VMEM((N_HEADS_A, tq), F32),
                        pltpu.VMEM((N_HEADS_A, V_ROWS, tq), F32),
                        pltpu.VMEM((N_KV_A, LANES, GROUP_A * tq), BF16),
                        pltpu.VMEM((D_IDX, N_IDX_HEADS * tq), BF16),
                        pltpu.VMEM((N_KV_A, SUBLANES, LANES), F32)],
        compiler_params=_cparams(("parallel", "arbitrary")), name="dsa",
    )(qat, qit, wit, k, vt, ki)


def _memkv_kernel(mem_ref, gmem_ref, w_ref, gkm_ref, k_ref, v_ref):
    h = _rms(mem_ref[...], gmem_ref[...]).astype(BF16)
    kv = jnp.dot(h, w_ref[...], preferred_element_type=F32)
    gkm = gkm_ref[...]
    for hd in range(N_HEADS_M):
        k_ref[:, hd * HD_M:(hd + 1) * HD_M] = _rms(kv[:, hd * HD_M:(hd + 1) * HD_M], gkm)
    v_ref[...] = kv[:, D_MEM_ATT:]


def _memkv_call(mem2d, gmem, w_kv, gkm, tm):
    n, d = mem2d.shape
    row = lambda i: (i, 0)
    return pl.pallas_call(
        _memkv_kernel, grid=(n // tm,),
        in_specs=[pl.BlockSpec((tm, d), row), _resident((1, d)),
                  _resident((d, 2 * D_MEM_ATT)), _resident((1, HD_M))],
        out_specs=[pl.BlockSpec((tm, D_MEM_ATT), row), pl.BlockSpec((tm, D_MEM_ATT), row)],
        out_shape=[jax.ShapeDtypeStruct((n, D_MEM_ATT), F32), jax.ShapeDtypeStruct((n, D_MEM_ATT), F32)],
        compiler_params=_cparams(("parallel",)), name="memkv",
    )(mem2d, gmem, w_kv, gkm)


def _merge_kernel(x_ref, a_ref, ub_ref, prev_ref, hist0_ref, qm_ref, mkt_ref, mv_ref,
                  gmix_ref, wg_ref, woa_ref, wob_ref, wom_ref, wout_ref, wpool_ref, spool_ref,
                  o_ref, ext_ref, *, tm, pos0):
    it = pl.program_id(1)
    x = x_ref[...]

    h = _rms(x, gmix_ref[...]).astype(BF16)
    gates = jax.nn.sigmoid(jnp.dot(h, wg_ref[...], preferred_element_type=F32))
    d = x.shape[1]

    ub = ub_ref[...]
    ext_ref[0:HIST_ROWS, :] = jnp.where(it == 0, hist0_ref[...], prev_ref[...])
    ext_ref[HIST_ROWS:HIST_ROWS + tm, :] = ub
    pos = pos0 + it * tm + lax.broadcasted_iota(jnp.int32, (tm, 1), 0)
    ys = []
    for g, w in enumerate(POOL_WINDOWS):
        c0 = g * POOL_GROUP
        win = ub[:, c0:c0 + POOL_GROUP]
        for k in range(1, w):
            win = win + ext_ref[HIST_ROWS - k:HIST_ROWS - k + tm, c0:c0 + POOL_GROUP]
        cnt = jnp.minimum(w, pos + 1).astype(F32)
        pg = (win / cnt - ub[:, c0:c0 + POOL_GROUP]).astype(BF16)
        ys.append(jnp.dot(pg, wpool_ref[g], preferred_element_type=F32))
    bmix = (jnp.concatenate(ys, axis=1) * spool_ref[...]).astype(BF16)

    qm = qm_ref[...]
    ms = []
    for hd in range(N_HEADS_M):
        lg = jnp.dot(qm[:, hd * HD_M:(hd + 1) * HD_M], mkt_ref[hd], preferred_element_type=F32)
        lg = lg * (HD_M ** -0.5)
        e = jnp.exp(lg - jnp.max(lg, axis=1, keepdims=True))
        pr = (e / jnp.sum(e, axis=1, keepdims=True)).astype(BF16)
        ms.append(jnp.dot(pr, mv_ref[hd], preferred_element_type=F32))
    mmix = jnp.concatenate(ms, axis=1).astype(BF16)

    mixed = (gates[:, 0:d] * jnp.dot(a_ref[...], woa_ref[...], preferred_element_type=F32)
             + gates[:, d:2 * d] * jnp.dot(bmix, wob_ref[...], preferred_element_type=F32)
             + gates[:, 2 * d:3 * d] * jnp.dot(mmix, wom_ref[...], preferred_element_type=F32))
    o_ref[...] = x + jnp.dot(mixed.astype(BF16), wout_ref[...], preferred_element_type=F32)


def _merge_call(x, a, ub, hist0, qm, mkt, mv, gmix, wg, woa, wob, wom, wout, wpool, spool, *, tm, pos0):
    b, t, d = x.shape
    nt = t // tm
    hb = tm // HIST_ROWS
    tok = lambda i, j: (i, j, 0)
    per_b3 = lambda i, j: (i, 0, 0)
    per_b4 = lambda i, j: (i, 0, 0, 0)
    in_specs = [
        pl.BlockSpec((None, tm, d), tok),
        pl.BlockSpec((None, tm, D_ATT_A), tok),
        pl.BlockSpec((None, tm, D_POOL), tok),
        pl.BlockSpec((None, HIST_ROWS, D_POOL), lambda i, j: (i, jnp.maximum(j * hb - 1, 0), 0)),
        pl.BlockSpec((None, HIST_ROWS, D_POOL), per_b3),
        pl.BlockSpec((None, tm, D_MEM_ATT), tok),
        pl.BlockSpec((None, N_HEADS_M, HD_M, mkt.shape[3]), per_b4),
        pl.BlockSpec((None, N_HEADS_M, mv.shape[2], HD_M), per_b4),
        _resident((1, d)),
        _resident(wg.shape),
        _resident(woa.shape), _resident(wob.shape), _resident(wom.shape),
        _resident(wout.shape),
        _resident(wpool.shape),
        _resident((1, D_POOL)),
    ]
    return pl.pallas_call(
        functools.partial(_merge_kernel, tm=tm, pos0=pos0), grid=(b, nt), in_specs=in_specs,
        out_specs=pl.BlockSpec((None, tm, d), tok),
        out_shape=jax.ShapeDtypeStruct((b, t, d), F32),
        scratch_shapes=[pltpu.VMEM((HIST_ROWS + tm, D_POOL), F32)],
        compiler_params=_cparams(("parallel", "arbitrary")), name="merge",
    )(x, a, ub, ub, hist0, qm, mkt, mv, gmix, wg, woa, wob, wom, wout, wpool, spool)


def _ffn_kernel(x_ref, g_ref, wgate_ref, wup_ref, wdown_ref, o_ref):
    x = x_ref[...]
    h = _rms(x, g_ref[...]).astype(BF16)
    gate = jnp.dot(h, wgate_ref[...], preferred_element_type=F32)
    up = jnp.dot(h, wup_ref[...], preferred_element_type=F32)
    act = (jax.nn.silu(gate) * up).astype(BF16)
    o_ref[...] = x + jnp.dot(act, wdown_ref[...], preferred_element_type=F32)


def _ffn_call(x2d, g, wgate, wup, wdown, tm):
    n, d = x2d.shape
    row = lambda i: (i, 0)
    return pl.pallas_call(
        _ffn_kernel, grid=(n // tm,),
        in_specs=[pl.BlockSpec((tm, d), row), _resident((1, d)),
                  _resident(wgate.shape), _resident(wup.shape), _resident(wdown.shape)],
        out_specs=pl.BlockSpec((tm, d), row),
        out_shape=jax.ShapeDtypeStruct((n, d), F32),
        compiler_params=_cparams(("parallel",)), name="ffn",
    )(x2d, g, wgate, wup, wdown)


def _rope_tables(pos, head_dim):
    half = head_dim // 2
    inv = ROPE_THETA ** (-jnp.arange(half, dtype=F32) / half)
    ang = pos.astype(F32)[:, None] * inv[None, :]
    reps = LANES // half
    return jnp.tile(jnp.cos(ang), (1, reps)), jnp.tile(jnp.sin(ang), (1, reps))


def _rope_tables_t(pos, head_dim):
    half = head_dim // 2
    inv = ROPE_THETA ** (-jnp.arange(half, dtype=F32) / half)
    ang = inv[:, None] * pos.astype(F32)[None, :]
    return jnp.cos(ang), jnp.sin(ang)


def _pack_weights(w_in, g_qa, g_ka, g_kidx, g_qm):
    d = w_in.shape[0]
    widths = (D_ATT_A, D_KV_A, D_KV_A, D_QIDX, D_IDX, N_IDX_HEADS, D_POOL, D_MEM_ATT, N_BRANCH * d)
    cuts = [int(c) for c in np.cumsum(widths)[:-1]]
    wqa, wka, wva, wqi, wki, wwi, wub, wqm, wgates = jnp.split(w_in, cuts, axis=1)
    pad = jnp.zeros((d, LANES - D_IDX - N_IDX_HEADS), w_in.dtype)
    w_cat = jnp.concatenate([wqa, wka, wva, wqi, wki, wwi, pad, wub, wqm], axis=1).astype(BF16)
    w_rows = jnp.concatenate([wka, wva, wki, wwi, pad, wub, wqm], axis=1).astype(BF16)
    wpad = jnp.zeros((d, T_END - T_WI - N_IDX_HEADS), w_in.dtype)
    w_t = jnp.concatenate([wqa, wqi, wva, wwi, wpad], axis=1).T.astype(BF16)
    gqa_t = jnp.tile(g_qa, N_HEADS_A)[None, :]
    gka_t = jnp.tile(g_ka, N_KV_A)[None, :]
    gki_t = jnp.concatenate([g_kidx, jnp.ones((LANES - D_IDX,), g_kidx.dtype)])[None, :]
    gqa_b = jnp.broadcast_to(g_qa[:, None], (HD_A, LANES))
    return w_cat, w_rows, w_t, wgates.astype(BF16), gqa_t, gka_t, gki_t, g_qm[None, :], gqa_b


def _block_diag_ones(n, group, limit=None):
    i = np.arange(n)
    m = (i[:, None] // group) == (i[None, :] // group)
    if limit is not None:
        m = m & (i[:, None] < limit) & (i[None, :] < limit)
    return jnp.asarray(m, BF16)


def _key_blocks(k_all, v_all, ki_all, kb):
    b, s = k_all.shape[0], k_all.shape[1]
    nkb = -(-s // kb)
    pad = nkb * kb - s
    if pad:
        k_all = jnp.pad(k_all, ((0, 0), (0, pad), (0, 0), (0, 0)))
        v_all = jnp.pad(v_all, ((0, 0), (0, pad), (0, 0), (0, 0)))
        ki_all = jnp.pad(ki_all, ((0, 0), (0, pad), (0, 0)))
    ones = jnp.ones(v_all.shape[:-1] + (1,), BF16)
    zeros = jnp.zeros(v_all.shape[:-1] + (LANES - HD_A - 1,), BF16)
    kaug = jnp.concatenate([k_all.astype(BF16), ones, zeros], axis=-1)
    k = kaug.reshape(b, nkb, kb, N_KV_A, LANES).transpose(3, 0, 1, 2, 4)
    vaug = jnp.concatenate([v_all.astype(BF16), ones, zeros[..., :V_ROWS - HD_A - 1]], axis=-1)
    vt = vaug.reshape(b, nkb, kb, N_KV_A, V_ROWS).transpose(0, 3, 1, 4, 2)
    ki = ki_all.astype(BF16).reshape(b, nkb, kb, D_IDX)
    return k, vt, ki


def _tile(n, pref):
    t = pref
    while n % t:
        t //= 2
    return t


def _group_forward(x, pos, pos0, hist0, k_hist, v_hist, ki_hist, mkt, mv, wts, *, causal):
    (gmix, w_cat, w_rows, w_t, gqa_b, wgates, gqa_t, gka_t, gki_t, gqm, bd64, bdki, wpool, spool,
     woa, wob, wom, wout, gffn, wgate, wup, wdown) = wts
    b, t, d = x.shape
    n = b * t
    tm = _tile(t, 512)
    cosa, sina = _rope_tables(pos, HD_A)
    cosi, sini = _rope_tables(pos, D_IDX)
    tq = _tile(t, 256)
    proj_args = ((cosa, sina, cosi, sini), gqa_t, gka_t, gki_t, gqm, bd64, bdki, tm, t // tm)
    if causal:
        tside = (w_t,) + _rope_tables_t(pos, HD_A) + _rope_tables_t(pos, D_IDX) + (gqa_b,)
        ka, va, kw, ub, qm, qat, qit, wit, kaug, vt, kib = _proj_call(
            x.reshape(n, d), gmix, w_rows, *proj_args, dsa_layout=True, tposed_side=tside)
        kb, s = tm, t
        kblk = kaug.reshape(N_KV_A, b, t // kb, kb, LANES)
        kiblk = kib.reshape(b, t // kb, kb, D_IDX)
    else:
        qa, ka, va, qi, kw, ub, qm = _proj_call(x.reshape(n, d), gmix, w_cat, *proj_args, dsa_layout=False)
        qat = qa.reshape(b, t, D_ATT_A).transpose(0, 2, 1)
        qit = qi.reshape(b, t, D_QIDX).transpose(0, 2, 1)
        wit = kw[:, D_IDX:D_IDX + N_IDX_HEADS].reshape(b, t, N_IDX_HEADS).transpose(0, 2, 1)
    ka = ka.reshape(b, t, N_KV_A, HD_A)
    va = va.reshape(b, t, N_KV_A, HD_A)
    ki = kw[:, :D_IDX].reshape(b, t, D_IDX)
    ub = ub.reshape(b, t, D_POOL)
    if not causal:
        k_all = jnp.concatenate([k_hist, ka], axis=1)
        s, kb = k_all.shape[1], 256
        kblk, vt, kiblk = _key_blocks(k_all, jnp.concatenate([v_hist, va], axis=1),
                                      jnp.concatenate([ki_hist, ki], axis=1), kb)
    topk = min(TOPK_MAX, s // 4)
    a = _dsa_call(qat, qit, wit, kblk, vt, kiblk, tq=tq, kb=kb, causal=causal, s_valid=s, topk=topk)
    if tq % LANES:
        a = a.transpose(0, 2, 1)

    tmm = _tile(t, 512)
    x2 = _merge_call(x, a, ub, hist0, qm.reshape(b, t, D_MEM_ATT), mkt, mv, gmix, wgates,
                     woa, wob, wom, wout, wpool, spool, tm=tmm, pos0=pos0)
    y = _ffn_call(x2.reshape(n, d), gffn, wgate, wup, wdown, _tile(n, 512)).reshape(b, t, d)
    return y, ka, va, ki, ub


def kernel(x_prompt, x_sample, mem_prompt, cache_a_k, cache_a_v, cache_idx_k, cache_pool, cache_mem_k,
           cache_mem_v, g_mix, w_in, g_qa, g_ka, g_kidx, g_qm, g_mem, w_mem_kv, g_km, w_pool, s_pool,
           w_oa, w_ob, w_om, w_out, g_ffn, w_gate, w_up, w_down):
    depth = w_in.shape[0]
    t = x_prompt.shape[1]
    ts = x_sample.shape[1]
    past = cache_a_k.shape[2]
    b = x_prompt.shape[0]
    n_mem = mem_prompt.shape[1]
    pos_p = jnp.arange(t, dtype=jnp.int32)
    pos_s = past + jnp.arange(ts, dtype=jnp.int32)
    bd64 = _block_diag_ones(D_ATT_A, HD_A)
    bdki = _block_diag_ones(LANES, LANES, limit=D_IDX)

    xp, xs = x_prompt, x_sample
    outs = [[] for _ in range(10)]
    for l in range(depth):
        w_cat, w_rows, w_t, wgates, gqa_t, gka_t, gki_t, gqm, gqa_b = _pack_weights(
            w_in[l], g_qa[l], g_ka[l], g_kidx[l], g_qm[l])
        wts = (g_mix[l][None, :], w_cat, w_rows, w_t, gqa_b, wgates, gqa_t, gka_t, gki_t, gqm, bd64, bdki,
               w_pool[l].astype(BF16), s_pool[l][None, :],
               w_oa[l].astype(BF16), w_ob[l].astype(BF16), w_om[l].astype(BF16), w_out[l].astype(BF16),
               g_ffn[l][None, :], w_gate[l].astype(BF16), w_up[l].astype(BF16), w_down[l].astype(BF16))

        mk, mv = _memkv_call(mem_prompt.reshape(b * n_mem, -1), g_mem[l][None, :],
                             w_mem_kv[l].astype(BF16), g_km[l][None, :], _tile(b * n_mem, 256))
        mk = mk.reshape(b, n_mem, N_HEADS_M, HD_M)
        mv = mv.reshape(b, n_mem, N_HEADS_M, HD_M)
        hist0 = jnp.zeros((b, HIST_ROWS, D_POOL), F32)
        xp, ka, va, ki, ub = _group_forward(
            xp, pos_p, 0, hist0, None, None, None,
            mk.astype(BF16).transpose(0, 2, 3, 1), mv.astype(BF16).transpose(0, 2, 1, 3), wts, causal=True)
        for lst, val in zip(outs[:6], (ka, va, ki, ub[:, -POOL_HIST:], mk, mv)):
            lst.append(val)

        bs = xs.shape[0]
        hist0 = jnp.concatenate([jnp.zeros((bs, HIST_ROWS - POOL_HIST, D_POOL), F32), cache_pool[l]], axis=1)
        xs, ka, va, ki, ub = _group_forward(
            xs, pos_s, past, hist0, cache_a_k[l], cache_a_v[l], cache_idx_k[l],
            cache_mem_k[l].astype(BF16).transpose(0, 2, 3, 1), cache_mem_v[l].astype(BF16).transpose(0, 2, 1, 3),
            wts, causal=False)
        pool_s = jnp.concatenate([cache_pool[l], ub], axis=1)[:, -POOL_HIST:]
        for lst, val in zip(outs[6:], (ka, va, ki, pool_s)):
            lst.append(val)

    stacked = [jnp.stack(o) for o in outs]
    return (xp, xs, *stacked)
```

```python
import functools

import jax
import jax.numpy as jnp
import numpy as np
from jax import lax
from jax.experimental import pallas as pl
from jax.experimental.pallas import tpu as pltpu

F32 = jnp.float32
BF16 = jnp.bfloat16

CHUNK = 64
EPS = 1e-6
ROPE_THETA = 10000.0
N_HEADS_A = 8
N_KV_A = 2
GROUP_A = N_HEADS_A // N_KV_A
HD_A = 64
D_ATT_A = N_HEADS_A * HD_A
D_KV_A = N_KV_A * HD_A
N_IDX_HEADS = 8
D_IDX = 32
D_QIDX = N_IDX_HEADS * D_IDX
IDX_SCALE = (N_IDX_HEADS * D_IDX) ** -0.5
TOPK_MAX = 256
POOL_WINDOWS = (2, 4, 8, 16)
POOL_GROUP = 128
D_POOL = len(POOL_WINDOWS) * POOL_GROUP
POOL_HIST = 15
N_HEADS_M = 4
HD_M = 128
D_MEM_ATT = N_HEADS_M * HD_M
N_BRANCH = 3

LANES = 128
SUBLANES = 8
V7X_VMEM_BYTES = 64 * 1024 * 1024
VMEM_LIMIT = 56 * 1024 * 1024

HIST_ROWS = 2 * SUBLANES
MASK_VALUE = -1e30
FLT_MAX = float(np.finfo(np.float32).max)
KEY_BITS = 32
Q_SCALE = HD_A ** -0.5 * float(np.log2(np.e))
SUM_FLOOR = 1e-30
COUNT_ROWS = 4 * SUBLANES
COUNT_CHAINS = 2
COUNT_BLOCKS_PER_STEP = 4
MXU_BLOCKS_PER_STEP = 4
V_ROWS = HD_A + 2 * SUBLANES

C_QA = 0
C_KA = C_QA + D_ATT_A
C_VA = C_KA + D_KV_A
C_QI = C_VA + D_KV_A
C_KW = C_QI + D_QIDX
C_UB = C_KW + LANES
C_QM = C_UB + D_POOL
C_END = C_QM + D_MEM_ATT

R_KA = 0
R_VA = R_KA + D_KV_A
R_KW = R_VA + D_KV_A
R_UB = R_KW + LANES
R_QM = R_UB + D_POOL
R_END = R_QM + D_MEM_ATT
T_QA = 0
T_QI = T_QA + D_ATT_A
T_VA = T_QI + D_QIDX
T_WI = T_VA + D_KV_A
T_END = T_WI + 2 * SUBLANES


def _cparams(sem):
    return pltpu.CompilerParams(dimension_semantics=sem, vmem_limit_bytes=VMEM_LIMIT)


def _resident(shape):
    zeros = (0,) * len(shape)
    return pl.BlockSpec(shape, lambda *_: zeros, pipeline_mode=pl.Buffered(1))


def _rms(x, g):
    ms = jnp.mean(x * x, axis=-1, keepdims=True)
    return x * lax.rsqrt(ms + EPS) * g


def _group_sumsq(x, bd):
    sq = x * x
    hi = sq.astype(BF16)
    lo = (sq - hi.astype(F32)).astype(BF16)
    return (jnp.dot(hi, bd, preferred_element_type=F32)
            + jnp.dot(lo, bd, preferred_element_type=F32))


def _rope_lanes(x, cos, sin, half):
    lane = lax.broadcasted_iota(jnp.int32, x.shape, 1)
    first = (lane % (2 * half)) < half
    left = pltpu.roll(x, LANES - half, 1)
    right = pltpu.roll(x, half, 1)
    rot = jnp.where(first, -left, right)
    return x * cos + rot * sin


def _proj_kernel(x_ref, gmix_ref, w_ref, cosa_ref, sina_ref, cosi_ref, sini_ref,
                 gqa_ref, gka_ref, gki_ref, gqm_ref, bd64_ref, bdki_ref, *refs, dsa_layout):
    if dsa_layout:
        (wt_ref, cosat_ref, sinat_ref, cosit_ref, sinit_ref, gqab_ref,
         ka_ref, va_ref, kw_ref, ub_ref, qm_ref, qat_ref, qit_ref, wit_ref, kaug_ref, vt_ref, kib_ref) = refs
        c_ka, c_va, c_kw, c_ub, c_qm = R_KA, R_VA, R_KW, R_UB, R_QM
    else:
        qa_ref, ka_ref, va_ref, qi_ref, kw_ref, ub_ref, qm_ref = refs
        c_ka, c_va, c_kw, c_ub, c_qm = C_KA, C_VA, C_KW, C_UB, C_QM
    x = x_ref[...]
    tm = x.shape[0]
    h = _rms(x, gmix_ref[...]).astype(BF16)
    p = jnp.dot(h, w_ref[...], preferred_element_type=F32)
    cosa, sina = cosa_ref[...], sina_ref[...]
    cosi, sini = cosi_ref[...], sini_ref[...]
    bd64 = bd64_ref[...]

    if dsa_layout:
        pt = lax.dot_general(wt_ref[...], h, (((1,), (1,)), ((), ())), preferred_element_type=F32)
        cosat, sinat = cosat_ref[...], sinat_ref[...]
        gq = jnp.concatenate([gqab_ref[...]] * (tm // LANES), axis=1)
        ha = HD_A // 2
        for hd in range(N_HEADS_A):
            xh = pt[T_QA + hd * HD_A:T_QA + (hd + 1) * HD_A, :]
            xh = xh * lax.rsqrt(jnp.sum(xh * xh, axis=0, keepdims=True) * (1.0 / HD_A) + EPS) * gq
            x1, x2 = xh[:ha, :], xh[ha:, :]
            qat_ref[hd * HD_A:hd * HD_A + ha, :] = ((x1 * cosat - x2 * sinat) * Q_SCALE).astype(BF16)
            qat_ref[hd * HD_A + ha:(hd + 1) * HD_A, :] = ((x2 * cosat + x1 * sinat) * Q_SCALE).astype(BF16)
        cosit, sinit = cosit_ref[...], sinit_ref[...]
        hi = D_IDX // 2
        for hd in range(N_IDX_HEADS):
            x1 = pt[T_QI + hd * D_IDX:T_QI + hd * D_IDX + hi, :]
            x2 = pt[T_QI + hd * D_IDX + hi:T_QI + (hd + 1) * D_IDX, :]
            qit_ref[hd * D_IDX:hd * D_IDX + hi, :] = (x1 * cosit - x2 * sinit).astype(BF16)
            qit_ref[hd * D_IDX + hi:(hd + 1) * D_IDX, :] = (x2 * cosit + x1 * sinit).astype(BF16)
        wit_ref[...] = pt[T_WI:T_WI + N_IDX_HEADS, :] * IDX_SCALE
    else:
        qa = p[:, C_QA:C_QA + D_ATT_A]
        ss = _group_sumsq(qa, bd64)
        qa = qa * lax.rsqrt(ss * (1.0 / HD_A) + EPS) * gqa_ref[...]
        for c in range(D_ATT_A // LANES):
            blk = _rope_lanes(qa[:, c * LANES:(c + 1) * LANES], cosa, sina, HD_A // 2) * Q_SCALE
            qa_ref[:, c * LANES:(c + 1) * LANES] = blk.astype(BF16)
        for c in range(D_QIDX // LANES):
            blk = _rope_lanes(p[:, C_QI + c * LANES:C_QI + (c + 1) * LANES], cosi, sini, D_IDX // 2)
            qi_ref[:, c * LANES:(c + 1) * LANES] = blk.astype(BF16)

    ka = p[:, c_ka:c_ka + D_KV_A]
    ss = _group_sumsq(ka, bd64[:D_KV_A, :D_KV_A])
    ka = ka * lax.rsqrt(ss * (1.0 / HD_A) + EPS) * gka_ref[...]
    ka = _rope_lanes(ka, cosa, sina, HD_A // 2)
    va = p[:, c_va:c_va + D_KV_A]
    for g in range(N_KV_A):
        ka_ref[:, g, :] = (ka if g == 0 else pltpu.roll(ka, LANES - g * HD_A, 1))[:, :HD_A]
        va_ref[:, g, :] = (va if g == 0 else pltpu.roll(va, LANES - g * HD_A, 1))[:, :HD_A]
    if dsa_layout:
        lane = lax.broadcasted_iota(jnp.int32, ka.shape, 1)
        tail = jnp.where(lane == HD_A, 1.0, 0.0)
        kaug_ref[0] = jnp.where(lane < HD_A, ka, tail).astype(BF16)
        kaug_ref[1] = jnp.where(lane < HD_A, pltpu.roll(ka, HD_A, 1), tail).astype(BF16)
        srow = lax.broadcasted_iota(jnp.int32, (V_ROWS - HD_A, tm), 0)
        ones_rows = jnp.where(srow == 0, 1.0, 0.0).astype(BF16)
        for g in range(N_KV_A):
            vt_ref[g, 0:HD_A, :] = pt[T_VA + g * HD_A:T_VA + (g + 1) * HD_A, :].astype(BF16)
            vt_ref[g, HD_A:V_ROWS, :] = ones_rows

    kw = p[:, c_kw:c_kw + LANES]
    ss = _group_sumsq(kw, bdki_ref[...])
    kin = kw * lax.rsqrt(ss * (1.0 / D_IDX) + EPS) * gki_ref[...]
    kin = _rope_lanes(kin, cosi, sini, D_IDX // 2)
    lane = lax.broadcasted_iota(jnp.int32, kw.shape, 1)
    kw = jnp.where(lane < D_IDX, kin, kw * IDX_SCALE)
    kw_ref[...] = kw
    if dsa_layout:
        kib_ref[...] = kw[:, :D_IDX].astype(BF16)

    ub_ref[...] = p[:, c_ub:c_ub + D_POOL]

    gqm = gqm_ref[...]
    for hd in range(N_HEADS_M):
        blk = p[:, c_qm + hd * HD_M:c_qm + (hd + 1) * HD_M]
        qm_ref[:, hd * HD_M:(hd + 1) * HD_M] = _rms(blk, gqm).astype(BF16)


def _proj_call(x2d, gmix, w_rows, tabs, gqa_t, gka_t, gki_t, gqm, bd64, bdki, tm, n_tab_blocks,
               dsa_layout, tposed_side=()):
    n = x2d.shape[0]
    d = x2d.shape[1]
    nb = n // (tm * n_tab_blocks)
    grid = (n // tm,)
    row = lambda i: (i, 0)
    tab = lambda i: (i % n_tab_blocks, 0)
    tabt = lambda i: (0, i % n_tab_blocks)
    in_specs = [
        pl.BlockSpec((tm, d), row),
        _resident((1, d)),
        _resident(w_rows.shape),
        pl.BlockSpec((tm, LANES), tab), pl.BlockSpec((tm, LANES), tab),
        pl.BlockSpec((tm, LANES), tab), pl.BlockSpec((tm, LANES), tab),
        _resident((1, D_ATT_A)), _resident((1, D_KV_A)),
        _resident((1, LANES)), _resident((1, HD_M)),
        _resident((D_ATT_A, D_ATT_A)), _resident((LANES, LANES)),
    ]
    if dsa_layout:
        in_specs += [_resident((T_END, d)),
                     pl.BlockSpec((HD_A // 2, tm), tabt), pl.BlockSpec((HD_A // 2, tm), tabt),
                     pl.BlockSpec((D_IDX // 2, tm), tabt), pl.BlockSpec((D_IDX // 2, tm), tabt),
                     _resident((HD_A, LANES))]
    rows = lambda width, dtype: (jax.ShapeDtypeStruct((n, width), dtype), pl.BlockSpec((tm, width), row))
    heads = (jax.ShapeDtypeStruct((n, N_KV_A, HD_A), F32), pl.BlockSpec((tm, N_KV_A, HD_A), lambda i: (i, 0, 0)))
    per_token = [heads, heads, rows(LANES, F32), rows(D_POOL, F32),
                 rows(D_MEM_ATT, BF16)]
    if dsa_layout:
        t = tm * n_tab_blocks
        cols = lambda i: (i // n_tab_blocks, 0, i % n_tab_blocks)
        tposed = lambda r, dtype: (jax.ShapeDtypeStruct((nb, r, t), dtype), pl.BlockSpec((None, r, tm), cols))
        outs = per_token + [
            tposed(D_ATT_A, BF16), tposed(D_QIDX, BF16), tposed(N_IDX_HEADS, F32),
            (jax.ShapeDtypeStruct((N_KV_A, n, LANES), BF16), pl.BlockSpec((N_KV_A, tm, LANES), lambda i: (0, i, 0))),
            (jax.ShapeDtypeStruct((nb, N_KV_A, n_tab_blocks, V_ROWS, tm), BF16),
             pl.BlockSpec((None, N_KV_A, None, V_ROWS, tm),
                          lambda i: (i // n_tab_blocks, 0, i % n_tab_blocks, 0, 0))),
            rows(D_IDX, BF16),
        ]
    else:
        outs = [rows(D_ATT_A, BF16)] + per_token[:2] + [rows(D_QIDX, BF16)] + per_token[2:]
    return pl.pallas_call(
        functools.partial(_proj_kernel, dsa_layout=dsa_layout), grid=grid, in_specs=in_specs,
        out_specs=[o[1] for o in outs], out_shape=[o[0] for o in outs],
        compiler_params=_cparams(("parallel",)), name="proj",
    )(x2d, gmix, w_rows, *tabs, gqa_t, gka_t, gki_t, gqm, bd64, bdki, *tposed_side)


def _key_bits_to_float(u):
    k = u ^ jnp.int32(-2147483648)
    bits = jnp.where(k >= 0, k, k ^ jnp.int32(0x7FFFFFFF))
    return lax.bitcast_convert_type(bits, F32)


def _dsa_kernel(qat_ref, qit_ref, wit_ref, k_ref, vt_ref, ki_ref, o_ref,
                sc_ref, m_ref, acc_ref, qaug_ref, qis_ref, km_ref,
                *, tq, kb, nkb_total, causal, s_valid, topk, rows_out):
    qblk = pl.program_id(1)
    if causal:
        nkb = lax.div(qblk + kb // tq, jnp.int32(kb // tq))
    else:
        nkb = nkb_total
    qpos = qblk * tq + lax.broadcasted_iota(jnp.int32, (1, tq), 1)
    if causal:
        lim = (lax.shift_right_logical(qpos, 6) + 1) * CHUNK
    else:
        lim = jnp.full((1, tq), s_valid, jnp.int32)
    limf = lim.astype(F32)
    krow = lax.broadcasted_iota(jnp.int32, (kb, tq), 0)
    kf = float(topk)

    def for_each_block_pairwise(fn):
        n = jnp.int32(nkb)
        groups = lax.div(n, MXU_BLOCKS_PER_STEP)

        def step(jj, carry):
            for w in range(MXU_BLOCKS_PER_STEP):
                fn(MXU_BLOCKS_PER_STEP * jj + w)
            return carry
        lax.fori_loop(0, groups, step, 0)
        done = groups * MXU_BLOCKS_PER_STEP

        @pl.when(n - done >= 2)
        def _():
            fn(done)
            fn(done + 1)

        @pl.when(lax.rem(n, 2) == 1)
        def _():
            fn(n - 1)

    for h in range(N_IDX_HEADS):
        qis_ref[:, h * tq:(h + 1) * tq] = qit_ref[h * D_IDX:(h + 1) * D_IDX, :]
    qit = qis_ref[...]
    wit = wit_ref[...]

    def score_block(j):
        z = jnp.dot(ki_ref[j], qit, preferred_element_type=F32)
        s = jnp.zeros((kb, tq), F32)
        for h in range(N_IDX_HEADS):
            s = s + wit[h:h + 1, :] * jnp.maximum(z[:, h * tq:(h + 1) * tq], 0.0)
        sc_ref[j] = jnp.where(j * kb + krow < lim, s, -jnp.inf)

    for_each_block_pairwise(score_block)

    crow = lax.broadcasted_iota(jnp.int32, (COUNT_ROWS, tq), 0)

    def count(pred):
        def tally(j, accs):
            accs = list(accs)
            for ci, r in enumerate(range(0, kb, COUNT_ROWS)):
                hit = pred(sc_ref[j, r:r + COUNT_ROWS, :], j * kb + r + crow)
                accs[ci % COUNT_CHAINS] = accs[ci % COUNT_CHAINS] + jnp.where(hit, 1.0, 0.0)
            return tuple(accs)

        def tally_group(jj, accs):
            for w in range(COUNT_BLOCKS_PER_STEP):
                accs = tally(COUNT_BLOCKS_PER_STEP * jj + w, accs)
            return accs

        n = jnp.int32(nkb)
        groups = lax.div(n, COUNT_BLOCKS_PER_STEP)
        zero = jnp.zeros((COUNT_ROWS, tq), F32)
        accs = lax.fori_loop(0, groups, tally_group, (zero,) * COUNT_CHAINS)
        accs = lax.fori_loop(groups * COUNT_BLOCKS_PER_STEP, n, tally, accs)
        return jnp.sum(sum(accs[1:], accs[0]), axis=0, keepdims=True)

    def radix_body(i, carry):
        prefix, cge, cgt = carry
        cand = prefix | lax.shift_left(jnp.int32(1), KEY_BITS - 1 - i)
        t = _key_bits_to_float(cand)
        c = count(lambda s, idx: s >= t)
        take = c >= kf
        return jnp.where(take, cand, prefix), jnp.where(take, c, cge), jnp.where(take, cgt, c)

    most_keys = (qblk + 1) * tq if causal else s_valid
    passes = jnp.where(most_keys > topk, KEY_BITS, 0)
    prefix, cge, cgt = lax.fori_loop(
        0, passes, radix_body, (jnp.zeros((1, tq), jnp.int32), limf, jnp.zeros((1, tq), F32)))
    thr = jnp.where(limf > kf, _key_bits_to_float(prefix), -FLT_MAX)

    any_tie = jnp.max(jnp.where(cge > kf, 1.0, 0.0), axis=1, keepdims=True)

    def mask_plain():
        def body(j, carry):
            sc_ref[j] = jnp.where(sc_ref[j] >= thr, 0.0, MASK_VALUE)
            return carry
        lax.fori_loop(0, nkb, body, 0)

    def mask_ties():
        need = kf - cgt
        tri = (lax.broadcasted_iota(jnp.int32, (kb, kb), 1)
               <= lax.broadcasted_iota(jnp.int32, (kb, kb), 0)).astype(BF16)

        def body(j, seen):
            s = sc_ref[j]
            eq = s == thr
            rank = jnp.dot(tri, jnp.where(eq, 1.0, 0.0).astype(BF16), preferred_element_type=F32) + seen
            keep = (s > thr) | (eq & (rank <= need))
            sc_ref[j] = jnp.where(keep, 0.0, MASK_VALUE)
            return rank[kb - 1:kb, :]

        lax.fori_loop(0, nkb, body, jnp.zeros((1, tq), F32))

    lax.cond(any_tie[0, 0] > 0.0, mask_ties, mask_plain)

    @pl.when(qblk == 0)
    def _():
        for g in range(N_KV_A):
            def norm_body(j, mx, g=g):
                kk = k_ref[g, j].astype(F32)
                return jnp.maximum(mx, jnp.sum(kk * kk, axis=1, keepdims=True))
            mx = lax.fori_loop(0, nkb_total, norm_body, jnp.zeros((kb, 1), F32))
            km_ref[g] = jnp.broadcast_to(jnp.max(mx, axis=0, keepdims=True), (SUBLANES, LANES))

    aug_row = lax.broadcasted_iota(jnp.int32, (LANES - HD_A, tq), 0)
    for g in range(N_KV_A):
        kmax2 = km_ref[g][0:1, :]
        kmax2 = jnp.concatenate([kmax2] * (tq // LANES), axis=1) if tq >= LANES else kmax2[:, :tq]
        for u in range(GROUP_A):
            hd = g * GROUP_A + u
            q = qat_ref[hd * HD_A:(hd + 1) * HD_A, :]
            qf = q.astype(F32)
            shift = jnp.sqrt(jnp.sum(qf * qf, axis=0, keepdims=True) * kmax2)
            qaug_ref[g, 0:HD_A, u * tq:(u + 1) * tq] = q
            qaug_ref[g, HD_A:LANES, u * tq:(u + 1) * tq] = jnp.where(aug_row == 0, -shift, 0.0).astype(BF16)

    acc_ref[...] = jnp.zeros(acc_ref.shape, F32)

    def attend_block(j):
        bias = sc_ref[j]
        for g in range(N_KV_A):
            lg = jnp.dot(k_ref[g, j], qaug_ref[g], preferred_element_type=F32)
            vt = vt_ref[g, j]
            for u in range(GROUP_A):
                hd = g * GROUP_A + u
                p = jnp.exp2(lg[:, u * tq:(u + 1) * tq] + bias).astype(BF16)
                acc_ref[hd] += jnp.dot(vt, p, preferred_element_type=F32)

    for_each_block_pairwise(attend_block)

    sums_ok = jnp.ones((1, tq), F32)
    for hd in range(N_HEADS_A):
        sums_ok = jnp.where(acc_ref[hd][HD_A:HD_A + 1, :] > SUM_FLOOR, sums_ok, 0.0)
    all_ok = jnp.min(sums_ok, axis=1, keepdims=True)

    @pl.when(all_ok[0, 0] < 1.0)
    def _():
        m_ref[...] = jnp.full(m_ref.shape, MASK_VALUE, F32)
        acc_ref[...] = jnp.zeros(acc_ref.shape, F32)

        def online_body(j, carry):
            bias = sc_ref[j]
            for g in range(N_KV_A):
                lg = jnp.dot(k_ref[g, j], qaug_ref[g], preferred_element_type=F32)
                vt = vt_ref[g, j]
                for u in range(GROUP_A):
                    hd = g * GROUP_A + u
                    l = lg[:, u * tq:(u + 1) * tq] + bias
                    m_old = m_ref[hd:hd + 1, :]
                    m_new = jnp.maximum(m_old, jnp.max(l, axis=0, keepdims=True))
                    p = jnp.exp2(l - m_new).astype(BF16)
                    pv = jnp.dot(vt, p, preferred_element_type=F32)
                    acc_ref[hd] = jnp.exp2(m_old - m_new) * acc_ref[hd] + pv
                    m_ref[hd:hd + 1, :] = m_new
            return carry

        lax.fori_loop(0, nkb, online_body, 0)

    heads = []
    for hd in range(N_HEADS_A):
        a = acc_ref[hd]
        heads.append(a[:HD_A, :] / a[HD_A:HD_A + 1, :])
    out_t = jnp.concatenate(heads, axis=0)
    o_ref[...] = (out_t.T if rows_out else out_t).astype(BF16)


def _dsa_call(qat, qit, wit, k, vt, ki, *, tq, kb, causal, s_valid, topk):
    b, t = qat.shape[0], qat.shape[2]
    nkb_total = k.shape[2]
    rows_out = tq % LANES == 0
    kern = functools.partial(_dsa_kernel, tq=tq, kb=kb, nkb_total=nkb_total, causal=causal,
                             s_valid=s_valid, topk=topk, rows_out=rows_out)
    qcols = lambda i, j: (i, 0, j)
    in_specs = [
        pl.BlockSpec((None, D_ATT_A, tq), qcols),
        pl.BlockSpec((None, D_QIDX, tq), qcols),
        pl.BlockSpec((None, N_IDX_HEADS, tq), qcols),
        pl.BlockSpec((N_KV_A, None, nkb_total, kb, LANES), lambda i, j: (0, i, 0, 0, 0)),
        pl.BlockSpec((None, N_KV_A, nkb_total, V_ROWS, kb), lambda i, j: (i, 0, 0, 0, 0)),
        pl.BlockSpec((None, nkb_total, kb, D_IDX), lambda i, j: (i, 0, 0, 0)),
    ]
    if rows_out:
        out_spec = pl.BlockSpec((None, tq, D_ATT_A), lambda i, j: (i, j, 0))
        out_shape = jax.ShapeDtypeStruct((b, t, D_ATT_A), BF16)
    else:
        out_spec = pl.BlockSpec((None, D_ATT_A, tq), qcols)
        out_shape = jax.ShapeDtypeStruct((b, D_ATT_A, t), BF16)
    return pl.pallas_call(
        kern, grid=(b, t // tq), in_specs=in_specs, out_specs=out_spec, out_shape=out_shape,
        scratch_shapes=[pltpu.VMEM((nkb_total, kb, tq), F32),
                        pltpu.VMEM((N_HEADS_A, tq), F32),
                        pltpu.VMEM((N_HEADS_A, V_ROWS, tq), F32),
                        pltpu.VMEM((N_KV_A, LANES, GROUP_A * tq), BF16),
                        pltpu.VMEM((D_IDX, N_IDX_HEADS * tq), BF16),
                        pltpu.VMEM((N_KV_A, SUBLANES, LANES), F32)],
        compiler_params=_cparams(("parallel", "arbitrary")), name="dsa",
    )(qat, qit, wit, k, vt, ki)


def _memkv_kernel(mem_ref, gmem_ref, w_ref, gkm_ref, k_ref, v_ref):
    h = _rms(mem_ref[...], gmem_ref[...]).astype(BF16)
    kv = jnp.dot(h, w_ref[...], preferred_element_type=F32)
    gkm = gkm_ref[...]
    for hd in range(N_HEADS_M):
        k_ref[:, hd * HD_M:(hd + 1) * HD_M] = _rms(kv[:, hd * HD_M:(hd + 1) * HD_M], gkm)
    v_ref[...] = kv[:, D_MEM_ATT:]


def _memkv_call(mem2d, gmem, w_kv, gkm, tm):
    n, d = mem2d.shape
    row = lambda i: (i, 0)
    return pl.pallas_call(
        _memkv_kernel, grid=(n // tm,),
        in_specs=[pl.BlockSpec((tm, d), row), _resident((1, d)),
                  _resident((d, 2 * D_MEM_ATT)), _resident((1, HD_M))],
        out_specs=[pl.BlockSpec((tm, D_MEM_ATT), row), pl.BlockSpec((tm, D_MEM_ATT), row)],
        out_shape=[jax.ShapeDtypeStruct((n, D_MEM_ATT), F32), jax.ShapeDtypeStruct((n, D_MEM_ATT), F32)],
        compiler_params=_cparams(("parallel",)), name="memkv",
    )(mem2d, gmem, w_kv, gkm)


def _merge_kernel(x_ref, a_ref, ub_ref, prev_ref, hist0_ref, qm_ref, mkt_ref, mv_ref,
                  gmix_ref, wg_ref, woa_ref, wob_ref, wom_ref, wout_ref, wpool_ref, spool_ref,
                  o_ref, ext_ref, *, tm, pos0):
    it = pl.program_id(1)
    x = x_ref[...]

    h = _rms(x, gmix_ref[...]).astype(BF16)
    gates = jax.nn.sigmoid(jnp.dot(h, wg_ref[...], preferred_element_type=F32))
    d = x.shape[1]

    ub = ub_ref[...]
    ext_ref[0:HIST_ROWS, :] = jnp.where(it == 0, hist0_ref[...], prev_ref[...])
    ext_ref[HIST_ROWS:HIST_ROWS + tm, :] = ub
    pos = pos0 + it * tm + lax.broadcasted_iota(jnp.int32, (tm, 1), 0)
    ys = []
    for g, w in enumerate(POOL_WINDOWS):
        c0 = g * POOL_GROUP
        win = ub[:, c0:c0 + POOL_GROUP]
        for k in range(1, w):
            win = win + ext_ref[HIST_ROWS - k:HIST_ROWS - k + tm, c0:c0 + POOL_GROUP]
        cnt = jnp.minimum(w, pos + 1).astype(F32)
        pg = (win / cnt - ub[:, c0:c0 + POOL_GROUP]).astype(BF16)
        ys.append(jnp.dot(pg, wpool_ref[g], preferred_element_type=F32))
    bmix = (jnp.concatenate(ys, axis=1) * spool_ref[...]).astype(BF16)

    qm = qm_ref[...]
    ms = []
    for hd in range(N_HEADS_M):
        lg = jnp.dot(qm[:, hd * HD_M:(hd + 1) * HD_M], mkt_ref[hd], preferred_element_type=F32)
        lg = lg * (HD_M ** -0.5)
        e = jnp.exp(lg - jnp.max(lg, axis=1, keepdims=True))
        pr = (e / jnp.sum(e, axis=1, keepdims=True)).astype(BF16)
        ms.append(jnp.dot(pr, mv_ref[hd], preferred_element_type=F32))
    mmix = jnp.concatenate(ms, axis=1).astype(BF16)

    mixed = (gates[:, 0:d] * jnp.dot(a_ref[...], woa_ref[...], preferred_element_type=F32)
             + gates[:, d:2 * d] * jnp.dot(bmix, wob_ref[...], preferred_element_type=F32)
             + gates[:, 2 * d:3 * d] * jnp.dot(mmix, wom_ref[...], preferred_element_type=F32))
    o_ref[...] = x + jnp.dot(mixed.astype(BF16), wout_ref[...], preferred_element_type=F32)


def _merge_call(x, a, ub, hist0, qm, mkt, mv, gmix, wg, woa, wob, wom, wout, wpool, spool, *, tm, pos0):
    b, t, d = x.shape
    nt = t // tm
    hb = tm // HIST_ROWS
    tok = lambda i, j: (i, j, 0)
    per_b3 = lambda i, j: (i, 0, 0)
    per_b4 = lambda i, j: (i, 0, 0, 0)
    in_specs = [
        pl.BlockSpec((None, tm, d), tok),
        pl.BlockSpec((None, tm, D_ATT_A), tok),
        pl.BlockSpec((None, tm, D_POOL), tok),
        pl.BlockSpec((None, HIST_ROWS, D_POOL), lambda i, j: (i, jnp.maximum(j * hb - 1, 0), 0)),
        pl.BlockSpec((None, HIST_ROWS, D_POOL), per_b3),
        pl.BlockSpec((None, tm, D_MEM_ATT), tok),
        pl.BlockSpec((None, N_HEADS_M, HD_M, mkt.shape[3]), per_b4),
        pl.BlockSpec((None, N_HEADS_M, mv.shape[2], HD_M), per_b4),
        _resident((1, d)),
        _resident(wg.shape),
        _resident(woa.shape), _resident(wob.shape), _resident(wom.shape),
        _resident(wout.shape),
        _resident(wpool.shape),
        _resident((1, D_POOL)),
    ]
    return pl.pallas_call(
        functools.partial(_merge_kernel, tm=tm, pos0=pos0), grid=(b, nt), in_specs=in_specs,
        out_specs=pl.BlockSpec((None, tm, d), tok),
        out_shape=jax.ShapeDtypeStruct((b, t, d), F32),
        scratch_shapes=[pltpu.VMEM((HIST_ROWS + tm, D_POOL), F32)],
        compiler_params=_cparams(("parallel", "arbitrary")), name="merge",
    )(x, a, ub, ub, hist0, qm, mkt, mv, gmix, wg, woa, wob, wom, wout, wpool, spool)


def _ffn_kernel(x_ref, g_ref, wgate_ref, wup_ref, wdown_ref, o_ref):
    x = x_ref[...]
    h = _rms(x, g_ref[...]).astype(BF16)
    gate = jnp.dot(h, wgate_ref[...], preferred_element_type=F32)
    up = jnp.dot(h, wup_ref[...], preferred_element_type=F32)
    act = (jax.nn.silu(gate) * up).astype(BF16)
    o_ref[...] = x + jnp.dot(act, wdown_ref[...], preferred_element_type=F32)


def _ffn_call(x2d, g, wgate, wup, wdown, tm):
    n, d = x2d.shape
    row = lambda i: (i, 0)
    return pl.pallas_call(
        _ffn_kernel, grid=(n // tm,),
        in_specs=[pl.BlockSpec((tm, d), row), _resident((1, d)),
                  _resident(wgate.shape), _resident(wup.shape), _resident(wdown.shape)],
        out_specs=pl.BlockSpec((tm, d), row),
        out_shape=jax.ShapeDtypeStruct((n, d), F32),
        compiler_params=_cparams(("parallel",)), name="ffn",
    )(x2d, g, wgate, wup, wdown)


def _rope_tables(pos, head_dim):
    half = head_dim // 2
    inv = ROPE_THETA ** (-jnp.arange(half, dtype=F32) / half)
    ang = pos.astype(F32)[:, None] * inv[None, :]
    reps = LANES // half
    return jnp.tile(jnp.cos(ang), (1, reps)), jnp.tile(jnp.sin(ang), (1, reps))


def _rope_tables_t(pos, head_dim):
    half = head_dim // 2
    inv = ROPE_THETA ** (-jnp.arange(half, dtype=F32) / half)
    ang = inv[:, None] * pos.astype(F32)[None, :]
    return jnp.cos(ang), jnp.sin(ang)


def _pack_weights(w_in, g_qa, g_ka, g_kidx, g_qm):
    d = w_in.shape[0]
    widths = (D_ATT_A, D_KV_A, D_KV_A, D_QIDX, D_IDX, N_IDX_HEADS, D_POOL, D_MEM_ATT, N_BRANCH * d)
    cuts = [int(c) for c in np.cumsum(widths)[:-1]]
    wqa, wka, wva, wqi, wki, wwi, wub, wqm, wgates = jnp.split(w_in, cuts, axis=1)
    pad = jnp.zeros((d, LANES - D_IDX - N_IDX_HEADS), w_in.dtype)
    w_cat = jnp.concatenate([wqa, wka, wva, wqi, wki, wwi, pad, wub, wqm], axis=1).astype(BF16)
    w_rows = jnp.concatenate([wka, wva, wki, wwi, pad, wub, wqm], axis=1).astype(BF16)
    wpad = jnp.zeros((d, T_END - T_WI - N_IDX_HEADS), w_in.dtype)
    w_t = jnp.concatenate([wqa, wqi, wva, wwi, wpad], axis=1).T.astype(BF16)
    gqa_t = jnp.tile(g_qa, N_HEADS_A)[None, :]
    gka_t = jnp.tile(g_ka, N_KV_A)[None, :]
    gki_t = jnp.concatenate([g_kidx, jnp.ones((LANES - D_IDX,), g_kidx.dtype)])[None, :]
    gqa_b = jnp.broadcast_to(g_qa[:, None], (HD_A, LANES))
    return w_cat, w_rows, w_t, wgates.astype(BF16), gqa_t, gka_t, gki_t, g_qm[None, :], gqa_b


def _block_diag_ones(n, group, limit=None):
    i = np.arange(n)
    m = (i[:, None] // group) == (i[None, :] // group)
    if limit is not None:
        m = m & (i[:, None] < limit) & (i[None, :] < limit)
    return jnp.asarray(m, BF16)


def _key_blocks(k_all, v_all, ki_all, kb):
    b, s = k_all.shape[0], k_all.shape[1]
    nkb = -(-s // kb)
    pad = nkb * kb - s
    if pad:
        k_all = jnp.pad(k_all, ((0, 0), (0, pad), (0, 0), (0, 0)))
        v_all = jnp.pad(v_all, ((0, 0), (0, pad), (0, 0), (0, 0)))
        ki_all = jnp.pad(ki_all, ((0, 0), (0, pad), (0, 0)))
    ones = jnp.ones(v_all.shape[:-1] + (1,), BF16)
    zeros = jnp.zeros(v_all.shape[:-1] + (LANES - HD_A - 1,), BF16)
    kaug = jnp.concatenate([k_all.astype(BF16), ones, zeros], axis=-1)
    k = kaug.reshape(b, nkb, kb, N_KV_A, LANES).transpose(3, 0, 1, 2, 4)
    vaug = jnp.concatenate([v_all.astype(BF16), ones, zeros[..., :V_ROWS - HD_A - 1]], axis=-1)
    vt = vaug.reshape(b, nkb, kb, N_KV_A, V_ROWS).transpose(0, 3, 1, 4, 2)
    ki = ki_all.astype(BF16).reshape(b, nkb, kb, D_IDX)
    return k, vt, ki


def _tile(n, pref):
    t = pref
    while n % t:
        t //= 2
    return t


def _group_forward(x, pos, pos0, hist0, k_hist, v_hist, ki_hist, mkt, mv, wts, *, causal):
    (gmix, w_cat, w_rows, w_t, gqa_b, wgates, gqa_t, gka_t, gki_t, gqm, bd64, bdki, wpool, spool,
     woa, wob, wom, wout, gffn, wgate, wup, wdown) = wts
    b, t, d = x.shape
    n = b * t
    tm = _tile(t, 512)
    cosa, sina = _rope_tables(pos, HD_A)
    cosi, sini = _rope_tables(pos, D_IDX)
    tq = _tile(t, 256)
    proj_args = ((cosa, sina, cosi, sini), gqa_t, gka_t, gki_t, gqm, bd64, bdki, tm, t // tm)
    if causal:
        tside = (w_t,) + _rope_tables_t(pos, HD_A) + _rope_tables_t(pos, D_IDX) + (gqa_b,)
        ka, va, kw, ub, qm, qat, qit, wit, kaug, vt, kib = _proj_call(
            x.reshape(n, d), gmix, w_rows, *proj_args, dsa_layout=True, tposed_side=tside)
        kb, s = tm, t
        kblk = kaug.reshape(N_KV_A, b, t // kb, kb, LANES)
        kiblk = kib.reshape(b, t // kb, kb, D_IDX)
    else:
        qa, ka, va, qi, kw, ub, qm = _proj_call(x.reshape(n, d), gmix, w_cat, *proj_args, dsa_layout=False)
        qat = qa.reshape(b, t, D_ATT_A).transpose(0, 2, 1)
        qit = qi.reshape(b, t, D_QIDX).transpose(0, 2, 1)
        wit = kw[:, D_IDX:D_IDX + N_IDX_HEADS].reshape(b, t, N_IDX_HEADS).transpose(0, 2, 1)
    ka = ka.reshape(b, t, N_KV_A, HD_A)
    va = va.reshape(b, t, N_KV_A, HD_A)
    ki = kw[:, :D_IDX].reshape(b, t, D_IDX)
    ub = ub.reshape(b, t, D_POOL)
    if not causal:
        k_all = jnp.concatenate([k_hist, ka], axis=1)
        s, kb = k_all.shape[1], 256
        kblk, vt, kiblk = _key_blocks(k_all, jnp.concatenate([v_hist, va], axis=1),
                                      jnp.concatenate([ki_hist, ki], axis=1), kb)
    topk = min(TOPK_MAX, s // 4)
    a = _dsa_call(qat, qit, wit, kblk, vt, kiblk, tq=tq, kb=kb, causal=causal, s_valid=s, topk=topk)
    if tq % LANES:
        a = a.transpose(0, 2, 1)

    tmm = _tile(t, 512)
    x2 = _merge_call(x, a, ub, hist0, qm.reshape(b, t, D_MEM_ATT), mkt, mv, gmix, wgates,
                     woa, wob, wom, wout, wpool, spool, tm=tmm, pos0=pos0)
    y = _ffn_call(x2.reshape(n, d), gffn, wgate, wup, wdown, _tile(n, 512)).reshape(b, t, d)
    return y, ka, va, ki, ub


def kernel(x_prompt, x_sample, mem_prompt, cache_a_k, cache_a_v, cache_idx_k, cache_pool, cache_mem_k,
           cache_mem_v, g_mix, w_in, g_qa, g_ka, g_kidx, g_qm, g_mem, w_mem_kv, g_km, w_pool, s_pool,
           w_oa, w_ob, w_om, w_out, g_ffn, w_gate, w_up, w_down):
    depth = w_in.shape[0]
    t = x_prompt.shape[1]
    ts = x_sample.shape[1]
    past = cache_a_k.shape[2]
    b = x_prompt.shape[0]
    n_mem = mem_prompt.shape[1]
    pos_p = jnp.arange(t, dtype=jnp.int32)
    pos_s = past + jnp.arange(ts, dtype=jnp.int32)
    bd64 = _block_diag_ones(D_ATT_A, HD_A)
    bdki = _block_diag_ones(LANES, LANES, limit=D_IDX)

    xp, xs = x_prompt, x_sample
    outs = [[] for _ in range(10)]
    for l in range(depth):
        w_cat, w_rows, w_t, wgates, gqa_t, gka_t, gki_t, gqm, gqa_b = _pack_weights(
            w_in[l], g_qa[l], g_ka[l], g_kidx[l], g_qm[l])
        wts = (g_mix[l][None, :], w_cat, w_rows, w_t, gqa_b, wgates, gqa_t, gka_t, gki_t, gqm, bd64, bdki,
               w_pool[l].astype(BF16), s_pool[l][None, :],
               w_oa[l].astype(BF16), w_ob[l].astype(BF16), w_om[l].astype(BF16), w_out[l].astype(BF16),
               g_ffn[l][None, :], w_gate[l].astype(BF16), w_up[l].astype(BF16), w_down[l].astype(BF16))

        mk, mv = _memkv_call(mem_prompt.reshape(b * n_mem, -1), g_mem[l][None, :],
                             w_mem_kv[l].astype(BF16), g_km[l][None, :], _tile(b * n_mem, 256))
        mk = mk.reshape(b, n_mem, N_HEADS_M, HD_M)
        mv = mv.reshape(b, n_mem, N_HEADS_M, HD_M)
        hist0 = jnp.zeros((b, HIST_ROWS, D_POOL), F32)
        xp, ka, va, ki, ub = _group_forward(
            xp, pos_p, 0, hist0, None, None, None,
            mk.astype(BF16).transpose(0, 2, 3, 1), mv.astype(BF16).transpose(0, 2, 1, 3), wts, causal=True)
        for lst, val in zip(outs[:6], (ka, va, ki, ub[:, -POOL_HIST:], mk, mv)):
            lst.append(val)

        bs = xs.shape[0]
        hist0 = jnp.concatenate([jnp.zeros((bs, HIST_ROWS - POOL_HIST, D_POOL), F32), cache_pool[l]], axis=1)
        xs, ka, va, ki, ub = _group_forward(
            xs, pos_s, past, hist0, cache_a_k[l], cache_a_v[l], cache_idx_k[l],
            cache_mem_k[l].astype(BF16).transpose(0, 2, 3, 1), cache_mem_v[l].astype(BF16).transpose(0, 2, 1, 3),
            wts, causal=False)
        pool_s = jnp.concatenate([cache_pool[l], ub], axis=1)[:, -POOL_HIST:]
        for lst, val in zip(outs[6:], (ka, va, ki, pool_s)):
            lst.append(val)

    stacked = [jnp.stack(o) for o in outs]
    return (xp, xs, *stacked)
```

```python
import functools

import jax
import jax.numpy as jnp
import numpy as np
from jax import lax
from jax.experimental import pallas as pl
from jax.experimental.pallas import tpu as pltpu

F32 = jnp.float32
BF16 = jnp.bfloat16

CHUNK = 64
EPS = 1e-6
ROPE_THETA = 10000.0
N_HEADS_A = 8
N_KV_A = 2
GROUP_A = N_HEADS_A // N_KV_A
HD_A = 64
D_ATT_A = N_HEADS_A * HD_A
D_KV_A = N_KV_A * HD_A
N_IDX_HEADS = 8
D_IDX = 32
D_QIDX = N_IDX_HEADS * D_IDX
IDX_SCALE = (N_IDX_HEADS * D_IDX) ** -0.5
TOPK_MAX = 256
POOL_WINDOWS = (2, 4, 8, 16)
POOL_GROUP = 128
D_POOL = len(POOL_WINDOWS) * POOL_GROUP
POOL_HIST = 15
N_HEADS_M = 4
HD_M = 128
D_MEM_ATT = N_HEADS_M * HD_M
N_BRANCH = 3

LANES = 128
SUBLANES = 8
V7X_VMEM_BYTES = 64 * 1024 * 1024
VMEM_LIMIT = 56 * 1024 * 1024

HIST_ROWS = 2 * SUBLANES
MASK_VALUE = -1e30
FLT_MAX = float(np.finfo(np.float32).max)
KEY_BITS = 32
Q_SCALE = HD_A ** -0.5 * float(np.log2(np.e))
SUM_FLOOR = 1e-30
COUNT_ROWS = 4 * SUBLANES
COUNT_CHAINS = 2
COUNT_BLOCKS_PER_STEP = 4
MXU_BLOCKS_PER_STEP = 4
V_ROWS = HD_A + 2 * SUBLANES

C_QA = 0
C_KA = C_QA + D_ATT_A
C_VA = C_KA + D_KV_A
C_QI = C_VA + D_KV_A
C_KW = C_QI + D_QIDX
C_UB = C_KW + LANES
C_QM = C_UB + D_POOL
C_END = C_QM + D_MEM_ATT

R_KA = 0
R_VA = R_KA + D_KV_A
R_KW = R_VA + D_KV_A
R_UB = R_KW + LANES
R_QM = R_UB + D_POOL
R_END = R_QM + D_MEM_ATT
T_QA = 0
T_QI = T_QA + D_ATT_A
T_VA = T_QI + D_QIDX
T_WI = T_VA + D_KV_A
T_END = T_WI + 2 * SUBLANES


def _cparams(sem):
    return pltpu.CompilerParams(dimension_semantics=sem, vmem_limit_bytes=VMEM_LIMIT)


def _resident(shape):
    zeros = (0,) * len(shape)
    return pl.BlockSpec(shape, lambda *_: zeros, pipeline_mode=pl.Buffered(1))


def _rms(x, g):
    ms = jnp.mean(x * x, axis=-1, keepdims=True)
    return x * lax.rsqrt(ms + EPS) * g


def _group_sumsq(x, bd):
    sq = x * x
    hi = sq.astype(BF16)
    lo = (sq - hi.astype(F32)).astype(BF16)
    return (jnp.dot(hi, bd, preferred_element_type=F32)
            + jnp.dot(lo, bd, preferred_element_type=F32))


def _rope_lanes(x, cos, sin, half):
    lane = lax.broadcasted_iota(jnp.int32, x.shape, 1)
    first = (lane % (2 * half)) < half
    left = pltpu.roll(x, LANES - half, 1)
    right = pltpu.roll(x, half, 1)
    rot = jnp.where(first, -left, right)
    return x * cos + rot * sin


def _proj_kernel(x_ref, gmix_ref, w_ref, cosa_ref, sina_ref, cosi_ref, sini_ref,
                 gqa_ref, gka_ref, gki_ref, gqm_ref, bd64_ref, bdki_ref, *refs, dsa_layout):
    if dsa_layout:
        (wt_ref, cosat_ref, sinat_ref, cosit_ref, sinit_ref, gqab_ref,
         ka_ref, va_ref, kw_ref, ub_ref, qm_ref, qat_ref, qit_ref, wit_ref, kaug_ref, vt_ref, kib_ref) = refs
        c_ka, c_va, c_kw, c_ub, c_qm = R_KA, R_VA, R_KW, R_UB, R_QM
    else:
        qa_ref, ka_ref, va_ref, qi_ref, kw_ref, ub_ref, qm_ref = refs
        c_ka, c_va, c_kw, c_ub, c_qm = C_KA, C_VA, C_KW, C_UB, C_QM
    x = x_ref[...]
    tm = x.shape[0]
    h = _rms(x, gmix_ref[...]).astype(BF16)
    p = jnp.dot(h, w_ref[...], preferred_element_type=F32)
    cosa, sina = cosa_ref[...], sina_ref[...]
    cosi, sini = cosi_ref[...], sini_ref[...]
    bd64 = bd64_ref[...]

    if dsa_layout:
        pt = lax.dot_general(wt_ref[...], h, (((1,), (1,)), ((), ())), preferred_element_type=F32)
        cosat, sinat = cosat_ref[...], sinat_ref[...]
        gq = jnp.concatenate([gqab_ref[...]] * (tm // LANES), axis=1)
        ha = HD_A // 2
        for hd in range(N_HEADS_A):
            xh = pt[T_QA + hd * HD_A:T_QA + (hd + 1) * HD_A, :]
            xh = xh * lax.rsqrt(jnp.sum(xh * xh, axis=0, keepdims=True) * (1.0 / HD_A) + EPS) * gq
            x1, x2 = xh[:ha, :], xh[ha:, :]
            qat_ref[hd * HD_A:hd * HD_A + ha, :] = ((x1 * cosat - x2 * sinat) * Q_SCALE).astype(BF16)
            qat_ref[hd * HD_A + ha:(hd + 1) * HD_A, :] = ((x2 * cosat + x1 * sinat) * Q_SCALE).astype(BF16)
        cosit, sinit = cosit_ref[...], sinit_ref[...]
        hi = D_IDX // 2
        for hd in range(N_IDX_HEADS):
            x1 = pt[T_QI + hd * D_IDX:T_QI + hd * D_IDX + hi, :]
            x2 = pt[T_QI + hd * D_IDX + hi:T_QI + (hd + 1) * D_IDX, :]
            qit_ref[hd * D_IDX:hd * D_IDX + hi, :] = (x1 * cosit - x2 * sinit).astype(BF16)
            qit_ref[hd * D_IDX + hi:(hd + 1) * D_IDX, :] = (x2 * cosit + x1 * sinit).astype(BF16)
        wit_ref[...] = pt[T_WI:T_WI + N_IDX_HEADS, :] * IDX_SCALE
    else:
        qa = p[:, C_QA:C_QA + D_ATT_A]
        ss = _group_sumsq(qa, bd64)
        qa = qa * lax.rsqrt(ss * (1.0 / HD_A) + EPS) * gqa_ref[...]
        for c in range(D_ATT_A // LANES):
            blk = _rope_lanes(qa[:, c * LANES:(c + 1) * LANES], cosa, sina, HD_A // 2) * Q_SCALE
            qa_ref[:, c * LANES:(c + 1) * LANES] = blk.astype(BF16)
        for c in range(D_QIDX // LANES):
            blk = _rope_lanes(p[:, C_QI + c * LANES:C_QI + (c + 1) * LANES], cosi, sini, D_IDX // 2)
            qi_ref[:, c * LANES:(c + 1) * LANES] = blk.astype(BF16)

    ka = p[:, c_ka:c_ka + D_KV_A]
    ss = _group_sumsq(ka, bd64[:D_KV_A, :D_KV_A])
    ka = ka * lax.rsqrt(ss * (1.0 / HD_A) + EPS) * gka_ref[...]
    ka = _rope_lanes(ka, cosa, sina, HD_A // 2)
    va = p[:, c_va:c_va + D_KV_A]
    for g in range(N_KV_A):
        ka_ref[:, g, :] = (ka if g == 0 else pltpu.roll(ka, LANES - g * HD_A, 1))[:, :HD_A]
        va_ref[:, g, :] = (va if g == 0 else pltpu.roll(va, LANES - g * HD_A, 1))[:, :HD_A]
    if dsa_layout:
        lane = lax.broadcasted_iota(jnp.int32, ka.shape, 1)
        tail = jnp.where(lane == HD_A, 1.0, 0.0)
        kaug_ref[0] = jnp.where(lane < HD_A, ka, tail).astype(BF16)
        kaug_ref[1] = jnp.where(lane < HD_A, pltpu.roll(ka, HD_A, 1), tail).astype(BF16)
        srow = lax.broadcasted_iota(jnp.int32, (V_ROWS - HD_A, tm), 0)
        ones_rows = jnp.where(srow == 0, 1.0, 0.0).astype(BF16)
        for g in range(N_KV_A):
            vt_ref[g, 0:HD_A, :] = pt[T_VA + g * HD_A:T_VA + (g + 1) * HD_A, :].astype(BF16)
            vt_ref[g, HD_A:V_ROWS, :] = ones_rows

    kw = p[:, c_kw:c_kw + LANES]
    ss = _group_sumsq(kw, bdki_ref[...])
    kin = kw * lax.rsqrt(ss * (1.0 / D_IDX) + EPS) * gki_ref[...]
    kin = _rope_lanes(kin, cosi, sini, D_IDX // 2)
    lane = lax.broadcasted_iota(jnp.int32, kw.shape, 1)
    kw = jnp.where(lane < D_IDX, kin, kw * IDX_SCALE)
    kw_ref[...] = kw
    if dsa_layout:
        kib_ref[...] = kw[:, :D_IDX].astype(BF16)

    ub_ref[...] = p[:, c_ub:c_ub + D_POOL]

    gqm = gqm_ref[...]
    for hd in range(N_HEADS_M):
        blk = p[:, c_qm + hd * HD_M:c_qm + (hd + 1) * HD_M]
        qm_ref[:, hd * HD_M:(hd + 1) * HD_M] = _rms(blk, gqm).astype(BF16)


def _proj_call(x2d, gmix, w_rows, tabs, gqa_t, gka_t, gki_t, gqm, bd64, bdki, tm, n_tab_blocks,
               dsa_layout, tposed_side=()):
    n = x2d.shape[0]
    d = x2d.shape[1]
    nb = n // (tm * n_tab_blocks)
    grid = (n // tm,)
    row = lambda i: (i, 0)
    tab = lambda i: (i % n_tab_blocks, 0)
    tabt = lambda i: (0, i % n_tab_blocks)
    in_specs = [
        pl.BlockSpec((tm, d), row),
        _resident((1, d)),
        _resident(w_rows.shape),
        pl.BlockSpec((tm, LANES), tab), pl.BlockSpec((tm, LANES), tab),
        pl.BlockSpec((tm, LANES), tab), pl.BlockSpec((tm, LANES), tab),
        _resident((1, D_ATT_A)), _resident((1, D_KV_A)),
        _resident((1, LANES)), _resident((1, HD_M)),
        _resident((D_ATT_A, D_ATT_A)), _resident((LANES, LANES)),
    ]
    if dsa_layout:
        in_specs += [_resident((T_END, d)),
                     pl.BlockSpec((HD_A // 2, tm), tabt), pl.BlockSpec((HD_A // 2, tm), tabt),
                     pl.BlockSpec((D_IDX // 2, tm), tabt), pl.BlockSpec((D_IDX // 2, tm), tabt),
                     _resident((HD_A, LANES))]
    rows = lambda width, dtype: (jax.ShapeDtypeStruct((n, width), dtype), pl.BlockSpec((tm, width), row))
    heads = (jax.ShapeDtypeStruct((n, N_KV_A, HD_A), F32), pl.BlockSpec((tm, N_KV_A, HD_A), lambda i: (i, 0, 0)))
    per_token = [heads, heads, rows(LANES, F32), rows(D_POOL, F32),
                 rows(D_MEM_ATT, BF16)]
    if dsa_layout:
        t = tm * n_tab_blocks
        cols = lambda i: (i // n_tab_blocks, 0, i % n_tab_blocks)
        tposed = lambda r, dtype: (jax.ShapeDtypeStruct((nb, r, t), dtype), pl.BlockSpec((None, r, tm), cols))
        outs = per_token + [
            tposed(D_ATT_A, BF16), tposed(D_QIDX, BF16), tposed(N_IDX_HEADS, F32),
            (jax.ShapeDtypeStruct((N_KV_A, n, LANES), BF16), pl.BlockSpec((N_KV_A, tm, LANES), lambda i: (0, i, 0))),
            (jax.ShapeDtypeStruct((nb, N_KV_A, n_tab_blocks, V_ROWS, tm), BF16),
             pl.BlockSpec((None, N_KV_A, None, V_ROWS, tm),
                          lambda i: (i // n_tab_blocks, 0, i % n_tab_blocks, 0, 0))),
            rows(D_IDX, BF16),
        ]
    else:
        outs = [rows(D_ATT_A, BF16)] + per_token[:2] + [rows(D_QIDX, BF16)] + per_token[2:]
    return pl.pallas_call(
        functools.partial(_proj_kernel, dsa_layout=dsa_layout), grid=grid, in_specs=in_specs,
        out_specs=[o[1] for o in outs], out_shape=[o[0] for o in outs],
        compiler_params=_cparams(("parallel",)), name="proj",
    )(x2d, gmix, w_rows, *tabs, gqa_t, gka_t, gki_t, gqm, bd64, bdki, *tposed_side)


def _key_bits_to_float(u):
    k = u ^ jnp.int32(-2147483648)
    bits = jnp.where(k >= 0, k, k ^ jnp.int32(0x7FFFFFFF))
    return lax.bitcast_convert_type(bits, F32)


def _dsa_kernel(qat_ref, qit_ref, wit_ref, k_ref, vt_ref, ki_ref, o_ref,
                sc_ref, m_ref, acc_ref, qaug_ref, qis_ref, km_ref,
                *, tq, kb, nkb_total, causal, s_valid, topk, rows_out):
    qblk = pl.program_id(1)
    if causal:
        nkb = lax.div(qblk + kb // tq, jnp.int32(kb // tq))
    else:
        nkb = nkb_total
    qpos = qblk * tq + lax.broadcasted_iota(jnp.int32, (1, tq), 1)
    if causal:
        lim = (lax.shift_right_logical(qpos, 6) + 1) * CHUNK
    else:
        lim = jnp.full((1, tq), s_valid, jnp.int32)
    limf = lim.astype(F32)
    krow = lax.broadcasted_iota(jnp.int32, (kb, tq), 0)
    kf = float(topk)

    def for_each_block_pairwise(fn):
        n = jnp.int32(nkb)
        groups = lax.div(n, MXU_BLOCKS_PER_STEP)

        def step(jj, carry):
            for w in range(MXU_BLOCKS_PER_STEP):
                fn(MXU_BLOCKS_PER_STEP * jj + w)
            return carry
        lax.fori_loop(0, groups, step, 0)
        done = groups * MXU_BLOCKS_PER_STEP

        @pl.when(n - done >= 2)
        def _():
            fn(done)
            fn(done + 1)

        @pl.when(lax.rem(n, 2) == 1)
        def _():
            fn(n - 1)

    for h in range(N_IDX_HEADS):
        qis_ref[:, h * tq:(h + 1) * tq] = qit_ref[h * D_IDX:(h + 1) * D_IDX, :]
    qit = qis_ref[...]
    wit = wit_ref[...]

    def score_block(j):
        z = jnp.dot(ki_ref[j], qit, preferred_element_type=F32)
        s = jnp.zeros((kb, tq), F32)
        for h in range(N_IDX_HEADS):
            s = s + wit[h:h + 1, :] * jnp.maximum(z[:, h * tq:(h + 1) * tq], 0.0)
        sc_ref[j] = jnp.where(j * kb + krow < lim, s, -jnp.inf)

    for_each_block_pairwise(score_block)

    crow = lax.broadcasted_iota(jnp.int32, (COUNT_ROWS, tq), 0)

    def count(pred):
        def tally(j, accs):
            accs = list(accs)
            for ci, r in enumerate(range(0, kb, COUNT_ROWS)):
                hit = pred(sc_ref[j, r:r + COUNT_ROWS, :], j * kb + r + crow)
                accs[ci % COUNT_CHAINS] = accs[ci % COUNT_CHAINS] + jnp.where(hit, 1.0, 0.0)
            return tuple(accs)

        def tally_group(jj, accs):
            for w in range(COUNT_BLOCKS_PER_STEP):
                accs = tally(COUNT_BLOCKS_PER_STEP * jj + w, accs)
            return accs

        n = jnp.int32(nkb)
        groups = lax.div(n, COUNT_BLOCKS_PER_STEP)
        zero = jnp.zeros((COUNT_ROWS, tq), F32)
        accs = lax.fori_loop(0, groups, tally_group, (zero,) * COUNT_CHAINS)
        accs = lax.fori_loop(groups * COUNT_BLOCKS_PER_STEP, n, tally, accs)
        return jnp.sum(sum(accs[1:], accs[0]), axis=0, keepdims=True)

    def radix_body(i, carry):
        prefix, cge, cgt = carry
        cand = prefix | lax.shift_left(jnp.int32(1), KEY_BITS - 1 - i)
        t = _key_bits_to_float(cand)
        c = count(lambda s, idx: s >= t)
        take = c >= kf
        return jnp.where(take, cand, prefix), jnp.where(take, c, cge), jnp.where(take, cgt, c)

    prefix, cge, cgt = lax.fori_loop(
        0, KEY_BITS, radix_body, (jnp.zeros((1, tq), jnp.int32), limf, jnp.zeros((1, tq), F32)))
    thr = jnp.where(limf >= kf, _key_bits_to_float(prefix), -FLT_MAX)

    any_tie = jnp.max(jnp.where(cge > kf, 1.0, 0.0), axis=1, keepdims=True)

    def mask_plain():
        return thr

    def mask_ties():
        need = kf - cgt
        tri = (lax.broadcasted_iota(jnp.int32, (kb, kb), 1)
               <= lax.broadcasted_iota(jnp.int32, (kb, kb), 0)).astype(BF16)

        def body(j, seen):
            s = sc_ref[j]
            eq = s == thr
            rank = jnp.dot(tri, jnp.where(eq, 1.0, 0.0).astype(BF16), preferred_element_type=F32) + seen
            keep = (s > thr) | (eq & (rank <= need))
            sc_ref[j] = jnp.where(keep, 0.0, MASK_VALUE)
            return rank[kb - 1:kb, :]

        lax.fori_loop(0, nkb, body, jnp.zeros((1, tq), F32))
        return jnp.full((1, tq), 0.5 * MASK_VALUE, F32)

    keep_from = lax.cond(any_tie[0, 0] > 0.0, mask_ties, mask_plain)

    @pl.when(qblk == 0)
    def _():
        for g in range(N_KV_A):
            def norm_body(j, mx, g=g):
                kk = k_ref[g, j].astype(F32)
                return jnp.maximum(mx, jnp.sum(kk * kk, axis=1, keepdims=True))
            mx = lax.fori_loop(0, nkb_total, norm_body, jnp.zeros((kb, 1), F32))
            km_ref[g] = jnp.broadcast_to(jnp.max(mx, axis=0, keepdims=True), (SUBLANES, LANES))

    aug_row = lax.broadcasted_iota(jnp.int32, (LANES - HD_A, tq), 0)
    for g in range(N_KV_A):
        kmax2 = km_ref[g][0:1, :]
        kmax2 = jnp.concatenate([kmax2] * (tq // LANES), axis=1) if tq >= LANES else kmax2[:, :tq]
        for u in range(GROUP_A):
            hd = g * GROUP_A + u
            q = qat_ref[hd * HD_A:(hd + 1) * HD_A, :]
            qf = q.astype(F32)
            shift = jnp.sqrt(jnp.sum(qf * qf, axis=0, keepdims=True) * kmax2)
            qaug_ref[g, 0:HD_A, u * tq:(u + 1) * tq] = q
            qaug_ref[g, HD_A:LANES, u * tq:(u + 1) * tq] = jnp.where(aug_row == 0, -shift, 0.0).astype(BF16)

    acc_ref[...] = jnp.zeros(acc_ref.shape, F32)

    def attend_block(j):
        keep = sc_ref[j] >= keep_from
        for g in range(N_KV_A):
            lg = jnp.dot(k_ref[g, j], qaug_ref[g], preferred_element_type=F32)
            vt = vt_ref[g, j]
            for u in range(GROUP_A):
                hd = g * GROUP_A + u
                p = jnp.exp2(jnp.where(keep, lg[:, u * tq:(u + 1) * tq], MASK_VALUE)).astype(BF16)
                acc_ref[hd] += jnp.dot(vt, p, preferred_element_type=F32)

    for_each_block_pairwise(attend_block)

    sums_ok = jnp.ones((1, tq), F32)
    for hd in range(N_HEADS_A):
        sums_ok = jnp.where(acc_ref[hd][HD_A:HD_A + 1, :] > SUM_FLOOR, sums_ok, 0.0)
    all_ok = jnp.min(sums_ok, axis=1, keepdims=True)

    @pl.when(all_ok[0, 0] < 1.0)
    def _():
        m_ref[...] = jnp.full(m_ref.shape, MASK_VALUE, F32)
        acc_ref[...] = jnp.zeros(acc_ref.shape, F32)

        def online_body(j, carry):
            keep = sc_ref[j] >= keep_from
            for g in range(N_KV_A):
                lg = jnp.dot(k_ref[g, j], qaug_ref[g], preferred_element_type=F32)
                vt = vt_ref[g, j]
                for u in range(GROUP_A):
                    hd = g * GROUP_A + u
                    l = jnp.where(keep, lg[:, u * tq:(u + 1) * tq], MASK_VALUE)
                    m_old = m_ref[hd:hd + 1, :]
                    m_new = jnp.maximum(m_old, jnp.max(l, axis=0, keepdims=True))
                    p = jnp.exp2(l - m_new).astype(BF16)
                    pv = jnp.dot(vt, p, preferred_element_type=F32)
                    acc_ref[hd] = jnp.exp2(m_old - m_new) * acc_ref[hd] + pv
                    m_ref[hd:hd + 1, :] = m_new
            return carry

        lax.fori_loop(0, nkb, online_body, 0)

    heads = []
    for hd in range(N_HEADS_A):
        a = acc_ref[hd]
        heads.append(a[:HD_A, :] / a[HD_A:HD_A + 1, :])
    out_t = jnp.concatenate(heads, axis=0)
    o_ref[...] = (out_t.T if rows_out else out_t).astype(BF16)


def _dsa_call(qat, qit, wit, k, vt, ki, *, tq, kb, causal, s_valid, topk):
    b, t = qat.shape[0], qat.shape[2]
    nkb_total = k.shape[2]
    rows_out = tq % LANES == 0
    kern = functools.partial(_dsa_kernel, tq=tq, kb=kb, nkb_total=nkb_total, causal=causal,
                             s_valid=s_valid, topk=topk, rows_out=rows_out)
    qcols = lambda i, j: (i, 0, j)
    in_specs = [
        pl.BlockSpec((None, D_ATT_A, tq), qcols),
        pl.BlockSpec((None, D_QIDX, tq), qcols),
        pl.BlockSpec((None, N_IDX_HEADS, tq), qcols),
        pl.BlockSpec((N_KV_A, None, nkb_total, kb, LANES), lambda i, j: (0, i, 0, 0, 0)),
        pl.BlockSpec((None, N_KV_A, nkb_total, V_ROWS, kb), lambda i, j: (i, 0, 0, 0, 0)),
        pl.BlockSpec((None, nkb_total, kb, D_IDX), lambda i, j: (i, 0, 0, 0)),
    ]
    if rows_out:
        out_spec = pl.BlockSpec((None, tq, D_ATT_A), lambda i, j: (i, j, 0))
        out_shape = jax.ShapeDtypeStruct((b, t, D_ATT_A), BF16)
    else:
        out_spec = pl.BlockSpec((None, D_ATT_A, tq), qcols)
        out_shape = jax.ShapeDtypeStruct((b, D_ATT_A, t), BF16)
    return pl.pallas_call(
        kern, grid=(b, t // tq), in_specs=in_specs, out_specs=out_spec, out_shape=out_shape,
        scratch_shapes=[pltpu.VMEM((nkb_total, kb, tq), F32),
                        pltpu.VMEM((N_HEADS_A, tq), F32),
                        pltpu.VMEM((N_HEADS_A, V_ROWS, tq), F32),
                        pltpu.VMEM((N_KV_A, LANES, GROUP_A * tq), BF16),
                        pltpu.VMEM((D_IDX, N_IDX_HEADS * tq), BF16),
                        pltpu.VMEM((N_KV_A, SUBLANES, LANES), F32)],
        compiler_params=_cparams(("parallel", "arbitrary")), name="dsa",
    )(qat, qit, wit, k, vt, ki)


def _memkv_kernel(mem_ref, gmem_ref, w_ref, gkm_ref, k_ref, v_ref):
    h = _rms(mem_ref[...], gmem_ref[...]).astype(BF16)
    kv = jnp.dot(h, w_ref[...], preferred_element_type=F32)
    gkm = gkm_ref[...]
    for hd in range(N_HEADS_M):
        k_ref[:, hd * HD_M:(hd + 1) * HD_M] = _rms(kv[:, hd * HD_M:(hd + 1) * HD_M], gkm)
    v_ref[...] = kv[:, D_MEM_ATT:]


def _memkv_call(mem2d, gmem, w_kv, gkm, tm):
    n, d = mem2d.shape
    row = lambda i: (i, 0)
    return pl.pallas_call(
        _memkv_kernel, grid=(n // tm,),
        in_specs=[pl.BlockSpec((tm, d), row), _resident((1, d)),
                  _resident((d, 2 * D_MEM_ATT)), _resident((1, HD_M))],
        out_specs=[pl.BlockSpec((tm, D_MEM_ATT), row), pl.BlockSpec((tm, D_MEM_ATT), row)],
        out_shape=[jax.ShapeDtypeStruct((n, D_MEM_ATT), F32), jax.ShapeDtypeStruct((n, D_MEM_ATT), F32)],
        compiler_params=_cparams(("parallel",)), name="memkv",
    )(mem2d, gmem, w_kv, gkm)


def _merge_kernel(x_ref, a_ref, ub_ref, prev_ref, hist0_ref, qm_ref, mkt_ref, mv_ref,
                  gmix_ref, wg_ref, woa_ref, wob_ref, wom_ref, wout_ref, wpool_ref, spool_ref,
                  o_ref, ext_ref, *, tm, pos0):
    it = pl.program_id(1)
    x = x_ref[...]

    h = _rms(x, gmix_ref[...]).astype(BF16)
    gates = jax.nn.sigmoid(jnp.dot(h, wg_ref[...], preferred_element_type=F32))
    d = x.shape[1]

    ub = ub_ref[...]
    ext_ref[0:HIST_ROWS, :] = jnp.where(it == 0, hist0_ref[...], prev_ref[...])
    ext_ref[HIST_ROWS:HIST_ROWS + tm, :] = ub
    pos = pos0 + it * tm + lax.broadcasted_iota(jnp.int32, (tm, 1), 0)
    ys = []
    for g, w in enumerate(POOL_WINDOWS):
        c0 = g * POOL_GROUP
        win = ub[:, c0:c0 + POOL_GROUP]
        for k in range(1, w):
            win = win + ext_ref[HIST_ROWS - k:HIST_ROWS - k + tm, c0:c0 + POOL_GROUP]
        cnt = jnp.minimum(w, pos + 1).astype(F32)
        pg = (win / cnt - ub[:, c0:c0 + POOL_GROUP]).astype(BF16)
        ys.append(jnp.dot(pg, wpool_ref[g], preferred_element_type=F32))
    bmix = (jnp.concatenate(ys, axis=1) * spool_ref[...]).astype(BF16)

    qm = qm_ref[...]
    ms = []
    for hd in range(N_HEADS_M):
        lg = jnp.dot(qm[:, hd * HD_M:(hd + 1) * HD_M], mkt_ref[hd], preferred_element_type=F32)
        lg = lg * (HD_M ** -0.5)
        e = jnp.exp(lg - jnp.max(lg, axis=1, keepdims=True))
        pr = (e / jnp.sum(e, axis=1, keepdims=True)).astype(BF16)
        ms.append(jnp.dot(pr, mv_ref[hd], preferred_element_type=F32))
    mmix = jnp.concatenate(ms, axis=1).astype(BF16)

    mixed = (gates[:, 0:d] * jnp.dot(a_ref[...], woa_ref[...], preferred_element_type=F32)
             + gates[:, d:2 * d] * jnp.dot(bmix, wob_ref[...], preferred_element_type=F32)
             + gates[:, 2 * d:3 * d] * jnp.dot(mmix, wom_ref[...], preferred_element_type=F32))
    o_ref[...] = x + jnp.dot(mixed.astype(BF16), wout_ref[...], preferred_element_type=F32)


def _merge_call(x, a, ub, hist0, qm, mkt, mv, gmix, wg, woa, wob, wom, wout, wpool, spool, *, tm, pos0):
    b, t, d = x.shape
    nt = t // tm
    hb = tm // HIST_ROWS
    tok = lambda i, j: (i, j, 0)
    per_b3 = lambda i, j: (i, 0, 0)
    per_b4 = lambda i, j: (i, 0, 0, 0)
    in_specs = [
        pl.BlockSpec((None, tm, d), tok),
        pl.BlockSpec((None, tm, D_ATT_A), tok),
        pl.BlockSpec((None, tm, D_POOL), tok),
        pl.BlockSpec((None, HIST_ROWS, D_POOL), lambda i, j: (i, jnp.maximum(j * hb - 1, 0), 0)),
        pl.BlockSpec((None, HIST_ROWS, D_POOL), per_b3),
        pl.BlockSpec((None, tm, D_MEM_ATT), tok),
        pl.BlockSpec((None, N_HEADS_M, HD_M, mkt.shape[3]), per_b4),
        pl.BlockSpec((None, N_HEADS_M, mv.shape[2], HD_M), per_b4),
        _resident((1, d)),
        _resident(wg.shape),
        _resident(woa.shape), _resident(wob.shape), _resident(wom.shape),
        _resident(wout.shape),
        _resident(wpool.shape),
        _resident((1, D_POOL)),
    ]
    return pl.pallas_call(
        functools.partial(_merge_kernel, tm=tm, pos0=pos0), grid=(b, nt), in_specs=in_specs,
        out_specs=pl.BlockSpec((None, tm, d), tok),
        out_shape=jax.ShapeDtypeStruct((b, t, d), F32),
        scratch_shapes=[pltpu.VMEM((HIST_ROWS + tm, D_POOL), F32)],
        compiler_params=_cparams(("parallel", "arbitrary")), name="merge",
    )(x, a, ub, ub, hist0, qm, mkt, mv, gmix, wg, woa, wob, wom, wout, wpool, spool)


def _ffn_kernel(x_ref, g_ref, wgate_ref, wup_ref, wdown_ref, o_ref):
    x = x_ref[...]
    h = _rms(x, g_ref[...]).astype(BF16)
    gate = jnp.dot(h, wgate_ref[...], preferred_element_type=F32)
    up = jnp.dot(h, wup_ref[...], preferred_element_type=F32)
    act = (jax.nn.silu(gate) * up).astype(BF16)
    o_ref[...] = x + jnp.dot(act, wdown_ref[...], preferred_element_type=F32)


def _ffn_call(x2d, g, wgate, wup, wdown, tm):
    n, d = x2d.shape
    row = lambda i: (i, 0)
    return pl.pallas_call(
        _ffn_kernel, grid=(n // tm,),
        in_specs=[pl.BlockSpec((tm, d), row), _resident((1, d)),
                  _resident(wgate.shape), _resident(wup.shape), _resident(wdown.shape)],
        out_specs=pl.BlockSpec((tm, d), row),
        out_shape=jax.ShapeDtypeStruct((n, d), F32),
        compiler_params=_cparams(("parallel",)), name="ffn",
    )(x2d, g, wgate, wup, wdown)


def _rope_tables(pos, head_dim):
    half = head_dim // 2
    inv = ROPE_THETA ** (-jnp.arange(half, dtype=F32) / half)
    ang = pos.astype(F32)[:, None] * inv[None, :]
    reps = LANES // half
    return jnp.tile(jnp.cos(ang), (1, reps)), jnp.tile(jnp.sin(ang), (1, reps))


def _rope_tables_t(pos, head_dim):
    half = head_dim // 2
    inv = ROPE_THETA ** (-jnp.arange(half, dtype=F32) / half)
    ang = inv[:, None] * pos.astype(F32)[None, :]
    return jnp.cos(ang), jnp.sin(ang)


def _pack_weights(w_in, g_qa, g_ka, g_kidx, g_qm):
    d = w_in.shape[0]
    widths = (D_ATT_A, D_KV_A, D_KV_A, D_QIDX, D_IDX, N_IDX_HEADS, D_POOL, D_MEM_ATT, N_BRANCH * d)
    cuts = [int(c) for c in np.cumsum(widths)[:-1]]
    wqa, wka, wva, wqi, wki, wwi, wub, wqm, wgates = jnp.split(w_in, cuts, axis=1)
    pad = jnp.zeros((d, LANES - D_IDX - N_IDX_HEADS), w_in.dtype)
    w_cat = jnp.concatenate([wqa, wka, wva, wqi, wki, wwi, pad, wub, wqm], axis=1).astype(BF16)
    w_rows = jnp.concatenate([wka, wva, wki, wwi, pad, wub, wqm], axis=1).astype(BF16)
    wpad = jnp.zeros((d, T_END - T_WI - N_IDX_HEADS), w_in.dtype)
    w_t = jnp.concatenate([wqa, wqi, wva, wwi, wpad], axis=1).T.astype(BF16)
    gqa_t = jnp.tile(g_qa, N_HEADS_A)[None, :]
    gka_t = jnp.tile(g_ka, N_KV_A)[None, :]
    gki_t = jnp.concatenate([g_kidx, jnp.ones((LANES - D_IDX,), g_kidx.dtype)])[None, :]
    gqa_b = jnp.broadcast_to(g_qa[:, None], (HD_A, LANES))
    return w_cat, w_rows, w_t, wgates.astype(BF16), gqa_t, gka_t, gki_t, g_qm[None, :], gqa_b


def _block_diag_ones(n, group, limit=None):
    i = np.arange(n)
    m = (i[:, None] // group) == (i[None, :] // group)
    if limit is not None:
        m = m & (i[:, None] < limit) & (i[None, :] < limit)
    return jnp.asarray(m, BF16)


def _key_blocks(k_all, v_all, ki_all, kb):
    b, s = k_all.shape[0], k_all.shape[1]
    nkb = -(-s // kb)
    pad = nkb * kb - s
    if pad:
        k_all = jnp.pad(k_all, ((0, 0), (0, pad), (0, 0), (0, 0)))
        v_all = jnp.pad(v_all, ((0, 0), (0, pad), (0, 0), (0, 0)))
        ki_all = jnp.pad(ki_all, ((0, 0), (0, pad), (0, 0)))
    ones = jnp.ones(v_all.shape[:-1] + (1,), BF16)
    zeros = jnp.zeros(v_all.shape[:-1] + (LANES - HD_A - 1,), BF16)
    kaug = jnp.concatenate([k_all.astype(BF16), ones, zeros], axis=-1)
    k = kaug.reshape(b, nkb, kb, N_KV_A, LANES).transpose(3, 0, 1, 2, 4)
    vaug = jnp.concatenate([v_all.astype(BF16), ones, zeros[..., :V_ROWS - HD_A - 1]], axis=-1)
    vt = vaug.reshape(b, nkb, kb, N_KV_A, V_ROWS).transpose(0, 3, 1, 4, 2)
    ki = ki_all.astype(BF16).reshape(b, nkb, kb, D_IDX)
    return k, vt, ki


def _tile(n, pref):
    t = pref
    while n % t:
        t //= 2
    return t


def _group_forward(x, pos, pos0, hist0, k_hist, v_hist, ki_hist, mkt, mv, wts, *, causal):
    (gmix, w_cat, w_rows, w_t, gqa_b, wgates, gqa_t, gka_t, gki_t, gqm, bd64, bdki, wpool, spool,
     woa, wob, wom, wout, gffn, wgate, wup, wdown) = wts
    b, t, d = x.shape
    n = b * t
    tm = _tile(t, 512)
    cosa, sina = _rope_tables(pos, HD_A)
    cosi, sini = _rope_tables(pos, D_IDX)
    tq = _tile(t, 256)
    proj_args = ((cosa, sina, cosi, sini), gqa_t, gka_t, gki_t, gqm, bd64, bdki, tm, t // tm)
    if causal:
        tside = (w_t,) + _rope_tables_t(pos, HD_A) + _rope_tables_t(pos, D_IDX) + (gqa_b,)
        ka, va, kw, ub, qm, qat, qit, wit, kaug, vt, kib = _proj_call(
            x.reshape(n, d), gmix, w_rows, *proj_args, dsa_layout=True, tposed_side=tside)
        kb, s = tm, t
        kblk = kaug.reshape(N_KV_A, b, t // kb, kb, LANES)
        kiblk = kib.reshape(b, t // kb, kb, D_IDX)
    else:
        qa, ka, va, qi, kw, ub, qm = _proj_call(x.reshape(n, d), gmix, w_cat, *proj_args, dsa_layout=False)
        qat = qa.reshape(b, t, D_ATT_A).transpose(0, 2, 1)
        qit = qi.reshape(b, t, D_QIDX).transpose(0, 2, 1)
        wit = kw[:, D_IDX:D_IDX + N_IDX_HEADS].reshape(b, t, N_IDX_HEADS).transpose(0, 2, 1)
    ka = ka.reshape(b, t, N_KV_A, HD_A)
    va = va.reshape(b, t, N_KV_A, HD_A)
    ki = kw[:, :D_IDX].reshape(b, t, D_IDX)
    ub = ub.reshape(b, t, D_POOL)
    if not causal:
        k_all = jnp.concatenate([k_hist, ka], axis=1)
        s, kb = k_all.shape[1], 256
        kblk, vt, kiblk = _key_blocks(k_all, jnp.concatenate([v_hist, va], axis=1),
                                      jnp.concatenate([ki_hist, ki], axis=1), kb)
    topk = min(TOPK_MAX, s // 4)
    a = _dsa_call(qat, qit, wit, kblk, vt, kiblk, tq=tq, kb=kb, causal=causal, s_valid=s, topk=topk)
    if tq % LANES:
        a = a.transpose(0, 2, 1)

    tmm = _tile(t, 512)
    x2 = _merge_call(x, a, ub, hist0, qm.reshape(b, t, D_MEM_ATT), mkt, mv, gmix, wgates,
                     woa, wob, wom, wout, wpool, spool, tm=tmm, pos0=pos0)
    y = _ffn_call(x2.reshape(n, d), gffn, wgate, wup, wdown, _tile(n, 512)).reshape(b, t, d)
    return y, ka, va, ki, ub


def kernel(x_prompt, x_sample, mem_prompt, cache_a_k, cache_a_v, cache_idx_k, cache_pool, cache_mem_k,
           cache_mem_v, g_mix, w_in, g_qa, g_ka, g_kidx, g_qm, g_mem, w_mem_kv, g_km, w_pool, s_pool,
           w_oa, w_ob, w_om, w_out, g_ffn, w_gate, w_up, w_down):
    depth = w_in.shape[0]
    t = x_prompt.shape[1]
    ts = x_sample.shape[1]
    past = cache_a_k.shape[2]
    b = x_prompt.shape[0]
    n_mem = mem_prompt.shape[1]
    pos_p = jnp.arange(t, dtype=jnp.int32)
    pos_s = past + jnp.arange(ts, dtype=jnp.int32)
    bd64 = _block_diag_ones(D_ATT_A, HD_A)
    bdki = _block_diag_ones(LANES, LANES, limit=D_IDX)

    xp, xs = x_prompt, x_sample
    outs = [[] for _ in range(10)]
    for l in range(depth):
        w_cat, w_rows, w_t, wgates, gqa_t, gka_t, gki_t, gqm, gqa_b = _pack_weights(
            w_in[l], g_qa[l], g_ka[l], g_kidx[l], g_qm[l])
        wts = (g_mix[l][None, :], w_cat, w_rows, w_t, gqa_b, wgates, gqa_t, gka_t, gki_t, gqm, bd64, bdki,
               w_pool[l].astype(BF16), s_pool[l][None, :],
               w_oa[l].astype(BF16), w_ob[l].astype(BF16), w_om[l].astype(BF16), w_out[l].astype(BF16),
               g_ffn[l][None, :], w_gate[l].astype(BF16), w_up[l].astype(BF16), w_down[l].astype(BF16))

        mk, mv = _memkv_call(mem_prompt.reshape(b * n_mem, -1), g_mem[l][None, :],
                             w_mem_kv[l].astype(BF16), g_km[l][None, :], _tile(b * n_mem, 256))
        mk = mk.reshape(b, n_mem, N_HEADS_M, HD_M)
        mv = mv.reshape(b, n_mem, N_HEADS_M, HD_M)
        hist0 = jnp.zeros((b, HIST_ROWS, D_POOL), F32)
        xp, ka, va, ki, ub = _group_forward(
            xp, pos_p, 0, hist0, None, None, None,
            mk.astype(BF16).transpose(0, 2, 3, 1), mv.astype(BF16).transpose(0, 2, 1, 3), wts, causal=True)
        for lst, val in zip(outs[:6], (ka, va, ki, ub[:, -POOL_HIST:], mk, mv)):
            lst.append(val)

        bs = xs.shape[0]
        hist0 = jnp.concatenate([jnp.zeros((bs, HIST_ROWS - POOL_HIST, D_POOL), F32), cache_pool[l]], axis=1)
        xs, ka, va, ki, ub = _group_forward(
            xs, pos_s, past, hist0, cache_a_k[l], cache_a_v[l], cache_idx_k[l],
            cache_mem_k[l].astype(BF16).transpose(0, 2, 3, 1), cache_mem_v[l].astype(BF16).transpose(0, 2, 1, 3),
            wts, causal=False)
        pool_s = jnp.concatenate([cache_pool[l], ub], axis=1)[:, -POOL_HIST:]
        for lst, val in zip(outs[6:], (ka, va, ki, pool_s)):
            lst.append(val)

    stacked = [jnp.stack(o) for o in outs]
    return (xp, xs, *stacked)
```
